```python
import jax, jax.numpy as jnp
from jax import lax
import numpy as np

D_MODEL = 1024
BATCH = 2
SEQ = 8192
DEPTH = 1
DEC_BATCH = 32
DEC_SEQ = 1
PAST_LEN = 8192
PAGE_SIZE = 128

HEAD_DIM = 64
N_HEADS_A = 8
N_HEADS_B = 8
WIDTH_A = N_HEADS_A * HEAD_DIM
WIDTH_B = N_HEADS_B * HEAD_DIM
MIX_WIDTH = WIDTH_A + WIDTH_B
MOBA_BLOCK = 256
MOBA_TOPK = 3
DSA_TOPK = 256
IDX_HEADS = 8
IDX_DIM = 64
D_FF = 2816
CONV_WIDTH = 3
Q_BLOCK = 64
EPS = 1e-6
N_MOD = 6
ATTN_SCALE = HEAD_DIM ** -0.5
PROJ_COLS = 3 * WIDTH_A + 3 * WIDTH_B + IDX_HEADS * IDX_DIM + IDX_DIM + IDX_HEADS

kernel_name = "hymba_moba_dsa_convffn_adaln_step"


def _alibi_slopes():
    n = N_HEADS_A + N_HEADS_B
    i = jnp.arange(1, n + 1, dtype=jnp.float32)
    m = jnp.exp2(-8.0 * i / n)
    return m[0::2], m[1::2]


def _rms(x, g):
    xf = x.astype(jnp.float32)
    y = xf * lax.rsqrt(jnp.mean(xf * xf, axis=-1, keepdims=True) + EPS)
    return (y * g.astype(jnp.float32)).astype(x.dtype)


def _modulation(c, w_ada, b_ada):
    mod = jax.nn.silu(c) @ w_ada + b_ada
    return jnp.split(mod[:, None, :], N_MOD, axis=-1)


def _project(h, w_in, qn_a, kn_a, qn_b, kn_b):
    B, T, _ = h.shape
    p = h @ w_in
    cuts = np.cumsum([WIDTH_A] * 3 + [WIDTH_B] * 3 + [IDX_HEADS * IDX_DIM, IDX_DIM]).tolist()
    qa, ka, va, qb, kb, vb, qi, ki, wi = jnp.split(p, cuts, axis=-1)
    qa = _rms(qa.reshape(B, T, N_HEADS_A, HEAD_DIM), qn_a)
    ka = _rms(ka.reshape(B, T, N_HEADS_A, HEAD_DIM), kn_a)
    va = va.reshape(B, T, N_HEADS_A, HEAD_DIM)
    qb = _rms(qb.reshape(B, T, N_HEADS_B, HEAD_DIM), qn_b)
    kb = _rms(kb.reshape(B, T, N_HEADS_B, HEAD_DIM), kn_b)
    vb = vb.reshape(B, T, N_HEADS_B, HEAD_DIM)
    qi = qi.reshape(B, T, IDX_HEADS, IDX_DIM)
    wi = wi * (IDX_HEADS * IDX_DIM) ** -0.5
    return qa, ka, va, qb, kb, vb, qi, ki, wi


def _sparse_attend(q, q_pos, key_pos, valid, kg, vg, slopes):
    logits = jnp.einsum('thd,thnd->thn', q, kg).astype(jnp.float32) * ATTN_SCALE
    dist = (q_pos[:, None, None] - key_pos).astype(jnp.float32)
    logits = jnp.where(valid, logits - slopes[None, :, None] * dist, -jnp.inf)
    p = jax.nn.softmax(logits, axis=-1)
    return jnp.einsum('thn,thnd->thd', p.astype(vg.dtype), vg)


def _moba_select(q, q_pos, kmean):
    T, H, _ = q.shape
    nbf = kmean.shape[0]
    own = q_pos // MOBA_BLOCK
    s = jnp.einsum('thd,nhd->thn', q, kmean).astype(jnp.float32)
    s = jnp.where(jnp.arange(nbf)[None, None, :] < own[:, None, None], s, -jnp.inf)
    s = jnp.concatenate([s, jnp.full((T, H, MOBA_TOPK), -jnp.inf, jnp.float32)], axis=-1)
    top_v, top_i = lax.top_k(s, MOBA_TOPK)
    blocks = jnp.concatenate([top_i, jnp.broadcast_to(own[:, None, None], (T, H, 1))], axis=-1)
    bvalid = jnp.concatenate([top_v > -jnp.inf, jnp.ones((T, H, 1), bool)], axis=-1)
    kp = blocks[..., None] * MOBA_BLOCK + jnp.arange(MOBA_BLOCK)
    valid = bvalid[..., None] & (kp <= q_pos[:, None, None, None])
    return kp.reshape(T, H, -1), valid.reshape(T, H, -1)


def _dsa_select(qi, wi, q_pos, kidx, n_sel):
    rel = jax.nn.relu(jnp.einsum('tid,sd->tis', qi, kidx).astype(jnp.float32))
    score = jnp.einsum('tis,ti->ts', rel, wi.astype(jnp.float32))
    L = kidx.shape[0]
    score = jnp.where(jnp.arange(L)[None, :] <= q_pos[:, None], score, -jnp.inf)
    _, idx = lax.top_k(score, n_sel)
    return idx, idx <= q_pos[:, None]


def _moba_prompt(q, k, v, slopes):
    B, T, H, Dh = q.shape
    nbf = T // MOBA_BLOCK
    kmean = k[:, :nbf * MOBA_BLOCK].reshape(B, nbf, MOBA_BLOCK, H, Dh).astype(jnp.float32).mean(2).astype(k.dtype)
    nq = T // Q_BLOCK
    hid = jnp.arange(H)[None, :, None]

    def one(args):
        b, c = args
        qc = lax.dynamic_slice_in_dim(q[b], c * Q_BLOCK, Q_BLOCK, 0)
        pos = c * Q_BLOCK + jnp.arange(Q_BLOCK)
        kp, valid = _moba_select(qc, pos, kmean[b])
        kc = jnp.clip(kp, 0, T - 1)
        return _sparse_attend(qc, pos, kp, valid, k[b, kc, hid], v[b, kc, hid], slopes)

    bs = jnp.repeat(jnp.arange(B), nq)
    cs = jnp.tile(jnp.arange(nq), B)
    out = lax.map(one, (bs, cs))
    return out.reshape(B, T, H * Dh)


def _dsa_prompt(q, k, v, qi, ki, wi, slopes):
    B, T, H, Dh = q.shape
    n_sel = min(DSA_TOPK, T // 4)
    nq = T // Q_BLOCK
    hid = jnp.arange(H)[None, :, None]

    def one(args):
        b, c = args
        sl = lambda a: lax.dynamic_slice_in_dim(a[b], c * Q_BLOCK, Q_BLOCK, 0)
        qc, qic, wic = sl(q), sl(qi), sl(wi)
        pos = c * Q_BLOCK + jnp.arange(Q_BLOCK)
        idx, valid = _dsa_select(qic, wic, pos, ki[b], n_sel)
        kp = jnp.broadcast_to(idx[:, None, :], (Q_BLOCK, H, n_sel))
        vm = jnp.broadcast_to(valid[:, None, :], (Q_BLOCK, H, n_sel))
        return _sparse_attend(qc, pos, kp, vm, k[b, kp, hid], v[b, kp, hid], slopes)

    bs = jnp.repeat(jnp.arange(B), nq)
    cs = jnp.tile(jnp.arange(nq), B)
    out = lax.map(one, (bs, cs))
    return out.reshape(B, T, H * Dh)


def _gather_pages(pool, page_table):
    g = pool[page_table]
    return g.reshape((g.shape[0], g.shape[1] * g.shape[2]) + g.shape[3:])


def _fetch_rows(pool, new_rows, pt, key_pos):
    past_len = pt.shape[0] * PAGE_SIZE
    ds = new_rows.shape[0]
    hid = jnp.arange(pool.shape[2])[None, :, None]
    pp = jnp.clip(key_pos, 0, past_len - 1)
    from_pool = pool[pt[pp // PAGE_SIZE], pp % PAGE_SIZE, hid]
    from_new = new_rows[jnp.clip(key_pos - past_len, 0, ds - 1), hid]
    return jnp.where((key_pos >= past_len)[..., None], from_new, from_pool)


def _moba_sample(q, k_new, v_new, pool_k, pool_v, page_table, slopes):
    DB, T, H, Dh = q.shape
    past_len = page_table.shape[1] * PAGE_SIZE
    L = past_len + T
    k_full = jnp.concatenate([_gather_pages(pool_k, page_table), k_new], axis=1)
    nbf = L // MOBA_BLOCK
    kmean = k_full[:, :nbf * MOBA_BLOCK].reshape(DB, nbf, MOBA_BLOCK, H, Dh).astype(jnp.float32).mean(2).astype(q.dtype)
    pos = past_len + jnp.arange(T)

    def one(qb, kmb, ptb, knb, vnb):
        kp, valid = _moba_select(qb, pos, kmb)
        kg = _fetch_rows(pool_k, knb, ptb, kp)
        vg = _fetch_rows(pool_v, vnb, ptb, kp)
        return _sparse_attend(qb, pos, kp, valid, kg, vg, slopes)

    out = jax.vmap(one)(q, kmean, page_table, k_new, v_new)
    return out.reshape(DB, T, H * Dh)


def _dsa_sample(q, k_new, v_new, qi, ki_new, wi, pool_k, pool_v, pool_kidx, page_table, slopes):
    DB, T, H, Dh = q.shape
    past_len = page_table.shape[1] * PAGE_SIZE
    L = past_len + T
    n_sel = min(DSA_TOPK, L // 4)
    kidx_full = jnp.concatenate([_gather_pages(pool_kidx, page_table), ki_new], axis=1)
    pos = past_len + jnp.arange(T)

    def one(qb, qib, wib, kib, ptb, knb, vnb):
        idx, valid = _dsa_select(qib, wib, pos, kib, n_sel)
        kp = jnp.broadcast_to(idx[:, None, :], (T, H, n_sel))
        vm = jnp.broadcast_to(valid[:, None, :], (T, H, n_sel))
        kg = _fetch_rows(pool_k, knb, ptb, kp)
        vg = _fetch_rows(pool_v, vnb, ptb, kp)
        return _sparse_attend(qb, pos, kp, vm, kg, vg, slopes)

    out = jax.vmap(one)(q, qi, wi, kidx_full, page_table, k_new, v_new)
    return out.reshape(DB, T, H * Dh)


def _conv_ffn(h, prefix, w_up, conv_w, conv_b, w_down):
    a, g = jnp.split(h @ w_up, 2, axis=-1)
    T = a.shape[1]
    ap = jnp.concatenate([prefix.astype(a.dtype), a], axis=1)
    conv = conv_b + ap[:, 0:T] * conv_w[0]
    for j in range(1, CONV_WIDTH):
        conv = conv + ap[:, j:j + T] * conv_w[j]
    y = (jax.nn.gelu(conv) * g) @ w_down
    return y, ap[:, T:]


def _pre(x, c, w_ada, b_ada, g_attn, w_in, qn_a, kn_a, qn_b, kn_b):
    mods = _modulation(c, w_ada, b_ada)
    shift_a, scale_a = mods[0], mods[1]
    h = _rms(x, g_attn) * (1 + scale_a) + shift_a
    return mods, _project(h, w_in, qn_a, kn_a, qn_b, kn_b)


def _finish(x, mods, o_mix, conv_prefix, w_out, g_ffn, w_up, conv_w, conv_b, w_down):
    gate_a, shift_f, scale_f, gate_f = mods[2], mods[3], mods[4], mods[5]
    x = x + gate_a * (o_mix @ w_out)
    h = _rms(x, g_ffn) * (1 + scale_f) + shift_f
    y, conv_state = _conv_ffn(h, conv_prefix, w_up, conv_w, conv_b, w_down)
    return x + gate_f * y, conv_state


def setup_inputs(seed: int = 0) -> dict:
    key = jax.random.key(seed)
    ks = jax.random.split(key, 26)
    n_pages = PAST_LEN // PAGE_SIZE
    n_phys = (DEC_BATCH * n_pages * 5) // 4
    nrm = lambda k, shape, s=1.0: s * jax.random.normal(k, shape, jnp.float32)
    page_table = jax.random.permutation(ks[0], n_phys)[: DEC_BATCH * n_pages].reshape(DEC_BATCH, n_pages).astype(jnp.int32)
    return {
        "x_prompt": nrm(ks[1], (BATCH, SEQ, D_MODEL)),
        "x_sample": nrm(ks[2], (DEC_BATCH, DEC_SEQ, D_MODEL)),
        "cache_moba_k": nrm(ks[3], (DEPTH, n_phys, PAGE_SIZE, N_HEADS_A, HEAD_DIM)),
        "cache_moba_v": nrm(ks[4], (DEPTH, n_phys, PAGE_SIZE, N_HEADS_A, HEAD_DIM)),
        "cache_dsa_k": nrm(ks[5], (DEPTH, n_phys, PAGE_SIZE, N_HEADS_B, HEAD_DIM)),
        "cache_dsa_v": nrm(ks[6], (DEPTH, n_phys, PAGE_SIZE, N_HEADS_B, HEAD_DIM)),
        "cache_dsa_kidx": nrm(ks[7], (DEPTH, n_phys, PAGE_SIZE, IDX_DIM)),
        "state_ffn_conv": nrm(ks[8], (DEPTH, DEC_BATCH, CONV_WIDTH - 1, D_FF)),
        "page_table": page_table,
        "c_prompt": nrm(ks[9], (BATCH, D_MODEL)),
        "c_sample": nrm(ks[10], (DEC_BATCH, D_MODEL)),
        "w_ada": nrm(ks[11], (DEPTH, D_MODEL, N_MOD * D_MODEL), 0.5 * D_MODEL ** -0.5),
        "b_ada": nrm(ks[12], (DEPTH, N_MOD * D_MODEL), 0.02),
        "g_attn": 1.0 + nrm(ks[13], (DEPTH, D_MODEL), 0.02),
        "w_in": nrm(ks[14], (DEPTH, D_MODEL, PROJ_COLS), D_MODEL ** -0.5),
        "qn_a": 1.0 + nrm(ks[15], (DEPTH, HEAD_DIM), 0.02),
        "kn_a": 1.0 + nrm(ks[16], (DEPTH, HEAD_DIM), 0.02),
        "qn_b": 1.0 + nrm(ks[17], (DEPTH, HEAD_DIM), 0.02),
        "kn_b": 1.0 + nrm(ks[18], (DEPTH, HEAD_DIM), 0.02),
        "w_out": nrm(ks[19], (DEPTH, MIX_WIDTH, D_MODEL), MIX_WIDTH ** -0.5),
        "g_ffn": 1.0 + nrm(ks[20], (DEPTH, D_MODEL), 0.02),
        "w_up": nrm(ks[21], (DEPTH, D_MODEL, 2 * D_FF), D_MODEL ** -0.5),
        "conv_w": nrm(ks[22], (DEPTH, CONV_WIDTH, D_FF), CONV_WIDTH ** -0.5),
        "conv_b": nrm(ks[23], (DEPTH, D_FF), 0.02),
        "w_down": nrm(ks[24], (DEPTH, D_FF, D_MODEL), D_FF ** -0.5),
    }


def reference(x_prompt, x_sample, cache_moba_k, cache_moba_v, cache_dsa_k, cache_dsa_v, cache_dsa_kidx,
              state_ffn_conv, page_table, c_prompt, c_sample, w_ada, b_ada, g_attn, w_in, qn_a, kn_a,
              qn_b, kn_b, w_out, g_ffn, w_up, conv_w, conv_b, w_down):
    slopes_a, slopes_b = _alibi_slopes()
    xp, xs = x_prompt, x_sample
    mk_p, mv_p, dk_p, dv_p, di_p, cv_p = [], [], [], [], [], []
    mk_s, mv_s, dk_s, dv_s, di_s, cv_s = [], [], [], [], [], []
    for l in range(DEPTH):
        attn_w = (w_ada[l], b_ada[l], g_attn[l], w_in[l], qn_a[l], kn_a[l], qn_b[l], kn_b[l])
        ffn_w = (w_out[l], g_ffn[l], w_up[l], conv_w[l], conv_b[l], w_down[l])
        mods, (qa, ka, va, qb, kb, vb, qi, ki, wi) = _pre(xp, c_prompt, *attn_w)
        o_a = _moba_prompt(qa, ka, va, slopes_a)
        o_b = _dsa_prompt(qb, kb, vb, qi, ki, wi, slopes_b)
        zeros = jnp.zeros((xp.shape[0], CONV_WIDTH - 1, D_FF), xp.dtype)
        xp, conv_p = _finish(xp, mods, jnp.concatenate([o_a, o_b], axis=-1), zeros, *ffn_w)
        mk_p.append(ka); mv_p.append(va); dk_p.append(kb); dv_p.append(vb); di_p.append(ki); cv_p.append(conv_p)
        mods_s, (qa_s, ka_s, va_s, qb_s, kb_s, vb_s, qi_s, ki_s, wi_s) = _pre(xs, c_sample, *attn_w)
        o_a_s = _moba_sample(qa_s, ka_s, va_s, cache_moba_k[l], cache_moba_v[l], page_table, slopes_a)
        o_b_s = _dsa_sample(qb_s, kb_s, vb_s, qi_s, ki_s, wi_s, cache_dsa_k[l], cache_dsa_v[l],
                            cache_dsa_kidx[l], page_table, slopes_b)
        xs, conv_s = _finish(xs, mods_s, jnp.concatenate([o_a_s, o_b_s], axis=-1), state_ffn_conv[l], *ffn_w)
        mk_s.append(ka_s); mv_s.append(va_s); dk_s.append(kb_s); dv_s.append(vb_s); di_s.append(ki_s); cv_s.append(conv_s)
    return (xp, xs,
            jnp.stack(mk_p), jnp.stack(mv_p), jnp.stack(dk_p), jnp.stack(dv_p), jnp.stack(di_p), jnp.stack(cv_p),
            jnp.stack(mk_s), jnp.stack(mv_s), jnp.stack(dk_s), jnp.stack(dv_s), jnp.stack(di_s), jnp.stack(cv_s))
```

```python
import functools

import jax
import jax.numpy as jnp
from jax import lax
from jax.experimental import pallas as pl
from jax.experimental.pallas import tpu as pltpu

HEAD_DIM = 64
N_HEADS = 8
WIDTH = N_HEADS * HEAD_DIM
MOBA_BLOCK = 256
MOBA_TOPK = 3
DSA_TOPK = 256
IDX_HEADS = 8
IDX_DIM = 64
CONV_WIDTH = 3
EPS = 1e-6
N_MOD = 6
ATTN_SCALE = HEAD_DIM ** -0.5
N_QKV_CHUNKS = 7
LANES = 128
SUBLANES = 8
TQ = MOBA_BLOCK
FF_CHUNK = 256
VMEM_LIMIT = 56 * 1024 * 1024

BF = jnp.bfloat16
F32 = jnp.float32
I32 = jnp.int32
NEG_INF = float("-inf")
INT_MIN = -2 ** 31


def _cp(sem):
    return pltpu.CompilerParams(dimension_semantics=sem, vmem_limit_bytes=VMEM_LIMIT)


def _const_spec(shape):
    nd = len(shape)
    return pl.BlockSpec(shape, lambda *_: (0,) * nd)


def _mod_kernel(c_ref, w_ref, b_ref, o_ref):
    s = jax.nn.silu(c_ref[...])
    o_ref[...] = jnp.dot(s.astype(BF), w_ref[...].astype(BF), preferred_element_type=F32) + b_ref[...]


def _modulation(c_all, w_ada, b_ada):
    r, d = c_all.shape
    n = w_ada.shape[1]
    tn = 1024
    return pl.pallas_call(
        _mod_kernel,
        grid=(n // tn,),
        in_specs=[pl.BlockSpec((r, d), lambda j: (0, 0)),
                  pl.BlockSpec((d, tn), lambda j: (0, j)),
                  pl.BlockSpec((1, tn), lambda j: (0, j))],
        out_specs=pl.BlockSpec((r, tn), lambda j: (0, j)),
        out_shape=jax.ShapeDtypeStruct((r, n), F32),
        compiler_params=_cp(("arbitrary",)),
        name="modulation",
    )(c_all, w_ada, b_ada.reshape(1, n))


def _pre_kernel(x_ref, shift_ref, scale_ref, g_ref, wqkv_ref, wkw_ref, nrm_ref, gmat_ref, *outs, tm, with_kmean):
    x = x_ref[...]
    ms = jnp.mean(x * x, axis=-1, keepdims=True)
    xn = x * lax.rsqrt(ms + EPS) * g_ref[...]
    h = (xn * (1.0 + scale_ref[0]) + shift_ref[0]).astype(BF)
    norm_row = {0: 0, 1: 1, 3: 2, 4: 3}
    for c in range(N_QKV_CHUNKS):
        p = jnp.dot(h, wqkv_ref[:, c * WIDTH:(c + 1) * WIDTH], preferred_element_type=F32)
        if c in norm_row:
            ss = jnp.dot((p * p).astype(BF), gmat_ref[...], preferred_element_type=F32)
            k = norm_row[c]
            p = p * lax.rsqrt(ss * (1.0 / HEAD_DIM) + EPS) * nrm_ref[k:k + 1, :]
        outs[c][...] = p
        if c == 1 and with_kmean:
            outs[N_QKV_CHUNKS + 1][0] = jnp.mean(p.reshape(tm // MOBA_BLOCK, MOBA_BLOCK, WIDTH), axis=1)
    outs[N_QKV_CHUNKS][...] = jnp.dot(h, wkw_ref[...], preferred_element_type=F32)


def _pre_project(x2, shift, scale, g_attn, wqkv, wkw, nrm, gmat, *, tm, rows_per_group, with_kmean):
    n, d = x2.shape
    grid = (n // tm,)
    mr = shift.shape[1]
    tiles_per_group = rows_per_group // tm
    mod_spec = pl.BlockSpec((1, mr, d), lambda i: (i // tiles_per_group, 0, 0))
    row_spec = lambda w: pl.BlockSpec((tm, w), lambda i: (i, 0))
    out_shape = [jax.ShapeDtypeStruct((n, WIDTH), F32)] * N_QKV_CHUNKS + [jax.ShapeDtypeStruct((n, LANES), F32)]
    out_specs = [row_spec(WIDTH)] * N_QKV_CHUNKS + [row_spec(LANES)]
    if with_kmean:
        nb = tm // MOBA_BLOCK
        out_shape.append(jax.ShapeDtypeStruct((n // tm, nb, WIDTH), F32))
        out_specs.append(pl.BlockSpec((1, nb, WIDTH), lambda i: (i, 0, 0)))
    return pl.pallas_call(
        functools.partial(_pre_kernel, tm=tm, with_kmean=with_kmean),
        grid=grid,
        in_specs=[row_spec(d), mod_spec, mod_spec, _const_spec((1, d)),
                  _const_spec(wqkv.shape), _const_spec(wkw.shape), _const_spec(nrm.shape), _const_spec(gmat.shape)],
        out_specs=out_specs,
        out_shape=out_shape,
        compiler_params=_cp(("arbitrary",)),
        name="pre_project",
    )(x2, shift, scale, g_attn, wqkv, wkw, nrm, gmat)


def _rel_alibi(slope, j, i):
    cols = lax.broadcasted_iota(I32, (1, TQ), 1) + (j - i) * TQ
    return slope * cols.astype(F32)


def _two_pass_attention(q, kt_at, v_at, slope, i, n_blocks, bias_at, l_ref, m_ref, acc_ref):
    m_ref[...] = jnp.full((TQ, TQ), NEG_INF, F32)

    def pass1(j, carry):
        l = jnp.dot(q, kt_at(j), preferred_element_type=F32) + _rel_alibi(slope, j, i) + bias_at(j)
        l_ref[j] = l
        m_ref[...] = jnp.maximum(m_ref[...], l)
        return carry

    lax.fori_loop(0, n_blocks, pass1, 0)
    m = jnp.max(m_ref[...], axis=-1, keepdims=True)
    m_ref[...] = jnp.broadcast_to(m, (TQ, TQ))
    acc_ref[...] = jnp.zeros((TQ, LANES), F32)

    def pass2(j, carry):
        p = jnp.exp(l_ref[j] - m_ref[...]).astype(BF)
        acc_ref[...] += jnp.dot(p, v_at(j), preferred_element_type=F32)
        return carry

    lax.fori_loop(0, n_blocks, pass2, 0)
    acc = acc_ref[...]
    return acc[:, :HEAD_DIM] / acc[:, HEAD_DIM:HEAD_DIM + 1]


def _causal_bias():
    r = lax.broadcasted_iota(I32, (TQ, TQ), 0)
    c = lax.broadcasted_iota(I32, (TQ, TQ), 1)
    return jnp.where(c <= r, 0.0, NEG_INF).astype(F32)


def _moba_prompt_kernel(slopes_ref, q_ref, kt_ref, v_ref, kmt_ref, o_ref, l_ref, m_ref, acc_ref, bias_ref):
    h = pl.program_id(1)
    i = pl.program_id(2)
    slope = slopes_ref[h]
    q = q_ref[0, 0]
    s = jnp.dot(q, kmt_ref[0, 0], preferred_element_type=F32)
    lane = lax.broadcasted_iota(I32, s.shape, 1)
    lanef = lane.astype(F32)
    s = jnp.where(lane < i, s, NEG_INF)
    bias = jnp.full(s.shape, NEG_INF, F32)
    for _ in range(MOBA_TOPK):
        mx = jnp.max(s, axis=-1, keepdims=True)
        first = jnp.min(jnp.where(s == mx, lanef, float(LANES)), axis=-1, keepdims=True)
        pick = (lanef == first) & (mx > NEG_INF)
        bias = jnp.where(pick, 0.0, bias)
        s = jnp.where(pick, NEG_INF, s)
    causal = _causal_bias()
    bias_ref[...] = bias

    def bias_at(j):
        col = jnp.max(jnp.where(lane == j, bias_ref[...], NEG_INF), axis=-1, keepdims=True)
        return jnp.where(j == i, causal, col)

    o_ref[0, 0] = _two_pass_attention(
        q, lambda j: kt_ref[0, 0, j], lambda j: v_ref[0, 0, j], slope, i, i + 1, bias_at, l_ref, m_ref, acc_ref)


def _moba_prompt(slopes, q, kt, v, kmt):
    b, h, t, _ = q.shape
    nq = t // TQ
    grid_spec = pltpu.PrefetchScalarGridSpec(
        num_scalar_prefetch=1,
        grid=(b, h, nq),
        in_specs=[pl.BlockSpec((1, 1, TQ, HEAD_DIM), lambda bi, hi, i, s: (bi, hi, i, 0)),
                  pl.BlockSpec((1, 1, nq, HEAD_DIM, TQ), lambda bi, hi, i, s: (bi, hi, 0, 0, 0)),
                  pl.BlockSpec((1, 1, nq, TQ, LANES), lambda bi, hi, i, s: (bi, hi, 0, 0, 0)),
                  pl.BlockSpec((1, 1, HEAD_DIM, LANES), lambda bi, hi, i, s: (bi, hi, 0, 0))],
        out_specs=pl.BlockSpec((1, 1, TQ, HEAD_DIM), lambda bi, hi, i, s: (bi, hi, i, 0)),
        scratch_shapes=[pltpu.VMEM((nq, TQ, TQ), F32), pltpu.VMEM((TQ, TQ), F32),
                        pltpu.VMEM((TQ, LANES), F32), pltpu.VMEM((TQ, LANES), F32)],
    )
    return pl.pallas_call(
        _moba_prompt_kernel,
        grid_spec=grid_spec,
        out_shape=jax.ShapeDtypeStruct((b, h, t, HEAD_DIM), F32),
        compiler_params=_cp(("arbitrary", "arbitrary", "arbitrary")),
        name="moba_prompt",
    )(slopes, q, kt, v, kmt)


def _order_key(score):
    bits = lax.bitcast_convert_type(score, I32)
    return bits ^ ((bits >> 31) & 0x7FFFFFFF)


def _dsa_select(i, n_sel, idx_bits, sk_ref, b_ref, cb_ref, cb2_ref, cnt_ref):
    ones = jnp.ones((TQ, LANES), BF)
    r = lax.broadcasted_iota(I32, (TQ, TQ), 0)
    c = lax.broadcasted_iota(I32, (TQ, TQ), 1)

    def count(pred):
        cnt_ref[...] = jnp.zeros((TQ, LANES), F32)

        def body(j, carry):
            hit = jnp.where(pred(j, sk_ref[j]), 1.0, 0.0).astype(BF)
            cnt_ref[...] += jnp.dot(hit, ones, preferred_element_type=F32)
            return carry

        lax.fori_loop(0, i + 1, body, 0)
        return cnt_ref[:, 0:1]

    def bit_step(it, v):
        cand = v + jnp.left_shift(jnp.int32(1), 31 - it)
        cb_ref[...] = jnp.broadcast_to(cand, (TQ, TQ))
        cnt = count(lambda j, key: key >= cb_ref[...])
        return jnp.where(cnt >= n_sel, cand, v)

    v = lax.fori_loop(0, 32, bit_step, jnp.full((TQ, 1), INT_MIN, I32))
    cb_ref[...] = jnp.broadcast_to(v, (TQ, TQ))
    cnt_gt = count(lambda j, key: key > cb_ref[...])
    cnt_ge = count(lambda j, key: key >= cb_ref[...])
    need = n_sel - cnt_gt
    trim = (cnt_ge > n_sel) & (v > INT_MIN)
    cb2_ref[...] = jnp.full((TQ, TQ), 2 ** idx_bits, I32)

    @pl.when(jnp.max(jnp.where(trim, 1.0, 0.0)) > 0.0)
    def _():
        def idx_step(it, jcap):
            cand = jcap + jnp.left_shift(jnp.int32(1), idx_bits - 1 - it)
            cb2_ref[...] = jnp.broadcast_to(cand, (TQ, TQ))
            cnt = count(lambda j, key: (key == cb_ref[...]) & (c + j * TQ < cb2_ref[...]))
            return jnp.where(cnt < need, cand, jcap)

        jcap = lax.fori_loop(0, idx_bits, idx_step, jnp.zeros((TQ, 1), I32))
        jcap = jnp.where(trim, jcap, 2 ** idx_bits)
        cb2_ref[...] = jnp.broadcast_to(jcap, (TQ, TQ))

    def write(j, carry):
        key = sk_ref[j]
        sel = (key > cb_ref[...]) | ((key == cb_ref[...]) & (c + j * TQ <= cb2_ref[...]))
        sel = sel & ((j < i) | (c <= r))
        b_ref[j] = jnp.where(sel, 0.0, NEG_INF)
        return carry

    lax.fori_loop(0, i + 1, write, 0)


def _dsa_prompt_kernel(slopes_ref, qi_ref, kw_ref, kit_ref, q_ref, kt_ref, v_ref, o_ref,
                       sk_ref, b_ref, l_ref, w_ref, m_ref, cb_ref, cb2_ref, acc_ref, cnt_ref, *, n_sel, idx_bits):
    i = pl.program_id(1)
    h = pl.program_id(2)

    @pl.when(h == 0)
    def _():
        qi = qi_ref[0].reshape(IDX_HEADS * TQ, IDX_DIM)
        wi = kw_ref[...] * (IDX_HEADS * IDX_DIM) ** -0.5
        for hh in range(IDX_HEADS):
            w_ref[hh] = jnp.broadcast_to(wi[:, IDX_DIM + hh:IDX_DIM + hh + 1], (TQ, TQ))
        r = lax.broadcasted_iota(I32, (TQ, TQ), 0)
        c = lax.broadcasted_iota(I32, (TQ, TQ), 1)

        def score_blk(j, carry):
            rel = jnp.maximum(jnp.dot(qi, kit_ref[0, j], preferred_element_type=F32), 0.0)
            sc = rel[0:TQ] * w_ref[0]
            for hh in range(1, IDX_HEADS):
                sc = sc + rel[hh * TQ:(hh + 1) * TQ] * w_ref[hh]
            key = _order_key(sc)
            sk_ref[j] = jnp.where((j < i) | (c <= r), key, INT_MIN)
            return carry

        lax.fori_loop(0, i + 1, score_blk, 0)
        _dsa_select(i, n_sel, idx_bits, sk_ref, b_ref, cb_ref, cb2_ref, cnt_ref)

    o_ref[0, 0] = _two_pass_attention(
        q_ref[0, 0], lambda j: kt_ref[0, 0, j], lambda j: v_ref[0, 0, j], slopes_ref[h], i, i + 1,
        lambda j: b_ref[j], l_ref, m_ref, acc_ref)


def _dsa_prompt(slopes, qi, kw, kit, q, kt, v, n_sel):
    b, h, t, _ = q.shape
    nq = t // TQ
    idx_bits = max(1, (t - 1).bit_length())
    blk = lambda shape, fn: pl.BlockSpec(shape, fn)
    grid_spec = pltpu.PrefetchScalarGridSpec(
        num_scalar_prefetch=1,
        grid=(b, nq, h),
        in_specs=[blk((1, IDX_HEADS, TQ, IDX_DIM), lambda bi, i, hi, s: (bi, 0, i, 0)),
                  blk((TQ, LANES), lambda bi, i, hi, s: (bi * nq + i, 0)),
                  blk((1, nq, IDX_DIM, TQ), lambda bi, i, hi, s: (bi, 0, 0, 0)),
                  blk((1, 1, TQ, HEAD_DIM), lambda bi, i, hi, s: (bi, hi, i, 0)),
                  blk((1, 1, nq, HEAD_DIM, TQ), lambda bi, i, hi, s: (bi, hi, 0, 0, 0)),
                  blk((1, 1, nq, TQ, LANES), lambda bi, i, hi, s: (bi, hi, 0, 0, 0))],
        out_specs=blk((1, 1, TQ, HEAD_DIM), lambda bi, i, hi, s: (bi, hi, i, 0)),
        scratch_shapes=[pltpu.VMEM((nq, TQ, TQ), I32), pltpu.VMEM((nq, TQ, TQ), F32), pltpu.VMEM((nq, TQ, TQ), F32),
                        pltpu.VMEM((IDX_HEADS, TQ, TQ), F32), pltpu.VMEM((TQ, TQ), F32),
                        pltpu.VMEM((TQ, TQ), I32), pltpu.VMEM((TQ, TQ), I32),
                        pltpu.VMEM((TQ, LANES), F32), pltpu.VMEM((TQ, LANES), F32)],
    )
    return pl.pallas_call(
        functools.partial(_dsa_prompt_kernel, n_sel=n_sel, idx_bits=idx_bits),
        grid_spec=grid_spec,
        out_shape=jax.ShapeDtypeStruct((b, h, t, HEAD_DIM), F32),
        compiler_params=_cp(("arbitrary", "arbitrary", "arbitrary")),
        name="dsa_prompt",
    )(slopes, qi, kw, kit, q, kt, v)


def _finish_kernel(*refs, tm, nchunk, seq_mode, tiles_per_seq):
    (x_ref, oa_ref, ob_ref, gate_a_ref, shift_f_ref, scale_f_ref, gate_f_ref, g_ref, wout_ref,
     wupa_ref, wupg_ref, cw_ref, cb_ref, wdn_ref) = refs[:14]
    if seq_mode:
        y_ref, st_ref, abuf_ref, carry_ref, yacc_ref = refs[14:]
    else:
        pre0_ref, pre1_ref, y_ref, st_ref, yacc_ref = refs[14:]
    i = pl.program_id(0)
    o = (jnp.dot(oa_ref[...].astype(BF), wout_ref[0:WIDTH, :], preferred_element_type=F32)
         + jnp.dot(ob_ref[...].astype(BF), wout_ref[WIDTH:2 * WIDTH, :], preferred_element_type=F32))
    x1 = x_ref[...] + gate_a_ref[0] * o
    ms = jnp.mean(x1 * x1, axis=-1, keepdims=True)
    h = ((x1 * lax.rsqrt(ms + EPS) * g_ref[...]) * (1.0 + scale_f_ref[0]) + shift_f_ref[0]).astype(BF)
    yacc_ref[...] = jnp.zeros_like(yacc_ref)
    for c in range(nchunk):
        cols = slice(c * FF_CHUNK, (c + 1) * FF_CHUNK)
        a = jnp.dot(h, wupa_ref[c], preferred_element_type=F32)
        gt = jnp.dot(h, wupg_ref[c], preferred_element_type=F32)
        if seq_mode:
            first = (i % tiles_per_seq) == 0
            buf = abuf_ref.at[c % 2]
            buf[0:SUBLANES, :] = jnp.where(first, 0.0, carry_ref[c])
            buf[SUBLANES:SUBLANES + tm, :] = a
            a1 = buf[SUBLANES - 1:SUBLANES - 1 + tm, :]
            a2 = buf[SUBLANES - 2:SUBLANES - 2 + tm, :]
            carry_ref[c] = a[tm - SUBLANES:tm, :]
            st_ref[0, :, cols] = a[tm - (CONV_WIDTH - 1):tm, :]
        else:
            a2 = pre0_ref[:, cols]
            a1 = pre1_ref[:, cols]
            st_ref[:, cols] = a
        conv = cb_ref[:, cols] + a2 * cw_ref[0:1, cols]
        conv = conv + a1 * cw_ref[1:2, cols]
        conv = conv + a * cw_ref[2:3, cols]
        act = (jax.nn.gelu(conv) * gt).astype(BF)
        yacc_ref[...] += jnp.dot(act, wdn_ref[c], preferred_element_type=F32)
    y_ref[...] = x1 + gate_f_ref[0] * yacc_ref[...]


def _finish(x2, oa, ob, mods, g_ffn, wout, wupa, wupg, conv_w, conv_b, wdn, *, tm, rows_per_group, prefix=None):
    n, d = x2.shape
    nchunk = wupa.shape[0]
    f = nchunk * FF_CHUNK
    seq_mode = prefix is None
    tiles_per_seq = rows_per_group // tm
    mr = mods[0].shape[1]
    mod_spec = pl.BlockSpec((1, mr, d), lambda i: (i // tiles_per_seq, 0, 0))
    row_spec = lambda w: pl.BlockSpec((tm, w), lambda i: (i, 0))
    in_specs = ([row_spec(d), row_spec(WIDTH), row_spec(WIDTH)] + [mod_spec] * 4 +
                [_const_spec((1, d)), _const_spec(wout.shape), _const_spec(wupa.shape), _const_spec(wupg.shape),
                 _const_spec(conv_w.shape), _const_spec((1, f)), _const_spec(wdn.shape)])
    args = [x2, oa, ob, *mods, g_ffn, wout, wupa, wupg, conv_w, conv_b, wdn]
    scratch = []
    if seq_mode:
        nseq = n // rows_per_group
        st_shape = jax.ShapeDtypeStruct((nseq, CONV_WIDTH - 1, f), F32)
        st_spec = pl.BlockSpec((1, CONV_WIDTH - 1, f), lambda i: (i // tiles_per_seq, 0, 0))
        scratch += [pltpu.VMEM((2, SUBLANES + tm, FF_CHUNK), F32), pltpu.VMEM((nchunk, SUBLANES, FF_CHUNK), F32)]
    else:
        in_specs += [row_spec(f), row_spec(f)]
        args += list(prefix)
        st_shape = jax.ShapeDtypeStruct((n, f), F32)
        st_spec = row_spec(f)
    scratch.append(pltpu.VMEM((tm, d), F32))
    return pl.pallas_call(
        functools.partial(_finish_kernel, tm=tm, nchunk=nchunk, seq_mode=seq_mode, tiles_per_seq=tiles_per_seq),
        grid=(n // tm,),
        in_specs=in_specs,
        out_specs=[row_spec(d), st_spec],
        out_shape=[jax.ShapeDtypeStruct((n, d), F32), st_shape],
        scratch_shapes=scratch,
        compiler_params=_cp(("arbitrary",)),
        name="finish",
    )(*args)


PAGES_PER_STEP = 4


def _block_diag_rows(row):
    hrow = lax.broadcasted_iota(I32, (N_HEADS, WIDTH), 0)
    hlane = lax.broadcasted_iota(I32, (N_HEADS, WIDTH), 1) // HEAD_DIM
    return jnp.where(hrow == hlane, jnp.broadcast_to(row, (N_HEADS, WIDTH)), 0.0)


def _dot_nt(a, b):
    return lax.dot_general(a, b, (((1,), (1,)), ((), ())), preferred_element_type=F32)


def _sample_stats_kernel(pt_ref, qi_ref, wi_ref, kin_ref, qa_ref, *refs, page, nblk):
    pps = PAGES_PER_STEP
    kidx_refs = refs[:pps]
    kmoba_refs = refs[pps:2 * pps]
    sc_ref, scn_ref, km_ref, bb_ref = refs[2 * pps:]
    s = pl.program_id(1)
    qi = qi_ref[0].astype(BF)
    wi = wi_ref[0] * (IDX_HEADS * IDX_DIM) ** -0.5
    ppb = MOBA_BLOCK // page
    for k in range(pps):
        rel = jnp.maximum(_dot_nt(qi, kidx_refs[k][0].astype(BF)), 0.0)
        sc_ref[0, :, k * page:(k + 1) * page] = jnp.sum(rel * wi, axis=0, keepdims=True)
    for kb in range(pps // ppb):
        tot = jnp.sum(kmoba_refs[kb * ppb][0], axis=0, keepdims=True)
        for k in range(1, ppb):
            tot = tot + jnp.sum(kmoba_refs[kb * ppb + k][0], axis=0, keepdims=True)
        km_ref[0, pl.ds(s * (pps // ppb) + kb, 1), :] = tot * (1.0 / MOBA_BLOCK)

    @pl.when(s == pl.num_programs(1) - 1)
    def _():
        kin = jnp.broadcast_to(kin_ref[0], (SUBLANES, IDX_DIM)).astype(BF)
        reln = jnp.maximum(_dot_nt(qi, kin), 0.0)
        scn = jnp.sum(reln[:, 0:1] * wi, axis=0, keepdims=True)
        scn_ref[0] = jnp.broadcast_to(scn, (1, LANES))
        qbd = _block_diag_rows(qa_ref[0]).astype(BF)
        sb = _dot_nt(qbd, km_ref[0].astype(BF))
        lanef = lax.broadcasted_iota(I32, sb.shape, 1).astype(F32)
        bias = jnp.full(sb.shape, NEG_INF, F32)
        for _ in range(MOBA_TOPK):
            mx = jnp.max(sb, axis=-1, keepdims=True)
            first = jnp.min(jnp.where(sb == mx, lanef, float(nblk)), axis=-1, keepdims=True)
            pick = (lanef == first) & (mx > NEG_INF)
            bias = jnp.where(pick, 0.0, bias)
            sb = jnp.where(pick, NEG_INF, sb)
        bb_ref[0] = jnp.full((N_HEADS, LANES), NEG_INF, F32)
        bb_ref[0, :, 0:nblk] = bias


def _sample_stats(page_table, qi_s, wi_s, ki_s, qa_s, pool_kidx, pool_mk):
    db, n_pages = page_table.shape
    page = pool_kidx.shape[1]
    pps = PAGES_PER_STEP
    past = n_pages * page
    nblk = past // MOBA_BLOCK
    assert n_pages % pps == 0 and pps % (MOBA_BLOCK // page) == 0 and nblk <= LANES
    pool_spec = lambda w, k: pl.BlockSpec((1, page, w), lambda b, s, pt: (pt[b, s * pps + k], 0, 0))
    per_seq = lambda shape: pl.BlockSpec((1,) + shape, lambda b, s, pt: (b, 0, 0))
    grid_spec = pltpu.PrefetchScalarGridSpec(
        num_scalar_prefetch=1,
        grid=(db, n_pages // pps),
        in_specs=([per_seq((IDX_HEADS, IDX_DIM)), per_seq((IDX_HEADS, 1)), per_seq((1, IDX_DIM)), per_seq((1, WIDTH))]
                  + [pool_spec(IDX_DIM, k) for k in range(pps)] + [pool_spec(WIDTH, k) for k in range(pps)]),
        out_specs=[pl.BlockSpec((1, 1, pps * page), lambda b, s, pt: (b, 0, s)),
                   per_seq((1, LANES)), per_seq((nblk, WIDTH)), per_seq((N_HEADS, LANES))],
    )
    return pl.pallas_call(
        functools.partial(_sample_stats_kernel, page=page, nblk=nblk),
        grid_spec=grid_spec,
        out_shape=[jax.ShapeDtypeStruct((db, 1, past), F32), jax.ShapeDtypeStruct((db, 1, LANES), F32),
                   jax.ShapeDtypeStruct((db, nblk, WIDTH), F32), jax.ShapeDtypeStruct((db, N_HEADS, LANES), F32)],
        compiler_params=_cp(("arbitrary", "arbitrary")),
        name="sample_stats",
    )(page_table, qi_s, wi_s, ki_s, qa_s, *([pool_kidx] * pps), *([pool_mk] * pps))


def _dsa_sample_select_kernel(sc_ref, scn_ref, b_ref, bn_ref, *, n_sel, idx_bits):
    key = _order_key(sc_ref[...])
    keyn = _order_key(scn_ref[:, 0:1])
    db, past = key.shape
    gidx = lax.broadcasted_iota(I32, key.shape, 1)

    def count(pred, pred_new):
        return (jnp.sum(jnp.where(pred, 1.0, 0.0), axis=-1, keepdims=True) + jnp.where(pred_new, 1.0, 0.0))

    def bit_step(it, v):
        cand = v + jnp.left_shift(jnp.int32(1), 31 - it)
        cnt = count(key >= cand, keyn >= cand)
        return jnp.where(cnt >= n_sel, cand, v)

    v = lax.fori_loop(0, 32, bit_step, jnp.full((db, 1), INT_MIN, I32))
    cnt_gt = count(key > v, keyn > v)
    cnt_ge = count(key >= v, keyn >= v)
    need = n_sel - cnt_gt
    trim = (cnt_ge > n_sel) & (v > INT_MIN)

    def idx_step(it, jcap):
        cand = jcap + jnp.left_shift(jnp.int32(1), idx_bits - 1 - it)
        cnt = count((key == v) & (gidx < cand), (keyn == v) & (past < cand))
        return jnp.where(cnt < need, cand, jcap)

    jcap = lax.fori_loop(0, idx_bits, idx_step, jnp.zeros((db, 1), I32))
    jcap = jnp.where(trim, jcap, 2 ** idx_bits)
    sel = (key > v) | ((key == v) & (gidx <= jcap))
    seln = (keyn > v) | ((keyn == v) & (past <= jcap))
    b_ref[...] = jnp.where(sel, 0.0, NEG_INF)
    bn_ref[...] = jnp.broadcast_to(jnp.where(seln, 0.0, NEG_INF), bn_ref.shape)


def _dsa_sample_select(scores, score_new, n_sel):
    db, past = scores.shape
    idx_bits = past.bit_length()
    return pl.pallas_call(
        functools.partial(_dsa_sample_select_kernel, n_sel=n_sel, idx_bits=idx_bits),
        out_shape=[jax.ShapeDtypeStruct((db, past), F32), jax.ShapeDtypeStruct((db, LANES), F32)],
        compiler_params=pltpu.CompilerParams(vmem_limit_bytes=VMEM_LIMIT),
        name="dsa_sample_select",
    )(scores, score_new)


def _paged_attention_kernel(pt_ref, slopes_ref, q_ref, kn_ref, vn_ref, bias_ref, bn_ref, *refs, page, block_bias):
    pps = PAGES_PER_STEP
    k_refs = refs[:pps]
    v_refs = refs[pps:2 * pps]
    o_ref, qbd_ref, sl_ref, m_ref, l_ref, acc_ref = refs[2 * pps:]
    s = pl.program_id(1)
    ns = pl.num_programs(1)
    past = ns * pps * page

    @pl.when(s == 0)
    def _():
        qbd_ref[...] = _block_diag_rows(q_ref[0] * ATTN_SCALE).astype(BF)
        hrow = lax.broadcasted_iota(I32, (N_HEADS, LANES), 0)
        sl = jnp.zeros((N_HEADS, LANES), F32)
        for h in range(N_HEADS):
            sl = jnp.where(hrow == h, slopes_ref[h], sl)
        sl_ref[...] = sl
        m_ref[...] = jnp.full(m_ref.shape, -1e30, F32)
        l_ref[...] = jnp.zeros(l_ref.shape, F32)
        acc_ref[...] = jnp.zeros(acc_ref.shape, F32)

    qbd = qbd_ref[...]
    lane = lax.broadcasted_iota(I32, (N_HEADS, LANES), 1)

    def update(logits, pv_fn):
        m_old = m_ref[:, 0:1]
        m_new = jnp.maximum(m_old, jnp.max(logits, axis=-1, keepdims=True))
        p = jnp.exp(logits - m_new)
        alpha = jnp.exp(m_old - m_new)
        l_ref[...] = jnp.broadcast_to(alpha * l_ref[:, 0:1] + jnp.sum(p, axis=-1, keepdims=True), l_ref.shape)
        acc_ref[...] = alpha * acc_ref[...] + pv_fn(p)
        m_ref[...] = jnp.broadcast_to(m_new, m_ref.shape)

    for k in range(pps):
        pg = s * pps + k
        logits = _dot_nt(qbd, k_refs[k][0].astype(BF))
        dist = (past - pg * page - lax.broadcasted_iota(I32, (1, page), 1)).astype(F32)
        logits = logits - sl_ref[:, 0:1] * dist
        if block_bias:
            blk = pg // (MOBA_BLOCK // page)
            logits = logits + jnp.max(jnp.where(lane == blk, bias_ref[0], NEG_INF), axis=-1, keepdims=True)
        else:
            logits = logits + bias_ref[0, :, k * page:(k + 1) * page]
        vk = v_refs[k]
        update(logits, lambda p: jnp.dot(p.astype(BF), vk[0].astype(BF), preferred_element_type=F32))

    @pl.when(s == ns - 1)
    def _():
        kn = jnp.broadcast_to(kn_ref[0], (SUBLANES, WIDTH)).astype(BF)
        ln = _dot_nt(qbd, kn)[:, 0:1] + bn_ref[0, :, 0:1]
        vn = vn_ref[0].astype(BF).astype(F32)
        update(ln, lambda p: p.astype(BF).astype(F32) * vn)
        o = acc_ref[...] / l_ref[:, 0:1]
        hrow = lax.broadcasted_iota(I32, (N_HEADS, WIDTH), 0)
        hlane = lax.broadcasted_iota(I32, (N_HEADS, WIDTH), 1) // HEAD_DIM
        o_ref[0] = jnp.sum(jnp.where(hrow == hlane, o, 0.0), axis=0, keepdims=True)


def _paged_attention(page_table, slopes, q_s, k_new, v_new, bias, bias_new, pool_k, pool_v, *, block_bias):
    db, n_pages = page_table.shape
    page = pool_k.shape[1]
    pps = PAGES_PER_STEP
    assert n_pages % pps == 0
    pool_spec = lambda k: pl.BlockSpec((1, page, WIDTH), lambda b, s, pt, sl: (pt[b, s * pps + k], 0, 0))
    per_seq = lambda shape: pl.BlockSpec((1,) + shape, lambda b, s, pt, sl: (b, 0, 0))
    if block_bias:
        bias_spec = per_seq((N_HEADS, LANES))
    else:
        bias_spec = pl.BlockSpec((1, 1, pps * page), lambda b, s, pt, sl: (b, 0, s))
    grid_spec = pltpu.PrefetchScalarGridSpec(
        num_scalar_prefetch=2,
        grid=(db, n_pages // pps),
        in_specs=([per_seq((1, WIDTH))] * 3 + [bias_spec, per_seq((1, LANES))]
                  + [pool_spec(k) for k in range(pps)] * 2),
        out_specs=per_seq((1, WIDTH)),
        scratch_shapes=[pltpu.VMEM((N_HEADS, WIDTH), BF), pltpu.VMEM((N_HEADS, LANES), F32),
                        pltpu.VMEM((N_HEADS, LANES), F32), pltpu.VMEM((N_HEADS, LANES), F32),
                        pltpu.VMEM((N_HEADS, WIDTH), F32)],
    )
    return pl.pallas_call(
        functools.partial(_paged_attention_kernel, page=page, block_bias=block_bias),
        grid_spec=grid_spec,
        out_shape=jax.ShapeDtypeStruct((db, 1, WIDTH), F32),
        compiler_params=_cp(("arbitrary", "arbitrary")),
        name="paged_attention",
    )(page_table, slopes, q_s, k_new, v_new, bias, bias_new, *([pool_k] * pps), *([pool_v] * pps))


def _alibi_slopes():
    n = 2 * N_HEADS
    i = jnp.arange(1, n + 1, dtype=F32)
    m = jnp.exp2(-8.0 * i / n)
    return m[0::2], m[1::2]


def _heads_major(a, b, t):
    return a.reshape(b, t, N_HEADS, HEAD_DIM).transpose(0, 2, 1, 3)


def _kt_blocks(k, b, t):
    nb = t // TQ
    return k.astype(BF).reshape(b, nb, TQ, N_HEADS, HEAD_DIM).transpose(0, 3, 1, 4, 2)


def _v_blocks(v, b, t):
    nb = t // TQ
    vb = v.astype(BF).reshape(b, nb, TQ, N_HEADS, HEAD_DIM).transpose(0, 3, 1, 2, 4)
    ones = jnp.ones(vb.shape[:-1] + (1,), BF)
    zeros = jnp.zeros(vb.shape[:-1] + (LANES - HEAD_DIM - 1,), BF)
    return jnp.concatenate([vb, ones, zeros], axis=-1)


def kernel(x_prompt, x_sample, cache_moba_k, cache_moba_v, cache_dsa_k, cache_dsa_v, cache_dsa_kidx, state_ffn_conv, page_table, c_prompt, c_sample, w_ada, b_ada, g_attn, w_in, qn_a, kn_a, qn_b, kn_b, w_out, g_ffn, w_up, conv_w, conv_b, w_down):
    bsz, t, d = x_prompt.shape
    db, ds, _ = x_sample.shape
    depth = w_in.shape[0]
    assert depth == 1 and ds == 1 and t % 512 == 0
    f = w_down.shape[1]
    nchunk = f // FF_CHUNK
    assert nchunk * FF_CHUNK == f
    slopes_a, slopes_b = _alibi_slopes()
    l = 0

    ncols = N_QKV_CHUNKS * WIDTH
    wqkv = w_in[l][:, :ncols].astype(BF)
    wkw = jnp.pad(w_in[l][:, ncols:], ((0, 0), (0, LANES - (w_in.shape[2] - ncols)))).astype(BF)
    nrm = jnp.stack([jnp.tile(g[l], N_HEADS) for g in (qn_a, kn_a, qn_b, kn_b)])
    hid = jnp.arange(WIDTH) // HEAD_DIM
    gmat = (hid[:, None] == hid[None, :]).astype(BF)
    wout = w_out[l].astype(BF)
    wupa = w_up[l][:, :f].astype(BF).reshape(d, nchunk, FF_CHUNK).transpose(1, 0, 2)
    wupg = w_up[l][:, f:].astype(BF).reshape(d, nchunk, FF_CHUNK).transpose(1, 0, 2)
    wdn = w_down[l].astype(BF).reshape(nchunk, FF_CHUNK, d)
    g_attn_l = g_attn[l].reshape(1, d)
    g_ffn_l = g_ffn[l].reshape(1, d)
    cb = conv_b[l].reshape(1, f)

    rc = bsz + db
    rpad = -rc % SUBLANES
    c_all = jnp.concatenate([c_prompt, c_sample, jnp.zeros((rpad, d), F32)], axis=0)
    mod = _modulation(c_all, w_ada[l], b_ada[l])
    mods_p = [m.reshape(bsz, 1, d) for m in jnp.split(mod[:bsz], N_MOD, axis=-1)]
    mods_s = [m.reshape(1, db, d) for m in jnp.split(mod[bsz:rc], N_MOD, axis=-1)]

    n = bsz * t
    xp2 = x_prompt.reshape(n, d)
    tm = 512
    qa, ka, va, qb, kb, vb, qi, kw, kmean = _pre_project(
        xp2, mods_p[0], mods_p[1], g_attn_l, wqkv, wkw, nrm, gmat, tm=tm, rows_per_group=t, with_kmean=True)
    nb = t // TQ
    assert nb <= LANES
    q_a = _heads_major((qa * ATTN_SCALE).astype(BF), bsz, t)
    q_b = _heads_major((qb * ATTN_SCALE).astype(BF), bsz, t)
    kmt = kmean.reshape(bsz, nb, N_HEADS, HEAD_DIM).transpose(0, 2, 3, 1).astype(BF)
    kmt = jnp.pad(kmt, ((0, 0), (0, 0), (0, 0), (0, LANES - nb)))
    o_a = _moba_prompt(slopes_a, q_a, _kt_blocks(ka, bsz, t), _v_blocks(va, bsz, t), kmt)
    qi_h = qi.astype(BF).reshape(bsz, t, IDX_HEADS, IDX_DIM).transpose(0, 2, 1, 3)
    kit = kw[:, :IDX_DIM].astype(BF).reshape(bsz, nb, TQ, IDX_DIM).transpose(0, 1, 3, 2)
    n_sel = min(DSA_TOPK, t // 4)
    o_b = _dsa_prompt(slopes_b, qi_h, kw, kit, q_b, _kt_blocks(kb, bsz, t), _v_blocks(vb, bsz, t), n_sel)
    o_a2 = o_a.transpose(0, 2, 1, 3).reshape(n, WIDTH)
    o_b2 = o_b.transpose(0, 2, 1, 3).reshape(n, WIDTH)
    y_p, conv_p = _finish(xp2, o_a2, o_b2, mods_p[2:], g_ffn_l, wout, wupa, wupg, conv_w[l], cb, wdn,
                          tm=tm, rows_per_group=t)

    kv_shape = (1, bsz, t, N_HEADS, HEAD_DIM)
    outs_p = (ka.reshape(kv_shape), va.reshape(kv_shape), kb.reshape(kv_shape),
              vb.reshape(kv_shape), kw[:, :IDX_DIM].reshape(1, bsz, t, IDX_DIM), conv_p[None])

    n_phys, page = cache_moba_k.shape[1], cache_moba_k.shape[2]
    n_pages = page_table.shape[1]
    past = n_pages * page
    assert past % MOBA_BLOCK == 0 and db % SUBLANES == 0
    xs2 = x_sample.reshape(db, d)
    qa_s, ka_s, va_s, qb_s, kb_s, vb_s, qi_s, kw_s = _pre_project(
        xs2, mods_s[0], mods_s[1], g_attn_l, wqkv, wkw, nrm, gmat, tm=db, rows_per_group=db, with_kmean=False)
    pool = lambda cch: cch[l].reshape(n_phys, page, -1)
    row3 = lambda a: a.reshape(db, 1, a.shape[-1])
    scores, score_new, _, bias_blk = _sample_stats(
        page_table, qi_s.reshape(db, IDX_HEADS, IDX_DIM), kw_s[:, IDX_DIM:IDX_DIM + IDX_HEADS].reshape(db, IDX_HEADS, 1),
        row3(kw_s[:, :IDX_DIM]), row3(qa_s), pool(cache_dsa_kidx), pool(cache_moba_k))
    n_sel_s = min(DSA_TOPK, (past + ds) // 4)
    bias_pos, bias_new = _dsa_sample_select(scores.reshape(db, past), score_new.reshape(db, LANES), n_sel_s)
    o_a_s = _paged_attention(page_table, slopes_a, row3(qa_s), row3(ka_s), row3(va_s), bias_blk,
                             jnp.zeros((db, 1, LANES), F32), pool(cache_moba_k), pool(cache_moba_v), block_bias=True)
    o_b_s = _paged_attention(page_table, slopes_b, row3(qb_s), row3(kb_s), row3(vb_s), bias_pos.reshape(db, 1, past),
                             bias_new.reshape(db, 1, LANES), pool(cache_dsa_k), pool(cache_dsa_v), block_bias=False)
    state = state_ffn_conv[l]
    y_s, a_s = _finish(xs2, o_a_s.reshape(db, WIDTH), o_b_s.reshape(db, WIDTH), mods_s[2:], g_ffn_l, wout, wupa, wupg,
                       conv_w[l], cb, wdn, tm=db, rows_per_group=db, prefix=(state[:, 0], state[:, 1]))
    conv_s = jnp.stack([state[:, 1], a_s], axis=1)
    kvs_shape = (1, db, ds, N_HEADS, HEAD_DIM)
    outs_s = (ka_s.reshape(kvs_shape), va_s.reshape(kvs_shape), kb_s.reshape(kvs_shape), vb_s.reshape(kvs_shape),
              kw_s[:, :IDX_DIM].reshape(1, db, ds, IDX_DIM), conv_s[None])
    return (y_p.reshape(bsz, t, d), y_s.reshape(db, ds, d)) + outs_p + outs_s
```

```python
import functools

import jax
import jax.numpy as jnp
from jax import lax
from jax.experimental import pallas as pl
from jax.experimental.pallas import tpu as pltpu

HEAD_DIM = 64
N_HEADS = 8
WIDTH = N_HEADS * HEAD_DIM
MOBA_BLOCK = 256
MOBA_TOPK = 3
DSA_TOPK = 256
IDX_HEADS = 8
IDX_DIM = 64
CONV_WIDTH = 3
EPS = 1e-6
N_MOD = 6
ATTN_SCALE = HEAD_DIM ** -0.5
N_QKV_CHUNKS = 7
LANES = 128
SUBLANES = 8
TQ = MOBA_BLOCK
SB = 1024
BPS = SB // TQ
FF_CHUNK = 256
PAGES_PER_STEP = 4
VMEM_LIMIT = 56 * 1024 * 1024

BF = jnp.bfloat16
F32 = jnp.float32
I32 = jnp.int32
NEG_INF = float("-inf")
INT_MIN = -2 ** 31


def _cp(sem):
    return pltpu.CompilerParams(dimension_semantics=sem, vmem_limit_bytes=VMEM_LIMIT)


def _const_spec(shape):
    nd = len(shape)
    return pl.BlockSpec(shape, lambda *_: (0,) * nd)


def _dot_nt(a, b):
    return lax.dot_general(a, b, (((1,), (1,)), ((), ())), preferred_element_type=F32)


def _mod_kernel(c_ref, w_ref, b_ref, o_ref):
    s = jax.nn.silu(c_ref[...])
    o_ref[...] = jnp.dot(s.astype(BF), w_ref[...].astype(BF), preferred_element_type=F32) + b_ref[...]


def _modulation(c_all, w_ada, b_ada):
    r, d = c_all.shape
    n = w_ada.shape[1]
    tn = 1024
    return pl.pallas_call(
        _mod_kernel,
        grid=(n // tn,),
        in_specs=[pl.BlockSpec((r, d), lambda j: (0, 0)),
                  pl.BlockSpec((d, tn), lambda j: (0, j)),
                  pl.BlockSpec((1, tn), lambda j: (0, j))],
        out_specs=pl.BlockSpec((r, tn), lambda j: (0, j)),
        out_shape=jax.ShapeDtypeStruct((r, n), F32),
        compiler_params=_cp(("arbitrary",)),
        name="modulation",
    )(c_all, w_ada, b_ada.reshape(1, n))


def _pre_kernel(x_ref, shift_ref, scale_ref, g_ref, wqkv_ref, wkw_ref, nrm_ref, gmat_ref, *outs, tm, with_kmean):
    x = x_ref[...]
    ms = jnp.mean(x * x, axis=-1, keepdims=True)
    xn = x * lax.rsqrt(ms + EPS) * g_ref[...]
    h = (xn * (1.0 + scale_ref[0]) + shift_ref[0]).astype(BF)
    norm_row = {0: 0, 1: 1, 3: 2, 4: 3}
    for c in range(N_QKV_CHUNKS):
        p = jnp.dot(h, wqkv_ref[:, c * WIDTH:(c + 1) * WIDTH], preferred_element_type=F32)
        if c in norm_row:
            ss = jnp.dot((p * p).astype(BF), gmat_ref[...], preferred_element_type=F32)
            k = norm_row[c]
            p = p * lax.rsqrt(ss * (1.0 / HEAD_DIM) + EPS) * nrm_ref[k:k + 1, :]
        outs[c][...] = p
        if c == 1 and with_kmean:
            outs[N_QKV_CHUNKS + 1][0] = jnp.mean(p.reshape(tm // MOBA_BLOCK, MOBA_BLOCK, WIDTH), axis=1)
    outs[N_QKV_CHUNKS][...] = jnp.dot(h, wkw_ref[...], preferred_element_type=F32)


def _pre_project(x2, shift, scale, g_attn, wqkv, wkw, nrm, gmat, *, tm, rows_per_group, with_kmean):
    n, d = x2.shape
    grid = (n // tm,)
    mr = shift.shape[1]
    tiles_per_group = rows_per_group // tm
    mod_spec = pl.BlockSpec((1, mr, d), lambda i: (i // tiles_per_group, 0, 0))
    row_spec = lambda w: pl.BlockSpec((tm, w), lambda i: (i, 0))
    out_shape = [jax.ShapeDtypeStruct((n, WIDTH), F32)] * N_QKV_CHUNKS + [jax.ShapeDtypeStruct((n, LANES), F32)]
    out_specs = [row_spec(WIDTH)] * N_QKV_CHUNKS + [row_spec(LANES)]
    if with_kmean:
        nb = tm // MOBA_BLOCK
        out_shape.append(jax.ShapeDtypeStruct((n // tm, nb, WIDTH), F32))
        out_specs.append(pl.BlockSpec((1, nb, WIDTH), lambda i: (i, 0, 0)))
    return pl.pallas_call(
        functools.partial(_pre_kernel, tm=tm, with_kmean=with_kmean),
        grid=grid,
        in_specs=[row_spec(d), mod_spec, mod_spec, _const_spec((1, d)),
                  _const_spec(wqkv.shape), _const_spec(wkw.shape), _const_spec(nrm.shape), _const_spec(gmat.shape)],
        out_specs=out_specs,
        out_shape=out_shape,
        compiler_params=_cp(("arbitrary",)),
        name="pre_project",
    )(x2, shift, scale, g_attn, wqkv, wkw, nrm, gmat)


def _two_pass_attention(q, kt_at, v_at, slope, i, bias_at, l_ref, m_ref, acc_ref):
    n_steps = i // BPS + 1
    m_ref[...] = jnp.full((TQ, TQ), NEG_INF, F32)

    def pass1(j, carry):
        cols = lax.broadcasted_iota(I32, (1, SB), 1) + (j * SB - i * TQ)
        l = jnp.dot(q, kt_at(j), preferred_element_type=F32) + slope * cols.astype(F32) + bias_at(j)
        l_ref[j] = l
        mx = m_ref[...]
        for k in range(BPS):
            mx = jnp.maximum(mx, l[:, k * TQ:(k + 1) * TQ])
        m_ref[...] = mx
        return carry

    lax.fori_loop(0, n_steps, pass1, 0)
    m = jnp.max(m_ref[...], axis=-1, keepdims=True)
    acc_ref[...] = jnp.zeros((TQ, LANES), F32)

    def pass2(j, carry):
        p = jnp.exp(l_ref[j] - m).astype(BF)
        acc_ref[...] += jnp.dot(p, v_at(j), preferred_element_type=F32)
        return carry

    lax.fori_loop(0, n_steps, pass2, 0)
    acc = acc_ref[...]
    return acc[:, :HEAD_DIM] / acc[:, HEAD_DIM:HEAD_DIM + 1]


def _moba_prompt_kernel(slopes_ref, q_ref, kt_ref, v_ref, kmt_ref, o_ref, l_ref, m_ref, acc_ref, bias_ref, sel_ref):
    h = pl.program_id(1)
    i = pl.program_id(2)
    sub = i % BPS
    row0 = pl.multiple_of(sub * TQ, TQ)

    @pl.when(sub == 0)
    def _():
        s = jnp.dot(q_ref[0, 0], kmt_ref[0, 0], preferred_element_type=F32)
        lane_s = lax.broadcasted_iota(I32, s.shape, 1)
        own = i + lax.broadcasted_iota(I32, s.shape, 0) // TQ
        tri = (lax.broadcasted_iota(I32, (LANES, LANES), 0) <= lax.broadcasted_iota(I32, (LANES, LANES), 1))
        tri = jnp.where(tri, 1.0, 0.0).astype(BF)
        s = jnp.where(lane_s < own, s, NEG_INF)
        bias = jnp.full(s.shape, NEG_INF, F32)
        for _ in range(MOBA_TOPK):
            mx = jnp.max(s, axis=-1, keepdims=True)
            eq = s == mx
            rank = jnp.dot(jnp.where(eq, 1.0, 0.0).astype(BF), tri, preferred_element_type=F32)
            pick = eq & (rank == 1.0) & (mx > NEG_INF)
            bias = jnp.where(pick, 0.0, bias)
            s = jnp.where(pick, NEG_INF, s)
        sel_ref[...] = bias

    q = q_ref[0, 0, pl.ds(row0, TQ), :]
    bias_ref[...] = sel_ref[pl.ds(row0, TQ), :]
    lane = lax.broadcasted_iota(I32, (TQ, LANES), 1)
    r = lax.broadcasted_iota(I32, (TQ, TQ), 0)
    c = lax.broadcasted_iota(I32, (TQ, TQ), 1)
    causal = jnp.where(c <= r, 0.0, NEG_INF).astype(F32)

    def bias_at(j):
        parts = []
        for k in range(BPS):
            blk = j * BPS + k
            col = jnp.max(jnp.where(lane == blk, bias_ref[...], NEG_INF), axis=-1, keepdims=True)
            parts.append(jnp.where(blk == i, causal, jnp.broadcast_to(col, (TQ, TQ))))
        return jnp.concatenate(parts, axis=1)

    o_ref[0, 0] = _two_pass_attention(
        q, lambda j: kt_ref[0, 0, j], lambda j: v_ref[0, 0, j], slopes_ref[h], i, bias_at, l_ref, m_ref, acc_ref)


def _moba_prompt(slopes, q, kt, v, kmt):
    b, h, t, _ = q.shape
    nq = t // TQ
    nsb = t // SB
    grid_spec = pltpu.PrefetchScalarGridSpec(
        num_scalar_prefetch=1,
        grid=(b, h, nq),
        in_specs=[pl.BlockSpec((1, 1, SB, HEAD_DIM), lambda bi, hi, i, s: (bi, hi, i // BPS, 0)),
                  pl.BlockSpec((1, 1, nsb, HEAD_DIM, SB), lambda bi, hi, i, s: (bi, hi, 0, 0, 0)),
                  pl.BlockSpec((1, 1, nsb, SB, LANES), lambda bi, hi, i, s: (bi, hi, 0, 0, 0)),
                  pl.BlockSpec((1, 1, HEAD_DIM, LANES), lambda bi, hi, i, s: (bi, hi, 0, 0))],
        out_specs=pl.BlockSpec((1, 1, TQ, HEAD_DIM), lambda bi, hi, i, s: (bi, hi, i, 0)),
        scratch_shapes=[pltpu.VMEM((nsb, TQ, SB), F32), pltpu.VMEM((TQ, TQ), F32),
                        pltpu.VMEM((TQ, LANES), F32), pltpu.VMEM((TQ, LANES), F32), pltpu.VMEM((SB, LANES), F32)],
    )
    return pl.pallas_call(
        _moba_prompt_kernel,
        grid_spec=grid_spec,
        out_shape=jax.ShapeDtypeStruct((b, h, t, HEAD_DIM), F32),
        compiler_params=_cp(("arbitrary", "arbitrary", "arbitrary")),
        name="moba_prompt",
    )(slopes, q, kt, v, kmt)


def _order_key(score):
    bits = lax.bitcast_convert_type(score, I32)
    return bits ^ ((bits >> 31) & 0x7FFFFFFF)


def _dsa_select(i, n_sel, idx_bits, sk_ref, b_ref, jc_ref):
    n_steps = i // BPS + 1
    gr = lax.broadcasted_iota(I32, (TQ, SB), 0) + i * TQ
    c = lax.broadcasted_iota(I32, (TQ, SB), 1)

    def count(pred):
        def body(j, acc):
            hit = jnp.where(pred(j, sk_ref[j]), 1.0, 0.0)
            for k in range(SB // LANES):
                acc = acc + hit[:, k * LANES:(k + 1) * LANES]
            return acc

        acc = lax.fori_loop(0, n_steps, body, jnp.zeros((TQ, LANES), F32))
        return jnp.sum(acc, axis=-1, keepdims=True)

    def bit_step(it, v):
        cand = v + jnp.left_shift(jnp.int32(1), 31 - it)
        cnt = count(lambda j, key: key >= cand)
        return jnp.where(cnt >= n_sel, cand, v)

    v = lax.fori_loop(0, 32, bit_step, jnp.full((TQ, 1), INT_MIN, I32))
    cnt_gt = count(lambda j, key: key > v)
    cnt_ge = count(lambda j, key: key >= v)
    need = n_sel - cnt_gt
    trim = (cnt_ge > n_sel) & (v > INT_MIN)
    jc_ref[...] = jnp.full((TQ, LANES), 2 ** idx_bits, I32)

    @pl.when(jnp.max(jnp.where(trim, 1.0, 0.0)) > 0.0)
    def _():
        def idx_step(it, jcap):
            cand = jcap + jnp.left_shift(jnp.int32(1), idx_bits - 1 - it)
            cnt = count(lambda j, key: (key == v) & (c + j * SB < cand))
            return jnp.where(cnt < need, cand, jcap)

        jcap = lax.fori_loop(0, idx_bits, idx_step, jnp.zeros((TQ, 1), I32))
        jc_ref[...] = jnp.broadcast_to(jnp.where(trim, jcap, 2 ** idx_bits), (TQ, LANES))

    jcap = jc_ref[:, 0:1]

    def write(j, carry):
        key = sk_ref[j]
        gc = c + j * SB
        sel = ((key > v) | ((key == v) & (gc <= jcap))) & (gc <= gr)
        b_ref[j] = jnp.where(sel, 0.0, NEG_INF)
        return carry

    lax.fori_loop(0, n_steps, write, 0)


def _dsa_prompt_kernel(slopes_ref, qi_ref, kw_ref, kit_ref, q_ref, kt_ref, v_ref, o_ref,
                       sk_ref, b_ref, l_ref, m_ref, acc_ref, jc_ref, *, n_sel, idx_bits):
    i = pl.program_id(1)
    h = pl.program_id(2)

    @pl.when(h == 0)
    def _():
        wi = kw_ref[...] * (IDX_HEADS * IDX_DIM) ** -0.5
        wcols = [wi[:, IDX_DIM + hh:IDX_DIM + hh + 1] for hh in range(IDX_HEADS)]
        gr = lax.broadcasted_iota(I32, (TQ, SB), 0) + i * TQ
        c = lax.broadcasted_iota(I32, (TQ, SB), 1)

        def score_step(j, carry):
            kit = kit_ref[0, j]
            sc = jnp.maximum(jnp.dot(qi_ref[0, 0], kit, preferred_element_type=F32), 0.0) * wcols[0]
            for hh in range(1, IDX_HEADS):
                sc = sc + jnp.maximum(jnp.dot(qi_ref[0, hh], kit, preferred_element_type=F32), 0.0) * wcols[hh]
            sk_ref[j] = jnp.where(c + j * SB <= gr, _order_key(sc), INT_MIN)
            return carry

        lax.fori_loop(0, i // BPS + 1, score_step, 0)
        _dsa_select(i, n_sel, idx_bits, sk_ref, b_ref, jc_ref)

    o_ref[0, 0] = _two_pass_attention(
        q_ref[0, 0], lambda j: kt_ref[0, 0, j], lambda j: v_ref[0, 0, j], slopes_ref[h], i,
        lambda j: b_ref[j], l_ref, m_ref, acc_ref)


def _dsa_prompt(slopes, qi, kw, kit, q, kt, v, n_sel):
    b, h, t, _ = q.shape
    nq = t // TQ
    nsb = t // SB
    idx_bits = max(1, (t - 1).bit_length())
    blk = lambda shape, fn: pl.BlockSpec(shape, fn)
    grid_spec = pltpu.PrefetchScalarGridSpec(
        num_scalar_prefetch=1,
        grid=(b, nq, h),
        in_specs=[blk((1, IDX_HEADS, TQ, IDX_DIM), lambda bi, i, hi, s: (bi, 0, i, 0)),
                  blk((TQ, LANES), lambda bi, i, hi, s: (bi * nq + i, 0)),
                  blk((1, nsb, IDX_DIM, SB), lambda bi, i, hi, s: (bi, 0, 0, 0)),
                  blk((1, 1, TQ, HEAD_DIM), lambda bi, i, hi, s: (bi, hi, i, 0)),
                  blk((1, 1, nsb, HEAD_DIM, SB), lambda bi, i, hi, s: (bi, hi, 0, 0, 0)),
                  blk((1, 1, nsb, SB, LANES), lambda bi, i, hi, s: (bi, hi, 0, 0, 0))],
        out_specs=blk((1, 1, TQ, HEAD_DIM), lambda bi, i, hi, s: (bi, hi, i, 0)),
        scratch_shapes=[pltpu.VMEM((nsb, TQ, SB), I32), pltpu.VMEM((nsb, TQ, SB), F32), pltpu.VMEM((nsb, TQ, SB), F32),
                        pltpu.VMEM((TQ, TQ), F32), pltpu.VMEM((TQ, LANES), F32), pltpu.VMEM((TQ, LANES), I32)],
    )
    return pl.pallas_call(
        functools.partial(_dsa_prompt_kernel, n_sel=n_sel, idx_bits=idx_bits),
        grid_spec=grid_spec,
        out_shape=jax.ShapeDtypeStruct((b, h, t, HEAD_DIM), F32),
        compiler_params=_cp(("arbitrary", "arbitrary", "arbitrary")),
        name="dsa_prompt",
    )(slopes, qi, kw, kit, q, kt, v)


def _finish_kernel(*refs, tm, nchunk, seq_mode, tiles_per_seq):
    (x_ref, oa_ref, ob_ref, gate_a_ref, shift_f_ref, scale_f_ref, gate_f_ref, g_ref, wout_ref,
     wupa_ref, wupg_ref, cw_ref, cb_ref, wdn_ref) = refs[:14]
    if seq_mode:
        y_ref, st_ref, abuf_ref, carry_ref, yacc_ref = refs[14:]
    else:
        pre0_ref, pre1_ref, y_ref, st_ref, yacc_ref = refs[14:]
    i = pl.program_id(0)
    o = (jnp.dot(oa_ref[...].astype(BF), wout_ref[0:WIDTH, :], preferred_element_type=F32)
         + jnp.dot(ob_ref[...].astype(BF), wout_ref[WIDTH:2 * WIDTH, :], preferred_element_type=F32))
    x1 = x_ref[...] + gate_a_ref[0] * o
    ms = jnp.mean(x1 * x1, axis=-1, keepdims=True)
    h = ((x1 * lax.rsqrt(ms + EPS) * g_ref[...]) * (1.0 + scale_f_ref[0]) + shift_f_ref[0]).astype(BF)
    yacc_ref[...] = jnp.zeros_like(yacc_ref)
    for c in range(nchunk):
        cols = slice(c * FF_CHUNK, (c + 1) * FF_CHUNK)
        a = jnp.dot(h, wupa_ref[c], preferred_element_type=F32)
        gt = jnp.dot(h, wupg_ref[c], preferred_element_type=F32)
        if seq_mode:
            first = (i % tiles_per_seq) == 0
            buf = abuf_ref.at[c % 2]
            buf[0:SUBLANES, :] = jnp.where(first, 0.0, carry_ref[c])
            buf[SUBLANES:SUBLANES + tm, :] = a
            a1 = buf[SUBLANES - 1:SUBLANES - 1 + tm, :]
            a2 = buf[SUBLANES - 2:SUBLANES - 2 + tm, :]
            carry_ref[c] = a[tm - SUBLANES:tm, :]
            st_ref[0, :, cols] = a[tm - (CONV_WIDTH - 1):tm, :]
        else:
            a2 = pre0_ref[:, cols]
            a1 = pre1_ref[:, cols]
            st_ref[:, cols] = a
        conv = cb_ref[:, cols] + a2 * cw_ref[0:1, cols]
        conv = conv + a1 * cw_ref[1:2, cols]
        conv = conv + a * cw_ref[2:3, cols]
        act = (jax.nn.gelu(conv) * gt).astype(BF)
        yacc_ref[...] += jnp.dot(act, wdn_ref[c], preferred_element_type=F32)
    y_ref[...] = x1 + gate_f_ref[0] * yacc_ref[...]


def _finish(x2, oa, ob, mods, g_ffn, wout, wupa, wupg, conv_w, conv_b, wdn, *, tm, rows_per_group, prefix=None):
    n, d = x2.shape
    nchunk = wupa.shape[0]
    f = nchunk * FF_CHUNK
    seq_mode = prefix is None
    tiles_per_seq = rows_per_group // tm
    mr = mods[0].shape[1]
    mod_spec = pl.BlockSpec((1, mr, d), lambda i: (i // tiles_per_seq, 0, 0))
    row_spec = lambda w: pl.BlockSpec((tm, w), lambda i: (i, 0))
    in_specs = ([row_spec(d), row_spec(WIDTH), row_spec(WIDTH)] + [mod_spec] * 4 +
                [_const_spec((1, d)), _const_spec(wout.shape), _const_spec(wupa.shape), _const_spec(wupg.shape),
                 _const_spec(conv_w.shape), _const_spec((1, f)), _const_spec(wdn.shape)])
    args = [x2, oa, ob, *mods, g_ffn, wout, wupa, wupg, conv_w, conv_b, wdn]
    scratch = []
    if seq_mode:
        nseq = n // rows_per_group
        st_shape = jax.ShapeDtypeStruct((nseq, CONV_WIDTH - 1, f), F32)
        st_spec = pl.BlockSpec((1, CONV_WIDTH - 1, f), lambda i: (i // tiles_per_seq, 0, 0))
        scratch += [pltpu.VMEM((2, SUBLANES + tm, FF_CHUNK), F32), pltpu.VMEM((nchunk, SUBLANES, FF_CHUNK), F32)]
    else:
        in_specs += [row_spec(f), row_spec(f)]
        args += list(prefix)
        st_shape = jax.ShapeDtypeStruct((n, f), F32)
        st_spec = row_spec(f)
    scratch.append(pltpu.VMEM((tm, d), F32))
    return pl.pallas_call(
        functools.partial(_finish_kernel, tm=tm, nchunk=nchunk, seq_mode=seq_mode, tiles_per_seq=tiles_per_seq),
        grid=(n // tm,),
        in_specs=in_specs,
        out_specs=[row_spec(d), st_spec],
        out_shape=[jax.ShapeDtypeStruct((n, d), F32), st_shape],
        scratch_shapes=scratch,
        compiler_params=_cp(("arbitrary",)),
        name="finish",
    )(*args)


def _own_head_bias(n):
    hrow = lax.broadcasted_iota(I32, (N_HEADS, n), 0)
    lane = lax.broadcasted_iota(I32, (N_HEADS, n), 1)
    return jnp.where(lane % N_HEADS == hrow, 0.0, NEG_INF).astype(F32)


def _sample_stats_kernel(pt_ref, qi_ref, wi_ref, kin_ref, qa_ref, *refs, page, nblk):
    pps = PAGES_PER_STEP
    kidx_refs = refs[:pps]
    kmoba_refs = refs[pps:2 * pps]
    sc_ref, scn_ref, bb_ref, km_ref = refs[2 * pps:]
    s = pl.program_id(1)
    qi = qi_ref[0].astype(BF)
    wi = wi_ref[0] * (IDX_HEADS * IDX_DIM) ** -0.5
    ppb = MOBA_BLOCK // page
    for k in range(pps):
        rel = jnp.maximum(_dot_nt(qi, kidx_refs[k][0, 0].astype(BF)), 0.0)
        sc_ref[0, :, k * page:(k + 1) * page] = jnp.sum(rel * wi, axis=0, keepdims=True)
    for kb in range(pps // ppb):
        tot = jnp.sum(kmoba_refs[kb * ppb][0, 0], axis=0)
        for k in range(1, ppb):
            tot = tot + jnp.sum(kmoba_refs[kb * ppb + k][0, 0], axis=0)
        km_ref[s * (pps // ppb) + kb] = tot * (1.0 / MOBA_BLOCK)

    @pl.when(s == pl.num_programs(1) - 1)
    def _():
        kin = jnp.broadcast_to(kin_ref[0], (SUBLANES, IDX_DIM)).astype(BF)
        reln = jnp.maximum(_dot_nt(qi, kin), 0.0)
        scn = jnp.sum(reln[:, 0:1] * wi, axis=0, keepdims=True)
        scn_ref[0] = jnp.broadcast_to(scn, (1, LANES))
        n = nblk * N_HEADS
        km = km_ref[...].reshape(n, HEAD_DIM).astype(BF)
        sb = _dot_nt(qa_ref[0].astype(BF), km) + _own_head_bias(n)
        lanef = lax.broadcasted_iota(I32, sb.shape, 1).astype(F32)
        bias = jnp.full(sb.shape, NEG_INF, F32)
        for _ in range(MOBA_TOPK):
            mx = jnp.max(sb, axis=-1, keepdims=True)
            first = jnp.min(jnp.where(sb == mx, lanef, float(n)), axis=-1, keepdims=True)
            pick = (lanef == first) & (mx > NEG_INF)
            bias = jnp.where(pick, 0.0, bias)
            sb = jnp.where(pick, NEG_INF, sb)
        bb_ref[0] = bias


def _sample_stats(page_table, qi_s, wi_s, ki_s, qa_s, pool_kidx, pool_mk):
    db, n_pages = page_table.shape
    page = pool_kidx.shape[2]
    pps = PAGES_PER_STEP
    past = n_pages * page
    nblk = past // MOBA_BLOCK
    assert n_pages % pps == 0 and pps % (MOBA_BLOCK // page) == 0
    kidx_spec = lambda k: pl.BlockSpec((1, 1, page, IDX_DIM), lambda b, s, pt: (0, pt[b, s * pps + k], 0, 0))
    kmoba_spec = lambda k: pl.BlockSpec((1, 1, page, N_HEADS, HEAD_DIM),
                                        lambda b, s, pt: (0, pt[b, s * pps + k], 0, 0, 0))
    per_seq = lambda shape: pl.BlockSpec((1,) + shape, lambda b, s, pt: (b, 0, 0))
    grid_spec = pltpu.PrefetchScalarGridSpec(
        num_scalar_prefetch=1,
        grid=(db, n_pages // pps),
        in_specs=([per_seq((IDX_HEADS, IDX_DIM)), per_seq((IDX_HEADS, 1)), per_seq((1, IDX_DIM)),
                   per_seq((N_HEADS, HEAD_DIM))]
                  + [kidx_spec(k) for k in range(pps)] + [kmoba_spec(k) for k in range(pps)]),
        out_specs=[pl.BlockSpec((1, 1, pps * page), lambda b, s, pt: (b, 0, s)),
                   per_seq((1, LANES)), per_seq((N_HEADS, nblk * N_HEADS))],
        scratch_shapes=[pltpu.VMEM((nblk, N_HEADS, HEAD_DIM), F32)],
    )
    return pl.pallas_call(
        functools.partial(_sample_stats_kernel, page=page, nblk=nblk),
        grid_spec=grid_spec,
        out_shape=[jax.ShapeDtypeStruct((db, 1, past), F32), jax.ShapeDtypeStruct((db, 1, LANES), F32),
                   jax.ShapeDtypeStruct((db, N_HEADS, nblk * N_HEADS), F32)],
        compiler_params=_cp(("arbitrary", "arbitrary")),
        name="sample_stats",
    )(page_table, qi_s, wi_s, ki_s, qa_s, *([pool_kidx] * pps), *([pool_mk] * pps))


def _dsa_sample_select_kernel(sc_ref, scn_ref, b_ref, bn_ref, *, n_sel, idx_bits):
    key = _order_key(sc_ref[...])
    keyn = _order_key(scn_ref[:, 0:1])
    db, past = key.shape
    gidx = lax.broadcasted_iota(I32, key.shape, 1)

    def count(pred, pred_new):
        return (jnp.sum(jnp.where(pred, 1.0, 0.0), axis=-1, keepdims=True) + jnp.where(pred_new, 1.0, 0.0))

    def bit_step(it, v):
        cand = v + jnp.left_shift(jnp.int32(1), 31 - it)
        cnt = count(key >= cand, keyn >= cand)
        return jnp.where(cnt >= n_sel, cand, v)

    v = lax.fori_loop(0, 32, bit_step, jnp.full((db, 1), INT_MIN, I32))
    cnt_gt = count(key > v, keyn > v)
    cnt_ge = count(key >= v, keyn >= v)
    need = n_sel - cnt_gt
    trim = (cnt_ge > n_sel) & (v > INT_MIN)

    def idx_step(it, jcap):
        cand = jcap + jnp.left_shift(jnp.int32(1), idx_bits - 1 - it)
        cnt = count((key == v) & (gidx < cand), (keyn == v) & (past < cand))
        return jnp.where(cnt < need, cand, jcap)

    jcap = lax.fori_loop(0, idx_bits, idx_step, jnp.zeros((db, 1), I32))
    jcap = jnp.where(trim, jcap, 2 ** idx_bits)
    sel = (key > v) | ((key == v) & (gidx <= jcap))
    seln = (keyn > v) | ((keyn == v) & (past <= jcap))
    bn_ref[...] = jnp.broadcast_to(jnp.where(seln, 0.0, NEG_INF), bn_ref.shape)
    sel01 = jnp.where(sel, 1.0, 0.0).astype(BF)
    er = lax.broadcasted_iota(I32, (LANES, LANES * N_HEADS), 0)
    ec = lax.broadcasted_iota(I32, (LANES, LANES * N_HEADS), 1)
    expand = jnp.where(ec // N_HEADS == er, 1.0, 0.0).astype(BF)
    for ch in range(past // LANES):
        rep = jnp.dot(sel01[:, ch * LANES:(ch + 1) * LANES], expand, preferred_element_type=F32)
        b_ref[:, ch * LANES * N_HEADS:(ch + 1) * LANES * N_HEADS] = jnp.where(rep > 0.5, 0.0, NEG_INF)


def _dsa_sample_select(scores, score_new, n_sel):
    db, past = scores.shape
    idx_bits = past.bit_length()
    return pl.pallas_call(
        functools.partial(_dsa_sample_select_kernel, n_sel=n_sel, idx_bits=idx_bits),
        out_shape=[jax.ShapeDtypeStruct((db, past * N_HEADS), F32), jax.ShapeDtypeStruct((db, LANES), F32)],
        compiler_params=pltpu.CompilerParams(vmem_limit_bytes=VMEM_LIMIT),
        name="dsa_sample_select",
    )(scores, score_new)


def _paged_attention_kernel(pt_ref, slopes_ref, q_ref, kn_ref, vn_ref, bias_ref, bn_ref, *refs, page, block_bias):
    pps = PAGES_PER_STEP
    k_refs = refs[:pps]
    v_refs = refs[pps:2 * pps]
    o_ref, l_ref, mx_ref, m_ref, ps_ref, acc_ref, sl_ref = refs[2 * pps:]
    s = pl.program_id(1)
    ns = pl.num_programs(1) // 2
    past = ns * pps * page
    rows = page * N_HEADS
    own = _own_head_bias(rows)
    lane = lax.broadcasted_iota(I32, (N_HEADS, rows), 1)
    hrow = lax.broadcasted_iota(I32, (N_HEADS, rows), 0)
    q = (q_ref[0] * ATTN_SCALE).astype(BF)

    @pl.when(s == 0)
    def _():
        sl = jnp.zeros((N_HEADS, LANES), F32)
        for h in range(N_HEADS):
            sl = jnp.where(hrow[:, :LANES] == h, slopes_ref[h], sl)
        sl_ref[...] = sl
        mx_ref[...] = jnp.full(mx_ref.shape, NEG_INF, F32)

    @pl.when(s < ns)
    def _():
        for k in range(pps):
            pg = s * pps + k
            k2 = k_refs[k][0, 0].reshape(rows, HEAD_DIM).astype(BF)
            dist = (past - pg * page - lane // N_HEADS).astype(F32)
            logits = _dot_nt(q, k2) - sl_ref[:, 0:1] * dist + own
            if block_bias:
                blk = pg // (MOBA_BLOCK // page)
                nb = bias_ref.shape[2]
                lb = lax.broadcasted_iota(I32, (N_HEADS, nb), 1)
                hb = lax.broadcasted_iota(I32, (N_HEADS, nb), 0)
                logits = logits + jnp.max(jnp.where(lb == blk * N_HEADS + hb, bias_ref[0], NEG_INF),
                                          axis=-1, keepdims=True)
            else:
                logits = logits + bias_ref[0, :, k * rows:(k + 1) * rows]
            l_ref[pg] = logits
            mx_ref[...] = jnp.maximum(mx_ref[...], logits)

    @pl.when(s == ns - 1)
    def _():
        ln = _dot_nt(q, kn_ref[0].astype(BF))
        hh = lax.broadcasted_iota(I32, ln.shape, 0) == lax.broadcasted_iota(I32, ln.shape, 1)
        ln = jnp.max(jnp.where(hh, ln, NEG_INF), axis=-1, keepdims=True) + bn_ref[0, :, 0:1]
        m = jnp.maximum(jnp.max(mx_ref[...], axis=-1, keepdims=True), ln)
        m_ref[...] = jnp.broadcast_to(m, m_ref.shape)
        pn = jnp.exp(ln - m)
        ps_ref[...] = jnp.zeros(ps_ref.shape, F32)
        ps_ref[:, 0:1] = pn
        acc_ref[...] = pn.astype(BF).astype(F32) * vn_ref[0].astype(BF).astype(F32)

    @pl.when(s >= ns)
    def _():
        for k in range(pps):
            pg = (s - ns) * pps + k
            p = jnp.exp(l_ref[pg] - m_ref[:, 0:1])
            ps_ref[...] += p
            v2 = v_refs[k][0, 0].reshape(rows, HEAD_DIM).astype(BF)
            acc_ref[...] += jnp.dot(p.astype(BF), v2, preferred_element_type=F32)

    @pl.when(s == 2 * ns - 1)
    def _():
        o_ref[0] = acc_ref[...] / jnp.sum(ps_ref[...], axis=-1, keepdims=True)


def _paged_attention(page_table, slopes, q_s, k_new, v_new, bias, bias_new, pool_k, pool_v, *, block_bias):
    db, n_pages = page_table.shape
    page = pool_k.shape[2]
    pps = PAGES_PER_STEP
    assert n_pages % pps == 0
    ns = n_pages // pps
    rows = page * N_HEADS
    pool_blk = (1, 1, page, N_HEADS, HEAD_DIM)
    k_spec = lambda k: pl.BlockSpec(pool_blk, lambda b, s, pt, sl: (0, pt[b, jnp.minimum(s, ns - 1) * pps + k], 0, 0, 0))
    v_spec = lambda k: pl.BlockSpec(pool_blk, lambda b, s, pt, sl: (0, pt[b, jnp.maximum(s - ns, 0) * pps + k], 0, 0, 0))
    per_seq = lambda shape: pl.BlockSpec((1,) + shape, lambda b, s, pt, sl: (b, 0, 0))
    if block_bias:
        bias_spec = per_seq(bias.shape[1:])
    else:
        bias_spec = pl.BlockSpec((1, 1, pps * rows), lambda b, s, pt, sl: (b, 0, jnp.minimum(s, ns - 1)))
    grid_spec = pltpu.PrefetchScalarGridSpec(
        num_scalar_prefetch=2,
        grid=(db, 2 * ns),
        in_specs=([per_seq((N_HEADS, HEAD_DIM))] * 3 + [bias_spec, per_seq((1, LANES))]
                  + [k_spec(k) for k in range(pps)] + [v_spec(k) for k in range(pps)]),
        out_specs=per_seq((N_HEADS, HEAD_DIM)),
        scratch_shapes=[pltpu.VMEM((n_pages, N_HEADS, rows), F32), pltpu.VMEM((N_HEADS, rows), F32),
                        pltpu.VMEM((N_HEADS, LANES), F32), pltpu.VMEM((N_HEADS, rows), F32),
                        pltpu.VMEM((N_HEADS, HEAD_DIM), F32), pltpu.VMEM((N_HEADS, LANES), F32)],
    )
    return pl.pallas_call(
        functools.partial(_paged_attention_kernel, page=page, block_bias=block_bias),
        grid_spec=grid_spec,
        out_shape=jax.ShapeDtypeStruct((db, N_HEADS, HEAD_DIM), F32),
        compiler_params=_cp(("arbitrary", "arbitrary")),
        name="paged_attention",
    )(page_table, slopes, q_s, k_new, v_new, bias, bias_new, *([pool_k] * pps), *([pool_v] * pps))


def _alibi_slopes():
    n = 2 * N_HEADS
    i = jnp.arange(1, n + 1, dtype=F32)
    m = jnp.exp2(-8.0 * i / n)
    return m[0::2], m[1::2]


def _heads_major(a, b, t):
    return a.reshape(b, t, N_HEADS, HEAD_DIM).transpose(0, 2, 1, 3)


def _kt_steps(k, b, t):
    ns = t // SB
    return k.astype(BF).reshape(b, ns, SB, N_HEADS, HEAD_DIM).transpose(0, 3, 1, 4, 2)


def _v_steps(v, b, t):
    ns = t // SB
    vb = v.astype(BF).reshape(b, ns, SB, N_HEADS, HEAD_DIM).transpose(0, 3, 1, 2, 4)
    ones = jnp.ones(vb.shape[:-1] + (1,), BF)
    zeros = jnp.zeros(vb.shape[:-1] + (LANES - HEAD_DIM - 1,), BF)
    return jnp.concatenate([vb, ones, zeros], axis=-1)


def kernel(x_prompt, x_sample, cache_moba_k, cache_moba_v, cache_dsa_k, cache_dsa_v, cache_dsa_kidx, state_ffn_conv, page_table, c_prompt, c_sample, w_ada, b_ada, g_attn, w_in, qn_a, kn_a, qn_b, kn_b, w_out, g_ffn, w_up, conv_w, conv_b, w_down):
    bsz, t, d = x_prompt.shape
    db, ds, _ = x_sample.shape
    depth = w_in.shape[0]
    assert depth == 1 and ds == 1 and t % SB == 0
    f = w_down.shape[1]
    nchunk = f // FF_CHUNK
    assert nchunk * FF_CHUNK == f
    slopes_a, slopes_b = _alibi_slopes()
    l = 0

    ncols = N_QKV_CHUNKS * WIDTH
    wqkv = w_in[l][:, :ncols].astype(BF)
    wkw = jnp.pad(w_in[l][:, ncols:], ((0, 0), (0, LANES - (w_in.shape[2] - ncols)))).astype(BF)
    nrm = jnp.stack([jnp.tile(g[l], N_HEADS) for g in (qn_a, kn_a, qn_b, kn_b)])
    hid = jnp.arange(WIDTH) // HEAD_DIM
    gmat = (hid[:, None] == hid[None, :]).astype(BF)
    wout = w_out[l].astype(BF)
    wupa = w_up[l][:, :f].astype(BF).reshape(d, nchunk, FF_CHUNK).transpose(1, 0, 2)
    wupg = w_up[l][:, f:].astype(BF).reshape(d, nchunk, FF_CHUNK).transpose(1, 0, 2)
    wdn = w_down[l].astype(BF).reshape(nchunk, FF_CHUNK, d)
    g_attn_l = g_attn[l].reshape(1, d)
    g_ffn_l = g_ffn[l].reshape(1, d)
    cb = conv_b[l].reshape(1, f)

    rc = bsz + db
    rpad = -rc % SUBLANES
    c_all = jnp.concatenate([c_prompt, c_sample, jnp.zeros((rpad, d), F32)], axis=0)
    mod = _modulation(c_all, w_ada[l], b_ada[l])
    mods_p = [m.reshape(bsz, 1, d) for m in jnp.split(mod[:bsz], N_MOD, axis=-1)]
    mods_s = [m.reshape(1, db, d) for m in jnp.split(mod[bsz:rc], N_MOD, axis=-1)]

    n = bsz * t
    xp2 = x_prompt.reshape(n, d)
    tm = 512
    qa, ka, va, qb, kb, vb, qi, kw, kmean = _pre_project(
        xp2, mods_p[0], mods_p[1], g_attn_l, wqkv, wkw, nrm, gmat, tm=tm, rows_per_group=t, with_kmean=True)
    nb = t // TQ
    assert nb <= LANES
    q_a = _heads_major((qa * ATTN_SCALE).astype(BF), bsz, t)
    q_b = _heads_major((qb * ATTN_SCALE).astype(BF), bsz, t)
    kmt = kmean.reshape(bsz, nb, N_HEADS, HEAD_DIM).transpose(0, 2, 3, 1).astype(BF)
    kmt = jnp.pad(kmt, ((0, 0), (0, 0), (0, 0), (0, LANES - nb)))
    o_a = _moba_prompt(slopes_a, q_a, _kt_steps(ka, bsz, t), _v_steps(va, bsz, t), kmt)
    qi_h = qi.astype(BF).reshape(bsz, t, IDX_HEADS, IDX_DIM).transpose(0, 2, 1, 3)
    kit = kw[:, :IDX_DIM].astype(BF).reshape(bsz, t // SB, SB, IDX_DIM).transpose(0, 1, 3, 2)
    n_sel = min(DSA_TOPK, t // 4)
    o_b = _dsa_prompt(slopes_b, qi_h, kw, kit, q_b, _kt_steps(kb, bsz, t), _v_steps(vb, bsz, t), n_sel)
    o_a2 = o_a.transpose(0, 2, 1, 3).reshape(n, WIDTH)
    o_b2 = o_b.transpose(0, 2, 1, 3).reshape(n, WIDTH)
    y_p, conv_p = _finish(xp2, o_a2, o_b2, mods_p[2:], g_ffn_l, wout, wupa, wupg, conv_w[l], cb, wdn,
                          tm=tm, rows_per_group=t)

    kv_shape = (1, bsz, t, N_HEADS, HEAD_DIM)
    outs_p = (ka.reshape(kv_shape), va.reshape(kv_shape), kb.reshape(kv_shape),
              vb.reshape(kv_shape), kw[:, :IDX_DIM].reshape(1, bsz, t, IDX_DIM), conv_p[None])

    page = cache_moba_k.shape[2]
    n_pages = page_table.shape[1]
    past = n_pages * page
    assert past % MOBA_BLOCK == 0 and db % SUBLANES == 0 and cache_moba_k.shape[0] == 1
    xs2 = x_sample.reshape(db, d)
    qa_s, ka_s, va_s, qb_s, kb_s, vb_s, qi_s, kw_s = _pre_project(
        xs2, mods_s[0], mods_s[1], g_attn_l, wqkv, wkw, nrm, gmat, tm=db, rows_per_group=db, with_kmean=False)
    heads3 = lambda a: a.reshape(db, N_HEADS, HEAD_DIM)
    scores, score_new, bias_blk = _sample_stats(
        page_table, heads3(qi_s), kw_s[:, IDX_DIM:IDX_DIM + IDX_HEADS].reshape(db, IDX_HEADS, 1),
        kw_s[:, :IDX_DIM].reshape(db, 1, IDX_DIM), heads3(qa_s), cache_dsa_kidx, cache_moba_k)
    n_sel_s = min(DSA_TOPK, (past + ds) // 4)
    bias_pos, bias_new = _dsa_sample_select(scores.reshape(db, past), score_new.reshape(db, LANES), n_sel_s)
    o_a_s = _paged_attention(page_table, slopes_a, heads3(qa_s), heads3(ka_s), heads3(va_s), bias_blk,
                             jnp.zeros((db, 1, LANES), F32), cache_moba_k, cache_moba_v, block_bias=True)
    o_b_s = _paged_attention(page_table, slopes_b, heads3(qb_s), heads3(kb_s), heads3(vb_s),
                             bias_pos.reshape(db, 1, past * N_HEADS), bias_new.reshape(db, 1, LANES),
                             cache_dsa_k, cache_dsa_v, block_bias=False)
    state = state_ffn_conv[l]
    y_s, a_s = _finish(xs2, o_a_s.reshape(db, WIDTH), o_b_s.reshape(db, WIDTH), mods_s[2:], g_ffn_l, wout, wupa, wupg,
                       conv_w[l], cb, wdn, tm=db, rows_per_group=db, prefix=(state[:, 0], state[:, 1]))
    conv_s = jnp.stack([state[:, 1], a_s], axis=1)
    kvs_shape = (1, db, ds, N_HEADS, HEAD_DIM)
    outs_s = (ka_s.reshape(kvs_shape), va_s.reshape(kvs_shape), kb_s.reshape(kvs_shape), vb_s.reshape(kvs_shape),
              kw_s[:, :IDX_DIM].reshape(1, db, ds, IDX_DIM), conv_s[None])
    return (y_p.reshape(bsz, t, d), y_s.reshape(db, ds, d)) + outs_p + outs_s
```

```python
import functools

import jax
import jax.numpy as jnp
from jax import lax
from jax.experimental import pallas as pl
from jax.experimental.pallas import tpu as pltpu

HEAD_DIM = 64
N_HEADS = 8
WIDTH = N_HEADS * HEAD_DIM
MOBA_BLOCK = 256
MOBA_TOPK = 3
DSA_TOPK = 256
IDX_HEADS = 8
IDX_DIM = 64
CONV_WIDTH = 3
EPS = 1e-6
N_MOD = 6
ATTN_SCALE = HEAD_DIM ** -0.5
N_QKV_CHUNKS = 7
LANES = 128
SUBLANES = 8
TQ = MOBA_BLOCK
SB = 1024
BPS = SB // TQ
FF_CHUNK = 256
PAGES_PER_STEP = 4
VMEM_LIMIT = 56 * 1024 * 1024

BF = jnp.bfloat16
F32 = jnp.float32
I32 = jnp.int32
NEG_INF = float("-inf")
INT_MIN = -2 ** 31


def _cp(sem):
    return pltpu.CompilerParams(dimension_semantics=sem, vmem_limit_bytes=VMEM_LIMIT)


def _const_spec(shape):
    nd = len(shape)
    return pl.BlockSpec(shape, lambda *_: (0,) * nd)


def _dot_nt(a, b):
    return lax.dot_general(a, b, (((1,), (1,)), ((), ())), preferred_element_type=F32)


def _mod_kernel(c_ref, w_ref, b_ref, o_ref):
    s = jax.nn.silu(c_ref[...])
    o_ref[...] = jnp.dot(s.astype(BF), w_ref[...].astype(BF), preferred_element_type=F32) + b_ref[...]


def _modulation(c_all, w_ada, b_ada):
    r, d = c_all.shape
    n = w_ada.shape[1]
    tn = 1024
    return pl.pallas_call(
        _mod_kernel,
        grid=(n // tn,),
        in_specs=[pl.BlockSpec((r, d), lambda j: (0, 0)),
                  pl.BlockSpec((d, tn), lambda j: (0, j)),
                  pl.BlockSpec((1, tn), lambda j: (0, j))],
        out_specs=pl.BlockSpec((r, tn), lambda j: (0, j)),
        out_shape=jax.ShapeDtypeStruct((r, n), F32),
        compiler_params=_cp(("arbitrary",)),
        name="modulation",
    )(c_all, w_ada, b_ada.reshape(1, n))


def _pre_kernel(x_ref, shift_ref, scale_ref, g_ref, wqkv_ref, wkw_ref, nrm_ref, gmat_ref, *outs, tm, with_kmean):
    x = x_ref[...]
    ms = jnp.mean(x * x, axis=-1, keepdims=True)
    xn = x * lax.rsqrt(ms + EPS) * g_ref[...]
    h = (xn * (1.0 + scale_ref[0]) + shift_ref[0]).astype(BF)
    norm_row = {0: 0, 1: 1, 3: 2, 4: 3}
    for c in range(N_QKV_CHUNKS):
        p = jnp.dot(h, wqkv_ref[:, c * WIDTH:(c + 1) * WIDTH], preferred_element_type=F32)
        if c in norm_row:
            ss = jnp.dot((p * p).astype(BF), gmat_ref[...], preferred_element_type=F32)
            k = norm_row[c]
            p = p * lax.rsqrt(ss * (1.0 / HEAD_DIM) + EPS) * nrm_ref[k:k + 1, :]
        outs[c][...] = p
        if c == 1 and with_kmean:
            outs[N_QKV_CHUNKS + 1][0] = jnp.mean(p.reshape(tm // MOBA_BLOCK, MOBA_BLOCK, WIDTH), axis=1)
    outs[N_QKV_CHUNKS][...] = jnp.dot(h, wkw_ref[...], preferred_element_type=F32)


def _pre_project(x2, shift, scale, g_attn, wqkv, wkw, nrm, gmat, *, tm, rows_per_group, with_kmean):
    n, d = x2.shape
    grid = (n // tm,)
    mr = shift.shape[1]
    tiles_per_group = rows_per_group // tm
    mod_spec = pl.BlockSpec((1, mr, d), lambda i: (i // tiles_per_group, 0, 0))
    row_spec = lambda w: pl.BlockSpec((tm, w), lambda i: (i, 0))
    out_shape = [jax.ShapeDtypeStruct((n, WIDTH), F32)] * N_QKV_CHUNKS + [jax.ShapeDtypeStruct((n, LANES), F32)]
    out_specs = [row_spec(WIDTH)] * N_QKV_CHUNKS + [row_spec(LANES)]
    if with_kmean:
        nb = tm // MOBA_BLOCK
        out_shape.append(jax.ShapeDtypeStruct((n // tm, nb, WIDTH), F32))
        out_specs.append(pl.BlockSpec((1, nb, WIDTH), lambda i: (i, 0, 0)))
    return pl.pallas_call(
        functools.partial(_pre_kernel, tm=tm, with_kmean=with_kmean),
        grid=grid,
        in_specs=[row_spec(d), mod_spec, mod_spec, _const_spec((1, d)),
                  _const_spec(wqkv.shape), _const_spec(wkw.shape), _const_spec(nrm.shape), _const_spec(gmat.shape)],
        out_specs=out_specs,
        out_shape=out_shape,
        compiler_params=_cp(("arbitrary",)),
        name="pre_project",
    )(x2, shift, scale, g_attn, wqkv, wkw, nrm, gmat)


def _two_pass_attention(q, kt_at, v_at, slope, i, bias_at, l_ref, m_ref, acc_ref):
    n_steps = i // BPS + 1
    m_ref[...] = jnp.full((TQ, TQ), NEG_INF, F32)

    def pass1(j, carry):
        cols = lax.broadcasted_iota(I32, (1, SB), 1) + (j * SB - i * TQ)
        l = jnp.dot(q, kt_at(j), preferred_element_type=F32) + slope * cols.astype(F32) + bias_at(j)
        l_ref[j] = l
        mx = m_ref[...]
        for k in range(BPS):
            mx = jnp.maximum(mx, l[:, k * TQ:(k + 1) * TQ])
        m_ref[...] = mx
        return carry

    lax.fori_loop(0, n_steps, pass1, 0)
    m = jnp.max(m_ref[...], axis=-1, keepdims=True)
    acc_ref[...] = jnp.zeros((TQ, LANES), F32)

    def pass2(j, carry):
        p = jnp.exp(l_ref[j] - m).astype(BF)
        acc_ref[...] += jnp.dot(p, v_at(j), preferred_element_type=F32)
        return carry

    lax.fori_loop(0, n_steps, pass2, 0)
    acc = acc_ref[...]
    return acc[:, :HEAD_DIM] / acc[:, HEAD_DIM:HEAD_DIM + 1]


def _moba_prompt_kernel(slopes_ref, q_ref, kt_ref, v_ref, kmt_ref, o_ref, l_ref, m_ref, acc_ref, bias_ref, sel_ref):
    h = pl.program_id(1)
    i = pl.program_id(2)
    sub = i % BPS
    row0 = pl.multiple_of(sub * TQ, TQ)

    @pl.when(sub == 0)
    def _():
        s = jnp.dot(q_ref[0, 0], kmt_ref[0, 0], preferred_element_type=F32)
        lane_s = lax.broadcasted_iota(I32, s.shape, 1)
        own = i + lax.broadcasted_iota(I32, s.shape, 0) // TQ
        tri = (lax.broadcasted_iota(I32, (LANES, LANES), 0) <= lax.broadcasted_iota(I32, (LANES, LANES), 1))
        tri = jnp.where(tri, 1.0, 0.0).astype(BF)
        s = jnp.where(lane_s < own, s, NEG_INF)
        bias = jnp.full(s.shape, NEG_INF, F32)
        for _ in range(MOBA_TOPK):
            mx = jnp.max(s, axis=-1, keepdims=True)
            eq = s == mx
            rank = jnp.dot(jnp.where(eq, 1.0, 0.0).astype(BF), tri, preferred_element_type=F32)
            pick = eq & (rank == 1.0) & (mx > NEG_INF)
            bias = jnp.where(pick, 0.0, bias)
            s = jnp.where(pick, NEG_INF, s)
        sel_ref[...] = bias

    q = q_ref[0, 0, pl.ds(row0, TQ), :]
    bias_ref[...] = sel_ref[pl.ds(row0, TQ), :]
    lane = lax.broadcasted_iota(I32, (TQ, LANES), 1)
    r = lax.broadcasted_iota(I32, (TQ, TQ), 0)
    c = lax.broadcasted_iota(I32, (TQ, TQ), 1)
    causal = jnp.where(c <= r, 0.0, NEG_INF).astype(F32)

    def bias_at(j):
        parts = []
        for k in range(BPS):
            blk = j * BPS + k
            col = jnp.max(jnp.where(lane == blk, bias_ref[...], NEG_INF), axis=-1, keepdims=True)
            parts.append(jnp.where(blk == i, causal, jnp.broadcast_to(col, (TQ, TQ))))
        return jnp.concatenate(parts, axis=1)

    o_ref[0, 0] = _two_pass_attention(
        q, lambda j: kt_ref[0, 0, j], lambda j: v_ref[0, 0, j], slopes_ref[h], i, bias_at, l_ref, m_ref, acc_ref)


def _moba_prompt(slopes, q, kt, v, kmt):
    b, h, t, _ = q.shape
    nq = t // TQ
    nsb = t // SB
    grid_spec = pltpu.PrefetchScalarGridSpec(
        num_scalar_prefetch=1,
        grid=(b, h, nq),
        in_specs=[pl.BlockSpec((1, 1, SB, HEAD_DIM), lambda bi, hi, i, s: (bi, hi, i // BPS, 0)),
                  pl.BlockSpec((1, 1, nsb, HEAD_DIM, SB), lambda bi, hi, i, s: (bi, hi, 0, 0, 0)),
                  pl.BlockSpec((1, 1, nsb, SB, LANES), lambda bi, hi, i, s: (bi, hi, 0, 0, 0)),
                  pl.BlockSpec((1, 1, HEAD_DIM, LANES), lambda bi, hi, i, s: (bi, hi, 0, 0))],
        out_specs=pl.BlockSpec((1, 1, TQ, HEAD_DIM), lambda bi, hi, i, s: (bi, hi, i, 0)),
        scratch_shapes=[pltpu.VMEM((nsb, TQ, SB), F32), pltpu.VMEM((TQ, TQ), F32),
                        pltpu.VMEM((TQ, LANES), F32), pltpu.VMEM((TQ, LANES), F32), pltpu.VMEM((SB, LANES), F32)],
    )
    return pl.pallas_call(
        _moba_prompt_kernel,
        grid_spec=grid_spec,
        out_shape=jax.ShapeDtypeStruct((b, h, t, HEAD_DIM), F32),
        compiler_params=_cp(("arbitrary", "arbitrary", "arbitrary")),
        name="moba_prompt",
    )(slopes, q, kt, v, kmt)


def _order_key(score):
    bits = lax.bitcast_convert_type(score, I32)
    return bits ^ ((bits >> 31) & 0x7FFFFFFF)


def _dsa_select(i, n_sel, idx_bits, sk_ref, b_ref, jc_ref):
    n_steps = i // BPS + 1
    rc = TQ // 2
    c = lax.broadcasted_iota(I32, (rc, SB), 1)
    for r0 in range(0, TQ, rc):
        rows = slice(r0, r0 + rc)
        gr = lax.broadcasted_iota(I32, (rc, SB), 0) + (i * TQ + r0)

        def count(pred, rows=rows):
            def body(j, acc):
                hit = jnp.where(pred(j, sk_ref[j, rows, :]), 1.0, 0.0)
                for k in range(SB // LANES):
                    acc = acc + hit[:, k * LANES:(k + 1) * LANES]
                return acc

            acc = lax.fori_loop(0, n_steps, body, jnp.zeros((rc, LANES), F32))
            return jnp.sum(acc, axis=-1, keepdims=True)

        def bit_step(it, v, count=count):
            cand = v + jnp.left_shift(jnp.int32(1), 31 - it)
            cnt = count(lambda j, key: key >= cand)
            return jnp.where(cnt >= n_sel, cand, v)

        v = lax.fori_loop(0, 32, bit_step, jnp.full((rc, 1), INT_MIN, I32))
        cnt_gt = count(lambda j, key: key > v)
        cnt_ge = count(lambda j, key: key >= v)
        need = n_sel - cnt_gt
        trim = (cnt_ge > n_sel) & (v > INT_MIN)
        jc_ref[rows, :] = jnp.full((rc, LANES), 2 ** idx_bits, I32)

        @pl.when(jnp.max(jnp.where(trim, 1.0, 0.0)) > 0.0)
        def _(rows=rows, count=count, v=v, need=need, trim=trim):
            def idx_step(it, jcap):
                cand = jcap + jnp.left_shift(jnp.int32(1), idx_bits - 1 - it)
                cnt = count(lambda j, key: (key == v) & (c + j * SB < cand))
                return jnp.where(cnt < need, cand, jcap)

            jcap = lax.fori_loop(0, idx_bits, idx_step, jnp.zeros((rc, 1), I32))
            jc_ref[rows, :] = jnp.broadcast_to(jnp.where(trim, jcap, 2 ** idx_bits), (rc, LANES))

        jcap = jc_ref[rows, 0:1]

        def write(j, carry, rows=rows, v=v, jcap=jcap, gr=gr):
            key = sk_ref[j, rows, :]
            gc = c + j * SB
            sel = ((key > v) | ((key == v) & (gc <= jcap))) & (gc <= gr)
            b_ref[j, rows, :] = jnp.where(sel, 0.0, NEG_INF)
            return carry

        lax.fori_loop(0, n_steps, write, 0)


def _dsa_prompt_kernel(slopes_ref, qi_ref, kw_ref, kit_ref, q_ref, kt_ref, v_ref, o_ref,
                       sk_ref, b_ref, l_ref, m_ref, acc_ref, jc_ref, *, n_sel, idx_bits):
    i = pl.program_id(1)
    h = pl.program_id(2)

    @pl.when(h == 0)
    def _():
        wi = kw_ref[...] * (IDX_HEADS * IDX_DIM) ** -0.5
        wcols = [wi[:, IDX_DIM + hh:IDX_DIM + hh + 1] for hh in range(IDX_HEADS)]
        gr = lax.broadcasted_iota(I32, (TQ, SB), 0) + i * TQ
        c = lax.broadcasted_iota(I32, (TQ, SB), 1)

        def score_step(j, carry):
            kit = kit_ref[0, j]
            sc = jnp.maximum(jnp.dot(qi_ref[0, 0], kit, preferred_element_type=F32), 0.0) * wcols[0]
            for hh in range(1, IDX_HEADS):
                sc = sc + jnp.maximum(jnp.dot(qi_ref[0, hh], kit, preferred_element_type=F32), 0.0) * wcols[hh]
            sk_ref[j] = jnp.where(c + j * SB <= gr, _order_key(sc), INT_MIN)
            return carry

        lax.fori_loop(0, i // BPS + 1, score_step, 0)
        _dsa_select(i, n_sel, idx_bits, sk_ref, b_ref, jc_ref)

    o_ref[0, 0] = _two_pass_attention(
        q_ref[0, 0], lambda j: kt_ref[0, 0, j], lambda j: v_ref[0, 0, j], slopes_ref[h], i,
        lambda j: b_ref[j], l_ref, m_ref, acc_ref)


def _dsa_prompt(slopes, qi, kw, kit, q, kt, v, n_sel):
    b, h, t, _ = q.shape
    nq = t // TQ
    nsb = t // SB
    idx_bits = max(1, (t - 1).bit_length())
    blk = lambda shape, fn: pl.BlockSpec(shape, fn)
    grid_spec = pltpu.PrefetchScalarGridSpec(
        num_scalar_prefetch=1,
        grid=(b, nq, h),
        in_specs=[blk((1, IDX_HEADS, TQ, IDX_DIM), lambda bi, i, hi, s: (bi, 0, i, 0)),
                  blk((TQ, LANES), lambda bi, i, hi, s: (bi * nq + i, 0)),
                  blk((1, nsb, IDX_DIM, SB), lambda bi, i, hi, s: (bi, 0, 0, 0)),
                  blk((1, 1, TQ, HEAD_DIM), lambda bi, i, hi, s: (bi, hi, i, 0)),
                  blk((1, 1, nsb, HEAD_DIM, SB), lambda bi, i, hi, s: (bi, hi, 0, 0, 0)),
                  blk((1, 1, nsb, SB, LANES), lambda bi, i, hi, s: (bi, hi, 0, 0, 0))],
        out_specs=blk((1, 1, TQ, HEAD_DIM), lambda bi, i, hi, s: (bi, hi, i, 0)),
        scratch_shapes=[pltpu.VMEM((nsb, TQ, SB), I32), pltpu.VMEM((nsb, TQ, SB), F32), pltpu.VMEM((nsb, TQ, SB), F32),
                        pltpu.VMEM((TQ, TQ), F32), pltpu.VMEM((TQ, LANES), F32), pltpu.VMEM((TQ, LANES), I32)],
    )
    return pl.pallas_call(
        functools.partial(_dsa_prompt_kernel, n_sel=n_sel, idx_bits=idx_bits),
        grid_spec=grid_spec,
        out_shape=jax.ShapeDtypeStruct((b, h, t, HEAD_DIM), F32),
        compiler_params=_cp(("arbitrary", "arbitrary", "arbitrary")),
        name="dsa_prompt",
    )(slopes, qi, kw, kit, q, kt, v)


def _finish_kernel(*refs, tm, nchunk, seq_mode, tiles_per_seq):
    (x_ref, oa_ref, ob_ref, gate_a_ref, shift_f_ref, scale_f_ref, gate_f_ref, g_ref, wout_ref,
     wupa_ref, wupg_ref, cw_ref, cb_ref, wdn_ref) = refs[:14]
    if seq_mode:
        y_ref, st_ref, abuf_ref, carry_ref, yacc_ref = refs[14:]
    else:
        pre0_ref, pre1_ref, y_ref, st_ref, yacc_ref = refs[14:]
    i = pl.program_id(0)
    o = (jnp.dot(oa_ref[...].astype(BF), wout_ref[0:WIDTH, :], preferred_element_type=F32)
         + jnp.dot(ob_ref[...].astype(BF), wout_ref[WIDTH:2 * WIDTH, :], preferred_element_type=F32))
    x1 = x_ref[...] + gate_a_ref[0] * o
    ms = jnp.mean(x1 * x1, axis=-1, keepdims=True)
    h = ((x1 * lax.rsqrt(ms + EPS) * g_ref[...]) * (1.0 + scale_f_ref[0]) + shift_f_ref[0]).astype(BF)
    yacc_ref[...] = jnp.zeros_like(yacc_ref)
    for c in range(nchunk):
        cols = slice(c * FF_CHUNK, (c + 1) * FF_CHUNK)
        a = jnp.dot(h, wupa_ref[c], preferred_element_type=F32)
        gt = jnp.dot(h, wupg_ref[c], preferred_element_type=F32)
        if seq_mode:
            first = (i % tiles_per_seq) == 0
            buf = abuf_ref.at[c % 2]
            buf[0:SUBLANES, :] = jnp.where(first, 0.0, carry_ref[c])
            buf[SUBLANES:SUBLANES + tm, :] = a
            a1 = buf[SUBLANES - 1:SUBLANES - 1 + tm, :]
            a2 = buf[SUBLANES - 2:SUBLANES - 2 + tm, :]
            carry_ref[c] = a[tm - SUBLANES:tm, :]
            st_ref[0, :, cols] = a[tm - (CONV_WIDTH - 1):tm, :]
        else:
            a2 = pre0_ref[:, cols]
            a1 = pre1_ref[:, cols]
            st_ref[:, cols] = a
        conv = cb_ref[:, cols] + a2 * cw_ref[0:1, cols]
        conv = conv + a1 * cw_ref[1:2, cols]
        conv = conv + a * cw_ref[2:3, cols]
        act = (jax.nn.gelu(conv) * gt).astype(BF)
        yacc_ref[...] += jnp.dot(act, wdn_ref[c], preferred_element_type=F32)
    y_ref[...] = x1 + gate_f_ref[0] * yacc_ref[...]


def _finish(x2, oa, ob, mods, g_ffn, wout, wupa, wupg, conv_w, conv_b, wdn, *, tm, rows_per_group, prefix=None):
    n, d = x2.shape
    nchunk = wupa.shape[0]
    f = nchunk * FF_CHUNK
    seq_mode = prefix is None
    tiles_per_seq = rows_per_group // tm
    mr = mods[0].shape[1]
    mod_spec = pl.BlockSpec((1, mr, d), lambda i: (i // tiles_per_seq, 0, 0))
    row_spec = lambda w: pl.BlockSpec((tm, w), lambda i: (i, 0))
    in_specs = ([row_spec(d), row_spec(WIDTH), row_spec(WIDTH)] + [mod_spec] * 4 +
                [_const_spec((1, d)), _const_spec(wout.shape), _const_spec(wupa.shape), _const_spec(wupg.shape),
                 _const_spec(conv_w.shape), _const_spec((1, f)), _const_spec(wdn.shape)])
    args = [x2, oa, ob, *mods, g_ffn, wout, wupa, wupg, conv_w, conv_b, wdn]
    scratch = []
    if seq_mode:
        nseq = n // rows_per_group
        st_shape = jax.ShapeDtypeStruct((nseq, CONV_WIDTH - 1, f), F32)
        st_spec = pl.BlockSpec((1, CONV_WIDTH - 1, f), lambda i: (i // tiles_per_seq, 0, 0))
        scratch += [pltpu.VMEM((2, SUBLANES + tm, FF_CHUNK), F32), pltpu.VMEM((nchunk, SUBLANES, FF_CHUNK), F32)]
    else:
        in_specs += [row_spec(f), row_spec(f)]
        args += list(prefix)
        st_shape = jax.ShapeDtypeStruct((n, f), F32)
        st_spec = row_spec(f)
    scratch.append(pltpu.VMEM((tm, d), F32))
    return pl.pallas_call(
        functools.partial(_finish_kernel, tm=tm, nchunk=nchunk, seq_mode=seq_mode, tiles_per_seq=tiles_per_seq),
        grid=(n // tm,),
        in_specs=in_specs,
        out_specs=[row_spec(d), st_spec],
        out_shape=[jax.ShapeDtypeStruct((n, d), F32), st_shape],
        scratch_shapes=scratch,
        compiler_params=_cp(("arbitrary",)),
        name="finish",
    )(*args)


def _block_diag_rows(row):
    hrow = lax.broadcasted_iota(I32, (N_HEADS, WIDTH), 0)
    hlane = lax.broadcasted_iota(I32, (N_HEADS, WIDTH), 1) // HEAD_DIM
    return jnp.where(hrow == hlane, jnp.broadcast_to(row, (N_HEADS, WIDTH)), 0.0)


def _sample_stats_kernel(pt_ref, qi_ref, wi_ref, kin_ref, qa_ref, *refs, page, nblk):
    pps = PAGES_PER_STEP
    kidx_refs = refs[:pps]
    kmoba_refs = refs[pps:2 * pps]
    sc_ref, scn_ref, bb_ref, km_ref = refs[2 * pps:]
    s = pl.program_id(1)
    qi = qi_ref[0].astype(BF)
    wi = wi_ref[0] * (IDX_HEADS * IDX_DIM) ** -0.5
    ppb = MOBA_BLOCK // page
    lane = lax.broadcasted_iota(I32, (WIDTH, LANES), 1)

    @pl.when(s == 0)
    def _():
        km_ref[...] = jnp.zeros(km_ref.shape, F32)

    for k in range(pps):
        rel = jnp.maximum(jnp.dot(qi, kidx_refs[k][0, 0].astype(BF), preferred_element_type=F32), 0.0)
        sc_ref[0, :, k * page:(k + 1) * page] = jnp.sum(rel * wi, axis=0, keepdims=True)
    for kb in range(pps // ppb):
        tot = kmoba_refs[kb * ppb][0, 0].reshape(WIDTH, page)
        for k in range(1, ppb):
            tot = tot + kmoba_refs[kb * ppb + k][0, 0].reshape(WIDTH, page)
        mean = jnp.sum(tot, axis=-1, keepdims=True) * (1.0 / MOBA_BLOCK)
        km_ref[...] = jnp.where(lane == s * (pps // ppb) + kb, mean, km_ref[...])

    @pl.when(s == pl.num_programs(1) - 1)
    def _():
        kin = jnp.broadcast_to(kin_ref[0], (SUBLANES, IDX_DIM)).astype(BF)
        reln = jnp.maximum(_dot_nt(qi, kin), 0.0)
        scn = jnp.sum(reln[:, 0:1] * wi, axis=0, keepdims=True)
        scn_ref[0] = jnp.broadcast_to(scn, (1, LANES))
        qbd = _block_diag_rows(qa_ref[0]).astype(BF)
        sb = jnp.dot(qbd, km_ref[...].astype(BF), preferred_element_type=F32)
        lane_b = lax.broadcasted_iota(I32, sb.shape, 1)
        lanef = lane_b.astype(F32)
        sb = jnp.where(lane_b < nblk, sb, NEG_INF)
        bias = jnp.full(sb.shape, NEG_INF, F32)
        for _ in range(MOBA_TOPK):
            mx = jnp.max(sb, axis=-1, keepdims=True)
            first = jnp.min(jnp.where(sb == mx, lanef, float(LANES)), axis=-1, keepdims=True)
            pick = (lanef == first) & (mx > NEG_INF)
            bias = jnp.where(pick, 0.0, bias)
            sb = jnp.where(pick, NEG_INF, sb)
        bb_ref[0] = bias


def _sample_stats(page_table, qi_s, wi_s, ki_s, qa_s, pool_kidx, pool_mk):
    db, n_pages = page_table.shape
    page = pool_kidx.shape[3]
    pps = PAGES_PER_STEP
    past = n_pages * page
    nblk = past // MOBA_BLOCK
    assert n_pages % pps == 0 and pps % (MOBA_BLOCK // page) == 0 and nblk <= LANES and page == LANES
    kidx_spec = lambda k: pl.BlockSpec((1, 1, IDX_DIM, page), lambda b, s, pt: (0, pt[b, s * pps + k], 0, 0))
    kmoba_spec = lambda k: pl.BlockSpec((1, 1, N_HEADS, HEAD_DIM, page),
                                        lambda b, s, pt: (0, pt[b, s * pps + k], 0, 0, 0))
    per_seq = lambda shape: pl.BlockSpec((1,) + shape, lambda b, s, pt: (b, 0, 0))
    grid_spec = pltpu.PrefetchScalarGridSpec(
        num_scalar_prefetch=1,
        grid=(db, n_pages // pps),
        in_specs=([per_seq((IDX_HEADS, IDX_DIM)), per_seq((IDX_HEADS, 1)), per_seq((1, IDX_DIM)), per_seq((1, WIDTH))]
                  + [kidx_spec(k) for k in range(pps)] + [kmoba_spec(k) for k in range(pps)]),
        out_specs=[pl.BlockSpec((1, 1, pps * page), lambda b, s, pt: (b, 0, s)),
                   per_seq((1, LANES)), per_seq((N_HEADS, LANES))],
        scratch_shapes=[pltpu.VMEM((WIDTH, LANES), F32)],
    )
    return pl.pallas_call(
        functools.partial(_sample_stats_kernel, page=page, nblk=nblk),
        grid_spec=grid_spec,
        out_shape=[jax.ShapeDtypeStruct((db, 1, past), F32), jax.ShapeDtypeStruct((db, 1, LANES), F32),
                   jax.ShapeDtypeStruct((db, N_HEADS, LANES), F32)],
        compiler_params=_cp(("arbitrary", "arbitrary")),
        name="sample_stats",
    )(page_table, qi_s, wi_s, ki_s, qa_s, *([pool_kidx] * pps), *([pool_mk] * pps))


def _dsa_sample_select_kernel(sc_ref, scn_ref, b_ref, bn_ref, *, n_sel, idx_bits):
    key = _order_key(sc_ref[...])
    keyn = _order_key(scn_ref[:, 0:1])
    db, past = key.shape
    gidx = lax.broadcasted_iota(I32, key.shape, 1)

    def count(pred, pred_new):
        return (jnp.sum(jnp.where(pred, 1.0, 0.0), axis=-1, keepdims=True) + jnp.where(pred_new, 1.0, 0.0))

    def bit_step(it, v):
        cand = v + jnp.left_shift(jnp.int32(1), 31 - it)
        cnt = count(key >= cand, keyn >= cand)
        return jnp.where(cnt >= n_sel, cand, v)

    v = lax.fori_loop(0, 32, bit_step, jnp.full((db, 1), INT_MIN, I32))
    cnt_gt = count(key > v, keyn > v)
    cnt_ge = count(key >= v, keyn >= v)
    need = n_sel - cnt_gt
    trim = (cnt_ge > n_sel) & (v > INT_MIN)

    def idx_step(it, jcap):
        cand = jcap + jnp.left_shift(jnp.int32(1), idx_bits - 1 - it)
        cnt = count((key == v) & (gidx < cand), (keyn == v) & (past < cand))
        return jnp.where(cnt < need, cand, jcap)

    jcap = lax.fori_loop(0, idx_bits, idx_step, jnp.zeros((db, 1), I32))
    jcap = jnp.where(trim, jcap, 2 ** idx_bits)
    sel = (key > v) | ((key == v) & (gidx <= jcap))
    seln = (keyn > v) | ((keyn == v) & (past <= jcap))
    bn_ref[...] = jnp.broadcast_to(jnp.where(seln, 0.0, NEG_INF), bn_ref.shape)
    b_ref[...] = jnp.where(sel, 0.0, NEG_INF)


def _dsa_sample_select(scores, score_new, n_sel):
    db, past = scores.shape
    idx_bits = past.bit_length()
    return pl.pallas_call(
        functools.partial(_dsa_sample_select_kernel, n_sel=n_sel, idx_bits=idx_bits),
        out_shape=[jax.ShapeDtypeStruct((db, past), F32), jax.ShapeDtypeStruct((db, LANES), F32)],
        compiler_params=pltpu.CompilerParams(vmem_limit_bytes=VMEM_LIMIT),
        name="dsa_sample_select",
    )(scores, score_new)


def _paged_attention_kernel(pt_ref, slopes_ref, q_ref, kn_ref, vn_ref, bias_ref, bn_ref, *refs, page, block_bias):
    pps = PAGES_PER_STEP
    k_refs = refs[:pps]
    v_refs = refs[pps:2 * pps]
    o_ref, l_ref, mx_ref, m_ref, ps_ref, acc_ref, sl_ref, qbd_ref = refs[2 * pps:]
    s = pl.program_id(1)
    ns = pl.num_programs(1) // 2
    past = ns * pps * page
    lane = lax.broadcasted_iota(I32, (N_HEADS, page), 1)
    hrow = lax.broadcasted_iota(I32, (N_HEADS, page), 0)

    @pl.when(s == 0)
    def _():
        sl = jnp.zeros((N_HEADS, LANES), F32)
        for h in range(N_HEADS):
            sl = jnp.where(hrow[:, :LANES] == h, slopes_ref[h], sl)
        sl_ref[...] = sl
        mx_ref[...] = jnp.full(mx_ref.shape, NEG_INF, F32)
        qbd_ref[...] = _block_diag_rows(q_ref[0] * ATTN_SCALE).astype(BF)

    @pl.when(s < ns)
    def _():
        for k in range(pps):
            pg = s * pps + k
            kt = k_refs[k][0, 0].reshape(WIDTH, page).astype(BF)
            dist = (past - pg * page - lane).astype(F32)
            logits = jnp.dot(qbd_ref[...], kt, preferred_element_type=F32) - sl_ref[:, 0:1] * dist
            if block_bias:
                blk = pg // (MOBA_BLOCK // page)
                logits = logits + jnp.max(jnp.where(lane == blk, bias_ref[0], NEG_INF), axis=-1, keepdims=True)
            else:
                logits = logits + bias_ref[0, :, k * page:(k + 1) * page]
            l_ref[pg] = logits
            mx_ref[...] = jnp.maximum(mx_ref[...], logits)

    @pl.when(s == ns - 1)
    def _():
        kn = kn_ref[0].astype(BF).astype(F32)
        ln = jnp.sum(qbd_ref[...].astype(F32) * kn, axis=-1, keepdims=True) + bn_ref[0, :, 0:1]
        m = jnp.maximum(jnp.max(mx_ref[...], axis=-1, keepdims=True), ln)
        m_ref[...] = jnp.broadcast_to(m, m_ref.shape)
        pn = jnp.exp(ln - m)
        ps_ref[...] = jnp.zeros(ps_ref.shape, F32)
        ps_ref[:, 0:1] = pn
        acc_ref[...] = pn.astype(BF).astype(F32) * vn_ref[0].astype(BF).astype(F32)

    @pl.when(s >= ns)
    def _():
        for k in range(pps):
            pg = (s - ns) * pps + k
            p = jnp.exp(l_ref[pg] - m_ref[:, 0:1])
            ps_ref[...] += p
            vt = v_refs[k][0, 0].reshape(WIDTH, page).astype(BF)
            acc_ref[...] += _dot_nt(p.astype(BF), vt)

    @pl.when(s == 2 * ns - 1)
    def _():
        o = acc_ref[...] / jnp.sum(ps_ref[...], axis=-1, keepdims=True)
        hr = lax.broadcasted_iota(I32, (N_HEADS, WIDTH), 0)
        hl = lax.broadcasted_iota(I32, (N_HEADS, WIDTH), 1) // HEAD_DIM
        o_ref[0] = jnp.sum(jnp.where(hr == hl, o, 0.0), axis=0, keepdims=True)


def _paged_attention(page_table, slopes, q_s, k_new, v_new, bias, bias_new, pool_k, pool_v, *, block_bias):
    db, n_pages = page_table.shape
    page = pool_k.shape[4]
    pps = PAGES_PER_STEP
    assert n_pages % pps == 0 and page == LANES
    ns = n_pages // pps
    pool_blk = (1, 1, N_HEADS, HEAD_DIM, page)
    k_spec = lambda k: pl.BlockSpec(pool_blk, lambda b, s, pt, sl: (0, pt[b, jnp.minimum(s, ns - 1) * pps + k], 0, 0, 0))
    v_spec = lambda k: pl.BlockSpec(pool_blk, lambda b, s, pt, sl: (0, pt[b, jnp.maximum(s - ns, 0) * pps + k], 0, 0, 0))
    per_seq = lambda shape: pl.BlockSpec((1,) + shape, lambda b, s, pt, sl: (b, 0, 0))
    if block_bias:
        bias_spec = per_seq((N_HEADS, LANES))
    else:
        bias_spec = pl.BlockSpec((1, 1, pps * page), lambda b, s, pt, sl: (b, 0, jnp.minimum(s, ns - 1)))
    grid_spec = pltpu.PrefetchScalarGridSpec(
        num_scalar_prefetch=2,
        grid=(db, 2 * ns),
        in_specs=([per_seq((1, WIDTH))] * 3 + [bias_spec, per_seq((1, LANES))]
                  + [k_spec(k) for k in range(pps)] + [v_spec(k) for k in range(pps)]),
        out_specs=per_seq((1, WIDTH)),
        scratch_shapes=[pltpu.VMEM((n_pages, N_HEADS, page), F32), pltpu.VMEM((N_HEADS, page), F32),
                        pltpu.VMEM((N_HEADS, LANES), F32), pltpu.VMEM((N_HEADS, page), F32),
                        pltpu.VMEM((N_HEADS, WIDTH), F32), pltpu.VMEM((N_HEADS, LANES), F32),
                        pltpu.VMEM((N_HEADS, WIDTH), BF)],
    )
    return pl.pallas_call(
        functools.partial(_paged_attention_kernel, page=page, block_bias=block_bias),
        grid_spec=grid_spec,
        out_shape=jax.ShapeDtypeStruct((db, 1, WIDTH), F32),
        compiler_params=_cp(("arbitrary", "arbitrary")),
        name="paged_attention",
    )(page_table, slopes, q_s, k_new, v_new, bias, bias_new, *([pool_k] * pps), *([pool_v] * pps))


def _alibi_slopes():
    n = 2 * N_HEADS
    i = jnp.arange(1, n + 1, dtype=F32)
    m = jnp.exp2(-8.0 * i / n)
    return m[0::2], m[1::2]


def _heads_major(a, b, t):
    return a.reshape(b, t, N_HEADS, HEAD_DIM).transpose(0, 2, 1, 3)


def _kt_steps(k, b, t):
    ns = t // SB
    return k.astype(BF).reshape(b, ns, SB, N_HEADS, HEAD_DIM).transpose(0, 3, 1, 4, 2)


def _v_steps(v, b, t):
    ns = t // SB
    vb = v.astype(BF).reshape(b, ns, SB, N_HEADS, HEAD_DIM).transpose(0, 3, 1, 2, 4)
    ones = jnp.ones(vb.shape[:-1] + (1,), BF)
    zeros = jnp.zeros(vb.shape[:-1] + (LANES - HEAD_DIM - 1,), BF)
    return jnp.concatenate([vb, ones, zeros], axis=-1)


def kernel(x_prompt, x_sample, cache_moba_k, cache_moba_v, cache_dsa_k, cache_dsa_v, cache_dsa_kidx, state_ffn_conv, page_table, c_prompt, c_sample, w_ada, b_ada, g_attn, w_in, qn_a, kn_a, qn_b, kn_b, w_out, g_ffn, w_up, conv_w, conv_b, w_down):
    bsz, t, d = x_prompt.shape
    db, ds, _ = x_sample.shape
    depth = w_in.shape[0]
    assert depth == 1 and ds == 1 and t % SB == 0
    f = w_down.shape[1]
    nchunk = f // FF_CHUNK
    assert nchunk * FF_CHUNK == f
    slopes_a, slopes_b = _alibi_slopes()
    l = 0

    ncols = N_QKV_CHUNKS * WIDTH
    wqkv = w_in[l][:, :ncols].astype(BF)
    wkw = jnp.pad(w_in[l][:, ncols:], ((0, 0), (0, LANES - (w_in.shape[2] - ncols)))).astype(BF)
    nrm = jnp.stack([jnp.tile(g[l], N_HEADS) for g in (qn_a, kn_a, qn_b, kn_b)])
    hid = jnp.arange(WIDTH) // HEAD_DIM
    gmat = (hid[:, None] == hid[None, :]).astype(BF)
    wout = w_out[l].astype(BF)
    wupa = w_up[l][:, :f].astype(BF).reshape(d, nchunk, FF_CHUNK).transpose(1, 0, 2)
    wupg = w_up[l][:, f:].astype(BF).reshape(d, nchunk, FF_CHUNK).transpose(1, 0, 2)
    wdn = w_down[l].astype(BF).reshape(nchunk, FF_CHUNK, d)
    g_attn_l = g_attn[l].reshape(1, d)
    g_ffn_l = g_ffn[l].reshape(1, d)
    cb = conv_b[l].reshape(1, f)

    rc = bsz + db
    rpad = -rc % SUBLANES
    c_all = jnp.concatenate([c_prompt, c_sample, jnp.zeros((rpad, d), F32)], axis=0)
    mod = _modulation(c_all, w_ada[l], b_ada[l])
    mods_p = [m.reshape(bsz, 1, d) for m in jnp.split(mod[:bsz], N_MOD, axis=-1)]
    mods_s = [m.reshape(1, db, d) for m in jnp.split(mod[bsz:rc], N_MOD, axis=-1)]

    n = bsz * t
    xp2 = x_prompt.reshape(n, d)
    tm = 512
    qa, ka, va, qb, kb, vb, qi, kw, kmean = _pre_project(
        xp2, mods_p[0], mods_p[1], g_attn_l, wqkv, wkw, nrm, gmat, tm=tm, rows_per_group=t, with_kmean=True)
    nb = t // TQ
    assert nb <= LANES
    q_a = _heads_major((qa * ATTN_SCALE).astype(BF), bsz, t)
    q_b = _heads_major((qb * ATTN_SCALE).astype(BF), bsz, t)
    kmt = kmean.reshape(bsz, nb, N_HEADS, HEAD_DIM).transpose(0, 2, 3, 1).astype(BF)
    kmt = jnp.pad(kmt, ((0, 0), (0, 0), (0, 0), (0, LANES - nb)))
    o_a = _moba_prompt(slopes_a, q_a, _kt_steps(ka, bsz, t), _v_steps(va, bsz, t), kmt)
    qi_h = qi.astype(BF).reshape(bsz, t, IDX_HEADS, IDX_DIM).transpose(0, 2, 1, 3)
    kit = kw[:, :IDX_DIM].astype(BF).reshape(bsz, t // SB, SB, IDX_DIM).transpose(0, 1, 3, 2)
    n_sel = min(DSA_TOPK, t // 4)
    o_b = _dsa_prompt(slopes_b, qi_h, kw, kit, q_b, _kt_steps(kb, bsz, t), _v_steps(vb, bsz, t), n_sel)
    o_a2 = o_a.transpose(0, 2, 1, 3).reshape(n, WIDTH)
    o_b2 = o_b.transpose(0, 2, 1, 3).reshape(n, WIDTH)
    y_p, conv_p = _finish(xp2, o_a2, o_b2, mods_p[2:], g_ffn_l, wout, wupa, wupg, conv_w[l], cb, wdn,
                          tm=tm, rows_per_group=t)

    kv_shape = (1, bsz, t, N_HEADS, HEAD_DIM)
    outs_p = (ka.reshape(kv_shape), va.reshape(kv_shape), kb.reshape(kv_shape),
              vb.reshape(kv_shape), kw[:, :IDX_DIM].reshape(1, bsz, t, IDX_DIM), conv_p[None])

    page = cache_moba_k.shape[2]
    n_pages = page_table.shape[1]
    past = n_pages * page
    assert past % MOBA_BLOCK == 0 and db % SUBLANES == 0 and cache_moba_k.shape[0] == 1
    xs2 = x_sample.reshape(db, d)
    qa_s, ka_s, va_s, qb_s, kb_s, vb_s, qi_s, kw_s = _pre_project(
        xs2, mods_s[0], mods_s[1], g_attn_l, wqkv, wkw, nrm, gmat, tm=db, rows_per_group=db, with_kmean=False)
    pool = lambda cch: cch.transpose(0, 1, 3, 4, 2)
    pool_kidx = cache_dsa_kidx.transpose(0, 1, 3, 2)
    row3 = lambda a: a.reshape(db, 1, a.shape[-1])
    scores, score_new, bias_blk = _sample_stats(
        page_table, qi_s.reshape(db, IDX_HEADS, IDX_DIM), kw_s[:, IDX_DIM:IDX_DIM + IDX_HEADS].reshape(db, IDX_HEADS, 1),
        row3(kw_s[:, :IDX_DIM]), row3(qa_s), pool_kidx, pool(cache_moba_k))
    n_sel_s = min(DSA_TOPK, (past + ds) // 4)
    bias_pos, bias_new = _dsa_sample_select(scores.reshape(db, past), score_new.reshape(db, LANES), n_sel_s)
    o_a_s = _paged_attention(page_table, slopes_a, row3(qa_s), row3(ka_s), row3(va_s), bias_blk,
                             jnp.zeros((db, 1, LANES), F32), pool(cache_moba_k), pool(cache_moba_v), block_bias=True)
    o_b_s = _paged_attention(page_table, slopes_b, row3(qb_s), row3(kb_s), row3(vb_s),
                             bias_pos.reshape(db, 1, past), bias_new.reshape(db, 1, LANES),
                             pool(cache_dsa_k), pool(cache_dsa_v), block_bias=False)
    state = state_ffn_conv[l]
    y_s, a_s = _finish(xs2, o_a_s.reshape(db, WIDTH), o_b_s.reshape(db, WIDTH), mods_s[2:], g_ffn_l, wout, wupa, wupg,
                       conv_w[l], cb, wdn, tm=db, rows_per_group=db, prefix=(state[:, 0], state[:, 1]))
    conv_s = jnp.stack([state[:, 1], a_s], axis=1)
    kvs_shape = (1, db, ds, N_HEADS, HEAD_DIM)
    outs_s = (ka_s.reshape(kvs_shape), va_s.reshape(kvs_shape), kb_s.reshape(kvs_shape), vb_s.reshape(kvs_shape),
              kw_s[:, :IDX_DIM].reshape(1, db, ds, IDX_DIM), conv_s[None])
    return (y_p.reshape(bsz, t, d), y_s.reshape(db, ds, d)) + outs_p + outs_s
```

```python
import functools

import jax
import jax.numpy as jnp
from jax import lax
from jax.experimental import pallas as pl
from jax.experimental.pallas import tpu as pltpu

HEAD_DIM = 64
N_HEADS = 8
WIDTH = N_HEADS * HEAD_DIM
MOBA_BLOCK = 256
MOBA_TOPK = 3
DSA_TOPK = 256
IDX_HEADS = 8
IDX_DIM = 64
CONV_WIDTH = 3
EPS = 1e-6
N_MOD = 6
ATTN_SCALE = HEAD_DIM ** -0.5
N_QKV_CHUNKS = 7
LANES = 128
SUBLANES = 8
TQ = MOBA_BLOCK
SB = 1024
BPS = SB // TQ
FF_CHUNK = 256
PAGES_PER_STEP = 8
VMEM_LIMIT = 56 * 1024 * 1024

BF = jnp.bfloat16
F32 = jnp.float32
I32 = jnp.int32
NEG_INF = float("-inf")
INT_MIN = -2 ** 31


def _cp(sem):
    return pltpu.CompilerParams(dimension_semantics=sem, vmem_limit_bytes=VMEM_LIMIT)


def _const_spec(shape):
    nd = len(shape)
    return pl.BlockSpec(shape, lambda *_: (0,) * nd)


def _dot_nt(a, b):
    return lax.dot_general(a, b, (((1,), (1,)), ((), ())), preferred_element_type=F32)


def _mod_kernel(c_ref, w_ref, b_ref, o_ref):
    s = jax.nn.silu(c_ref[...])
    o_ref[...] = jnp.dot(s.astype(BF), w_ref[...].astype(BF), preferred_element_type=F32) + b_ref[...]


def _modulation(c_all, w_ada, b_ada):
    r, d = c_all.shape
    n = w_ada.shape[1]
    tn = 1024
    return pl.pallas_call(
        _mod_kernel,
        grid=(n // tn,),
        in_specs=[pl.BlockSpec((r, d), lambda j: (0, 0)),
                  pl.BlockSpec((d, tn), lambda j: (0, j)),
                  pl.BlockSpec((1, tn), lambda j: (0, j))],
        out_specs=pl.BlockSpec((r, tn), lambda j: (0, j)),
        out_shape=jax.ShapeDtypeStruct((r, n), F32),
        compiler_params=_cp(("arbitrary",)),
        name="modulation",
    )(c_all, w_ada, b_ada.reshape(1, n))


def _pre_kernel(x_ref, shift_ref, scale_ref, g_ref, wqkv_ref, wkw_ref, nrm_ref, gmat_ref, *outs, tm, with_kmean):
    x = x_ref[...]
    ms = jnp.mean(x * x, axis=-1, keepdims=True)
    xn = x * lax.rsqrt(ms + EPS) * g_ref[...]
    h = (xn * (1.0 + scale_ref[0]) + shift_ref[0]).astype(BF)
    norm_row = {0: 0, 1: 1, 3: 2, 4: 3}
    for c in range(N_QKV_CHUNKS):
        p = jnp.dot(h, wqkv_ref[:, c * WIDTH:(c + 1) * WIDTH], preferred_element_type=F32)
        if c in norm_row:
            ss = jnp.dot((p * p).astype(BF), gmat_ref[...], preferred_element_type=F32)
            k = norm_row[c]
            p = p * lax.rsqrt(ss * (1.0 / HEAD_DIM) + EPS) * nrm_ref[k:k + 1, :]
        outs[c][...] = p
        if c == 1 and with_kmean:
            outs[N_QKV_CHUNKS + 1][0] = jnp.mean(p.reshape(tm // MOBA_BLOCK, MOBA_BLOCK, WIDTH), axis=1)
    outs[N_QKV_CHUNKS][...] = jnp.dot(h, wkw_ref[...], preferred_element_type=F32)


def _pre_prompt_kernel(x_ref, shift_ref, scale_ref, g_ref, wqkv_ref, wkw_ref, nrm_ref, gmat_ref,
                       qa_ref, qb_ref, qi_ref, kw_ref, kat_ref, vat_ref, kbt_ref, vbt_ref,
                       katb_ref, vatb_ref, kbtb_ref, vbtb_ref, kwtb_ref, kit_ref, km_ref, *, tm):
    x = x_ref[...]
    ms = jnp.mean(x * x, axis=-1, keepdims=True)
    xn = x * lax.rsqrt(ms + EPS) * g_ref[...]
    h = (xn * (1.0 + scale_ref[0]) + shift_ref[0]).astype(BF)

    def proj(c, norm):
        p = jnp.dot(h, wqkv_ref[:, c * WIDTH:(c + 1) * WIDTH], preferred_element_type=F32)
        if norm is not None:
            ss = jnp.dot((p * p).astype(BF), gmat_ref[...], preferred_element_type=F32)
            p = p * lax.rsqrt(ss * (1.0 / HEAD_DIM) + EPS) * nrm_ref[norm:norm + 1, :]
        return p

    qa_ref[...] = (proj(0, 0) * ATTN_SCALE).astype(BF)
    ka = proj(1, 1)
    km_ref[0] = jnp.mean(ka.reshape(tm // MOBA_BLOCK, MOBA_BLOCK, WIDTH), axis=1)
    for p, t_ref, tb_ref in ((ka, kat_ref, katb_ref), (proj(2, None), vat_ref, vatb_ref)):
        pt = p.T
        t_ref[0] = pt
        tb_ref[0] = pt.astype(BF)
    qb_ref[...] = (proj(3, 2) * ATTN_SCALE).astype(BF)
    for p, t_ref, tb_ref in ((proj(4, 3), kbt_ref, kbtb_ref), (proj(5, None), vbt_ref, vbtb_ref)):
        pt = p.T
        t_ref[0] = pt
        tb_ref[0] = pt.astype(BF)
    qi_ref[...] = proj(6, None)
    kw = jnp.dot(h, wkw_ref[...], preferred_element_type=F32)
    kw_ref[...] = kw
    kwt = kw.T
    kwtb_ref[0] = kwt.astype(BF)
    kit_ref[0] = kwt[:IDX_DIM, :]


def _pre_project_prompt(x2, shift, scale, g_attn, wqkv, wkw, nrm, gmat, *, tm, bsz, t):
    n, d = x2.shape
    tpb = t // tm
    mod_spec = pl.BlockSpec((1, 1, d), lambda i: (i // tpb, 0, 0))
    row_spec = lambda w: pl.BlockSpec((tm, w), lambda i: (i, 0))
    tr_spec = lambda r: pl.BlockSpec((1, r, tm), lambda i: (i // tpb, 0, i % tpb))
    nb = tm // MOBA_BLOCK
    sds = jax.ShapeDtypeStruct
    out_shape = ([sds((n, WIDTH), BF)] * 2 + [sds((n, WIDTH), F32), sds((n, LANES), F32)]
                 + [sds((bsz, WIDTH, t), F32)] * 4 + [sds((bsz, WIDTH, t), BF)] * 4
                 + [sds((bsz, LANES, t), BF), sds((bsz, IDX_DIM, t), F32), sds((n // tm, nb, WIDTH), F32)])
    out_specs = ([row_spec(WIDTH)] * 3 + [row_spec(LANES)] + [tr_spec(WIDTH)] * 8
                 + [tr_spec(LANES), tr_spec(IDX_DIM), pl.BlockSpec((1, nb, WIDTH), lambda i: (i, 0, 0))])
    return pl.pallas_call(
        functools.partial(_pre_prompt_kernel, tm=tm),
        grid=(n // tm,),
        in_specs=[row_spec(d), mod_spec, mod_spec, _const_spec((1, d)),
                  _const_spec(wqkv.shape), _const_spec(wkw.shape), _const_spec(nrm.shape), _const_spec(gmat.shape)],
        out_specs=out_specs,
        out_shape=out_shape,
        compiler_params=_cp(("arbitrary",)),
        name="pre_project_prompt",
    )(x2, shift, scale, g_attn, wqkv, wkw, nrm, gmat)


def _pre_project(x2, shift, scale, g_attn, wqkv, wkw, nrm, gmat, *, tm, rows_per_group, with_kmean):
    n, d = x2.shape
    grid = (n // tm,)
    mr = shift.shape[1]
    tiles_per_group = rows_per_group // tm
    mod_spec = pl.BlockSpec((1, mr, d), lambda i: (i // tiles_per_group, 0, 0))
    row_spec = lambda w: pl.BlockSpec((tm, w), lambda i: (i, 0))
    out_shape = [jax.ShapeDtypeStruct((n, WIDTH), F32)] * N_QKV_CHUNKS + [jax.ShapeDtypeStruct((n, LANES), F32)]
    out_specs = [row_spec(WIDTH)] * N_QKV_CHUNKS + [row_spec(LANES)]
    if with_kmean:
        nb = tm // MOBA_BLOCK
        out_shape.append(jax.ShapeDtypeStruct((n // tm, nb, WIDTH), F32))
        out_specs.append(pl.BlockSpec((1, nb, WIDTH), lambda i: (i, 0, 0)))
    return pl.pallas_call(
        functools.partial(_pre_kernel, tm=tm, with_kmean=with_kmean),
        grid=grid,
        in_specs=[row_spec(d), mod_spec, mod_spec, _const_spec((1, d)),
                  _const_spec(wqkv.shape), _const_spec(wkw.shape), _const_spec(nrm.shape), _const_spec(gmat.shape)],
        out_specs=out_specs,
        out_shape=out_shape,
        compiler_params=_cp(("arbitrary",)),
        name="pre_project",
    )(x2, shift, scale, g_attn, wqkv, wkw, nrm, gmat)


def _pair_half(shape, axis):
    return lax.broadcasted_iota(I32, shape, axis) // HEAD_DIM


def _two_pass_attention(q_pair, par, kt_ref, vt_ref, slope, i, bias_at, l_ref, m_ref, acc_ref):
    n_steps = i // BPS + 1
    q = jnp.where(_pair_half((TQ, LANES), 1) == par, q_pair, jnp.zeros_like(q_pair))
    own_rows = _pair_half((LANES, SB), 0) == par
    m_ref[...] = jnp.full((TQ, TQ), NEG_INF, F32)

    def keys(ref, j):
        return ref[0, :, pl.ds(pl.multiple_of(j * SB, SB), SB)]

    def pass1(j, carry):
        cols = lax.broadcasted_iota(I32, (1, SB), 1) + (j * SB - i * TQ)
        l = jnp.dot(q, keys(kt_ref, j), preferred_element_type=F32) + slope * cols.astype(F32) + bias_at(j)
        l_ref[j] = l
        mx = m_ref[...]
        for k in range(BPS):
            mx = jnp.maximum(mx, l[:, k * TQ:(k + 1) * TQ])
        m_ref[...] = mx
        return carry

    lax.fori_loop(0, n_steps, pass1, 0)
    m = jnp.max(m_ref[...], axis=-1, keepdims=True)
    acc_ref[...] = jnp.zeros((TQ, LANES), F32)

    def pass2(j, carry):
        p = jnp.exp((l_ref[j] - m).astype(BF))
        vt = jnp.where(own_rows, keys(vt_ref, j), jnp.ones((LANES, SB), BF))
        acc_ref[...] += _dot_nt(p, vt)
        return carry

    lax.fori_loop(0, n_steps, pass2, 0)
    acc = acc_ref[...]
    return acc / pltpu.roll(acc, HEAD_DIM, axis=1)


def _store_pair_half(o_ref, par, val):
    keep = _pair_half((TQ, LANES), 1) == par

    @pl.when(par == 0)
    def _():
        o_ref[...] = jnp.where(keep, val, 0.0)

    @pl.when(par != 0)
    def _():
        o_ref[...] = jnp.where(keep, val, o_ref[...])


def _moba_prompt_kernel(slopes_ref, q_ref, kt_ref, vt_ref, kmt_ref, o_ref, l_ref, m_ref, acc_ref, bias_ref, sel_ref):
    pr = pl.program_id(1)
    i = pl.program_id(2)
    par = pl.program_id(3)
    sub = i % BPS
    row0 = pl.multiple_of(sub * TQ, TQ)

    @pl.when(sub == 0)
    def _():
        qs = q_ref[...]
        qs = jnp.where(_pair_half(qs.shape, 1) == par, qs, jnp.zeros_like(qs))
        s = jnp.dot(qs, kmt_ref[0], preferred_element_type=F32)
        lane_s = lax.broadcasted_iota(I32, s.shape, 1)
        own = i + lax.broadcasted_iota(I32, s.shape, 0) // TQ
        tri = (lax.broadcasted_iota(I32, (LANES, LANES), 0) <= lax.broadcasted_iota(I32, (LANES, LANES), 1))
        tri = jnp.where(tri, 1.0, 0.0).astype(BF)
        s = jnp.where(lane_s < own, s, NEG_INF)
        bias = jnp.full(s.shape, NEG_INF, F32)
        for _ in range(MOBA_TOPK):
            mx = jnp.max(s, axis=-1, keepdims=True)
            eq = s == mx
            rank = jnp.dot(jnp.where(eq, 1.0, 0.0).astype(BF), tri, preferred_element_type=F32)
            pick = eq & (rank == 1.0) & (mx > NEG_INF)
            bias = jnp.where(pick, 0.0, bias)
            s = jnp.where(pick, NEG_INF, s)
        sel_ref[par] = bias

    q = q_ref[pl.ds(row0, TQ), :]
    bias_ref[...] = sel_ref[par, pl.ds(row0, TQ), :]
    lane = lax.broadcasted_iota(I32, (TQ, LANES), 1)
    r = lax.broadcasted_iota(I32, (TQ, TQ), 0)
    c = lax.broadcasted_iota(I32, (TQ, TQ), 1)
    causal = jnp.where(c <= r, 0.0, NEG_INF).astype(F32)

    def bias_at(j):
        parts = []
        for k in range(BPS):
            blk = j * BPS + k
            col = jnp.max(jnp.where(lane == blk, bias_ref[...], NEG_INF), axis=-1, keepdims=True)
            parts.append(jnp.where(blk == i, causal, jnp.broadcast_to(col, (TQ, TQ))))
        return jnp.concatenate(parts, axis=1)

    out = _two_pass_attention(q, par, kt_ref, vt_ref, slopes_ref[2 * pr + par], i, bias_at, l_ref, m_ref, acc_ref)
    _store_pair_half(o_ref, par, out)


def _moba_prompt(slopes, q, kt, vt, kmt, bsz, t):
    nq = t // TQ
    nsb = t // SB
    npair = N_HEADS // 2
    grid_spec = pltpu.PrefetchScalarGridSpec(
        num_scalar_prefetch=1,
        grid=(bsz, npair, nq, 2),
        in_specs=[pl.BlockSpec((SB, LANES), lambda bi, pr, i, par, s: (bi * nsb + i // BPS, pr)),
                  pl.BlockSpec((1, LANES, t), lambda bi, pr, i, par, s: (bi, pr, 0)),
                  pl.BlockSpec((1, LANES, t), lambda bi, pr, i, par, s: (bi, pr, 0)),
                  pl.BlockSpec((1, LANES, LANES), lambda bi, pr, i, par, s: (bi, pr, 0))],
        out_specs=pl.BlockSpec((TQ, LANES), lambda bi, pr, i, par, s: (bi * nq + i, pr)),
        scratch_shapes=[pltpu.VMEM((nsb, TQ, SB), F32), pltpu.VMEM((TQ, TQ), F32),
                        pltpu.VMEM((TQ, LANES), F32), pltpu.VMEM((TQ, LANES), F32), pltpu.VMEM((2, SB, LANES), F32)],
    )
    return pl.pallas_call(
        _moba_prompt_kernel,
        grid_spec=grid_spec,
        out_shape=jax.ShapeDtypeStruct((bsz * t, WIDTH), F32),
        compiler_params=_cp(("arbitrary",) * 4),
        name="moba_prompt",
    )(slopes, q, kt, vt, kmt)


def _threshold_from_code(code):
    bits = code ^ ((code >> 31) & 0x7FFFFFFF)
    return lax.bitcast_convert_type(bits, F32)


def _dsa_select(i, n_sel, idx_bits, sk_ref, b_ref, jc_ref):
    n_steps = i // BPS + 1
    rc = TQ // 2
    c = lax.broadcasted_iota(I32, (rc, SB), 1)
    for r0 in range(0, TQ, rc):
        rows = slice(r0, r0 + rc)
        gr = lax.broadcasted_iota(I32, (rc, SB), 0) + (i * TQ + r0)

        def count(pred, rows=rows):
            def body(j, acc):
                hit = jnp.where(pred(j, sk_ref[j, rows, :]), 1.0, 0.0)
                for k in range(SB // LANES):
                    acc = acc + hit[:, k * LANES:(k + 1) * LANES]
                return acc

            acc = lax.fori_loop(0, n_steps, body, jnp.zeros((rc, LANES), F32))
            return jnp.sum(acc, axis=-1, keepdims=True)

        def bit_step(it, v, count=count):
            cand = v + jnp.left_shift(jnp.int32(1), 31 - it)
            cand_f = _threshold_from_code(cand)
            cnt = count(lambda j, sc: sc >= cand_f)
            return jnp.where(cnt >= n_sel, cand, v)

        v = lax.fori_loop(0, 32, bit_step, jnp.full((rc, 1), INT_MIN, I32))
        few = v == INT_MIN
        thr = jnp.where(few, NEG_INF, _threshold_from_code(v))
        cnt_gt = count(lambda j, sc: sc > thr)
        cnt_ge = count(lambda j, sc: sc >= thr)
        need = n_sel - cnt_gt
        trim = (cnt_ge > n_sel) & jnp.logical_not(few)
        jc_ref[rows, :] = jnp.full((rc, LANES), 2 ** idx_bits, I32)

        @pl.when(jnp.max(jnp.where(trim, 1.0, 0.0)) > 0.0)
        def _(rows=rows, count=count, thr=thr, need=need, trim=trim):
            def idx_step(it, jcap):
                cand = jcap + jnp.left_shift(jnp.int32(1), idx_bits - 1 - it)
                cnt = count(lambda j, sc: (sc == thr) & (c + j * SB < cand))
                return jnp.where(cnt < need, cand, jcap)

            jcap = lax.fori_loop(0, idx_bits, idx_step, jnp.zeros((rc, 1), I32))
            jc_ref[rows, :] = jnp.broadcast_to(jnp.where(trim, jcap, 2 ** idx_bits), (rc, LANES))

        jcap = jc_ref[rows, 0:1]

        def write(j, carry, rows=rows, thr=thr, jcap=jcap, gr=gr):
            sc = sk_ref[j, rows, :]
            gc = c + j * SB
            sel = ((sc > thr) | ((sc == thr) & (gc <= jcap))) & (gc <= gr)
            b_ref[j, rows, :] = jnp.where(sel, 0.0, NEG_INF)
            return carry

        lax.fori_loop(0, n_steps, write, 0)


def _dsa_prompt_kernel(slopes_ref, qi_ref, kw_ref, kwt_ref, q_ref, kt_ref, vt_ref, o_ref,
                       sk_ref, b_ref, l_ref, m_ref, acc_ref, jc_ref, qz_ref, *, n_sel, idx_bits):
    i = pl.program_id(1)
    h = pl.program_id(2)
    par = h % 2

    @pl.when(h == 0)
    def _():
        wi = kw_ref[...] * (IDX_HEADS * IDX_DIM) ** -0.5
        wcols = [wi[:, IDX_DIM + hh:IDX_DIM + hh + 1] for hh in range(IDX_HEADS)]
        gr = lax.broadcasted_iota(I32, (TQ, SB), 0) + i * TQ
        c = lax.broadcasted_iota(I32, (TQ, SB), 1)
        first = _pair_half((TQ, LANES), 1) == 0
        for hh in range(IDX_HEADS):
            qp = qi_ref[:, (hh // 2) * LANES:(hh // 2 + 1) * LANES]
            if hh % 2:
                qp = pltpu.roll(qp, HEAD_DIM, axis=1)
            qz_ref[hh] = jnp.where(first, qp, 0.0).astype(BF)

        def score_step(j, carry):
            kit = kwt_ref[0, :, pl.ds(pl.multiple_of(j * SB, SB), SB)]
            sc = jnp.maximum(jnp.dot(qz_ref[0], kit, preferred_element_type=F32), 0.0) * wcols[0]
            for hh in range(1, IDX_HEADS):
                sc = sc + jnp.maximum(jnp.dot(qz_ref[hh], kit, preferred_element_type=F32), 0.0) * wcols[hh]
            sk_ref[j] = jnp.where(c + j * SB <= gr, sc, NEG_INF)
            return carry

        lax.fori_loop(0, i // BPS + 1, score_step, 0)
        _dsa_select(i, n_sel, idx_bits, sk_ref, b_ref, jc_ref)

    out = _two_pass_attention(q_ref[...], par, kt_ref, vt_ref, slopes_ref[h], i, lambda j: b_ref[j],
                              l_ref, m_ref, acc_ref)
    _store_pair_half(o_ref, par, out)


def _dsa_prompt(slopes, qi, kw, kwt, q, kt, vt, n_sel, bsz, t):
    nq = t // TQ
    nsb = t // SB
    idx_bits = max(1, (t - 1).bit_length())
    blk = lambda shape, fn: pl.BlockSpec(shape, fn)
    grid_spec = pltpu.PrefetchScalarGridSpec(
        num_scalar_prefetch=1,
        grid=(bsz, nq, N_HEADS),
        in_specs=[blk((TQ, WIDTH), lambda bi, i, hi, s: (bi * nq + i, 0)),
                  blk((TQ, LANES), lambda bi, i, hi, s: (bi * nq + i, 0)),
                  blk((1, LANES, t), lambda bi, i, hi, s: (bi, 0, 0)),
                  blk((TQ, LANES), lambda bi, i, hi, s: (bi * nq + i, hi // 2)),
                  blk((1, LANES, t), lambda bi, i, hi, s: (bi, hi // 2, 0)),
                  blk((1, LANES, t), lambda bi, i, hi, s: (bi, hi // 2, 0))],
        out_specs=blk((TQ, LANES), lambda bi, i, hi, s: (bi * nq + i, hi // 2)),
        scratch_shapes=[pltpu.VMEM((nsb, TQ, SB), F32), pltpu.VMEM((nsb, TQ, SB), F32), pltpu.VMEM((nsb, TQ, SB), F32),
                        pltpu.VMEM((TQ, TQ), F32), pltpu.VMEM((TQ, LANES), F32), pltpu.VMEM((TQ, LANES), I32),
                        pltpu.VMEM((IDX_HEADS, TQ, LANES), BF)],
    )
    return pl.pallas_call(
        functools.partial(_dsa_prompt_kernel, n_sel=n_sel, idx_bits=idx_bits),
        grid_spec=grid_spec,
        out_shape=jax.ShapeDtypeStruct((bsz * t, WIDTH), F32),
        compiler_params=_cp(("arbitrary", "arbitrary", "arbitrary")),
        name="dsa_prompt",
    )(slopes, qi, kw, kwt, q, kt, vt)


def _finish_kernel(*refs, tm, nchunk, seq_mode, tiles_per_seq):
    (x_ref, oa_ref, ob_ref, gate_a_ref, shift_f_ref, scale_f_ref, gate_f_ref, g_ref, wout_ref,
     wupa_ref, wupg_ref, cw_ref, cb_ref, wdn_ref) = refs[:14]
    if seq_mode:
        y_ref, st_ref, abuf_ref, carry_ref, yacc_ref = refs[14:]
    else:
        pre0_ref, pre1_ref, y_ref, st_ref, yacc_ref = refs[14:]
    i = pl.program_id(0)
    o = (jnp.dot(oa_ref[...].astype(BF), wout_ref[0:WIDTH, :], preferred_element_type=F32)
         + jnp.dot(ob_ref[...].astype(BF), wout_ref[WIDTH:2 * WIDTH, :], preferred_element_type=F32))
    x1 = x_ref[...] + gate_a_ref[0] * o
    ms = jnp.mean(x1 * x1, axis=-1, keepdims=True)
    h = ((x1 * lax.rsqrt(ms + EPS) * g_ref[...]) * (1.0 + scale_f_ref[0]) + shift_f_ref[0]).astype(BF)
    yacc_ref[...] = jnp.zeros_like(yacc_ref)
    for c in range(nchunk):
        cols = slice(c * FF_CHUNK, (c + 1) * FF_CHUNK)
        a = jnp.dot(h, wupa_ref[c], preferred_element_type=F32)
        gt = jnp.dot(h, wupg_ref[c], preferred_element_type=F32)
        if seq_mode:
            first = (i % tiles_per_seq) == 0
            buf = abuf_ref.at[c % 2]
            buf[0:SUBLANES, :] = jnp.where(first, 0.0, carry_ref[c])
            buf[SUBLANES:SUBLANES + tm, :] = a
            a1 = buf[SUBLANES - 1:SUBLANES - 1 + tm, :]
            a2 = buf[SUBLANES - 2:SUBLANES - 2 + tm, :]
            carry_ref[c] = a[tm - SUBLANES:tm, :]
            st_ref[0, :, cols] = a[tm - (CONV_WIDTH - 1):tm, :]
        else:
            a2 = pre0_ref[:, cols]
            a1 = pre1_ref[:, cols]
            st_ref[:, cols] = a
        conv = cb_ref[:, cols] + a2 * cw_ref[0:1, cols]
        conv = conv + a1 * cw_ref[1:2, cols]
        conv = conv + a * cw_ref[2:3, cols]
        act = (jax.nn.gelu(conv) * gt).astype(BF)
        yacc_ref[...] += jnp.dot(act, wdn_ref[c], preferred_element_type=F32)
    y_ref[...] = x1 + gate_f_ref[0] * yacc_ref[...]


def _finish(x2, oa, ob, mods, g_ffn, wout, wupa, wupg, conv_w, conv_b, wdn, *, tm, rows_per_group, prefix=None):
    n, d = x2.shape
    nchunk = wupa.shape[0]
    f = nchunk * FF_CHUNK
    seq_mode = prefix is None
    tiles_per_seq = rows_per_group // tm
    mr = mods[0].shape[1]
    mod_spec = pl.BlockSpec((1, mr, d), lambda i: (i // tiles_per_seq, 0, 0))
    row_spec = lambda w: pl.BlockSpec((tm, w), lambda i: (i, 0))
    in_specs = ([row_spec(d), row_spec(WIDTH), row_spec(WIDTH)] + [mod_spec] * 4 +
                [_const_spec((1, d)), _const_spec(wout.shape), _const_spec(wupa.shape), _const_spec(wupg.shape),
                 _const_spec(conv_w.shape), _const_spec((1, f)), _const_spec(wdn.shape)])
    args = [x2, oa, ob, *mods, g_ffn, wout, wupa, wupg, conv_w, conv_b, wdn]
    scratch = []
    if seq_mode:
        nseq = n // rows_per_group
        st_shape = jax.ShapeDtypeStruct((nseq, CONV_WIDTH - 1, f), F32)
        st_spec = pl.BlockSpec((1, CONV_WIDTH - 1, f), lambda i: (i // tiles_per_seq, 0, 0))
        scratch += [pltpu.VMEM((2, SUBLANES + tm, FF_CHUNK), F32), pltpu.VMEM((nchunk, SUBLANES, FF_CHUNK), F32)]
    else:
        in_specs += [row_spec(f), row_spec(f)]
        args += list(prefix)
        st_shape = jax.ShapeDtypeStruct((n, f), F32)
        st_spec = row_spec(f)
    scratch.append(pltpu.VMEM((tm, d), F32))
    return pl.pallas_call(
        functools.partial(_finish_kernel, tm=tm, nchunk=nchunk, seq_mode=seq_mode, tiles_per_seq=tiles_per_seq),
        grid=(n // tm,),
        in_specs=in_specs,
        out_specs=[row_spec(d), st_spec],
        out_shape=[jax.ShapeDtypeStruct((n, d), F32), st_shape],
        scratch_shapes=scratch,
        compiler_params=_cp(("arbitrary",)),
        name="finish",
    )(*args)


def _block_diag_rows(row):
    hrow = lax.broadcasted_iota(I32, (N_HEADS, WIDTH), 0)
    hlane = lax.broadcasted_iota(I32, (N_HEADS, WIDTH), 1) // HEAD_DIM
    return jnp.where(hrow == hlane, jnp.broadcast_to(row, (N_HEADS, WIDTH)), 0.0)


def _sample_stats_kernel(pt_ref, qi_ref, wi_ref, kin_ref, qa_ref, *refs, page, nblk):
    pps = PAGES_PER_STEP
    kidx_refs = refs[:pps]
    kmoba_refs = refs[pps:2 * pps]
    sc_ref, scn_ref, bb_ref, km_ref = refs[2 * pps:]
    s = pl.program_id(1)
    qi = qi_ref[0].astype(BF)
    wi = wi_ref[0] * (IDX_HEADS * IDX_DIM) ** -0.5
    ppb = MOBA_BLOCK // page
    lane = lax.broadcasted_iota(I32, (WIDTH, LANES), 1)

    @pl.when(s == 0)
    def _():
        km_ref[...] = jnp.zeros(km_ref.shape, F32)

    kidx = jnp.concatenate([kidx_refs[k][0, 0].astype(BF) for k in range(pps)], axis=1)
    rel = jnp.maximum(jnp.dot(qi, kidx, preferred_element_type=F32), 0.0)
    sc_ref[0] = jnp.sum(rel * wi, axis=0, keepdims=True)
    for kb in range(pps // ppb):
        tot = kmoba_refs[kb * ppb][0, 0].reshape(WIDTH, page)
        for k in range(1, ppb):
            tot = tot + kmoba_refs[kb * ppb + k][0, 0].reshape(WIDTH, page)
        mean = jnp.sum(tot, axis=-1, keepdims=True) * (1.0 / MOBA_BLOCK)
        km_ref[...] = jnp.where(lane == s * (pps // ppb) + kb, mean, km_ref[...])

    @pl.when(s == pl.num_programs(1) - 1)
    def _():
        kin = jnp.broadcast_to(kin_ref[0], (SUBLANES, IDX_DIM)).astype(BF)
        reln = jnp.maximum(_dot_nt(qi, kin), 0.0)
        scn = jnp.sum(reln[:, 0:1] * wi, axis=0, keepdims=True)
        scn_ref[0] = jnp.broadcast_to(scn, (1, LANES))
        qbd = _block_diag_rows(qa_ref[0]).astype(BF)
        sb = jnp.dot(qbd, km_ref[...].astype(BF), preferred_element_type=F32)
        lane_b = lax.broadcasted_iota(I32, sb.shape, 1)
        lanef = lane_b.astype(F32)
        sb = jnp.where(lane_b < nblk, sb, NEG_INF)
        bias = jnp.full(sb.shape, NEG_INF, F32)
        for _ in range(MOBA_TOPK):
            mx = jnp.max(sb, axis=-1, keepdims=True)
            first = jnp.min(jnp.where(sb == mx, lanef, float(LANES)), axis=-1, keepdims=True)
            pick = (lanef == first) & (mx > NEG_INF)
            bias = jnp.where(pick, 0.0, bias)
            sb = jnp.where(pick, NEG_INF, sb)
        bb_ref[0] = bias


def _sample_stats(page_table, qi_s, wi_s, ki_s, qa_s, pool_kidx, pool_mk):
    db, n_pages = page_table.shape
    page = pool_kidx.shape[3]
    pps = PAGES_PER_STEP
    past = n_pages * page
    nblk = past // MOBA_BLOCK
    assert n_pages % pps == 0 and pps % (MOBA_BLOCK // page) == 0 and nblk <= LANES and page == LANES
    kidx_spec = lambda k: pl.BlockSpec((1, 1, IDX_DIM, page), lambda b, s, pt: (0, pt[b, s * pps + k], 0, 0))
    kmoba_spec = lambda k: pl.BlockSpec((1, 1, N_HEADS, HEAD_DIM, page),
                                        lambda b, s, pt: (0, pt[b, s * pps + k], 0, 0, 0))
    per_seq = lambda shape: pl.BlockSpec((1,) + shape, lambda b, s, pt: (b, 0, 0))
    grid_spec = pltpu.PrefetchScalarGridSpec(
        num_scalar_prefetch=1,
        grid=(db, n_pages // pps),
        in_specs=([per_seq((IDX_HEADS, IDX_DIM)), per_seq((IDX_HEADS, 1)), per_seq((1, IDX_DIM)), per_seq((1, WIDTH))]
                  + [kidx_spec(k) for k in range(pps)] + [kmoba_spec(k) for k in range(pps)]),
        out_specs=[pl.BlockSpec((1, 1, pps * page), lambda b, s, pt: (b, 0, s)),
                   per_seq((1, LANES)), per_seq((N_HEADS, LANES))],
        scratch_shapes=[pltpu.VMEM((WIDTH, LANES), F32)],
    )
    return pl.pallas_call(
        functools.partial(_sample_stats_kernel, page=page, nblk=nblk),
        grid_spec=grid_spec,
        out_shape=[jax.ShapeDtypeStruct((db, 1, past), F32), jax.ShapeDtypeStruct((db, 1, LANES), F32),
                   jax.ShapeDtypeStruct((db, N_HEADS, LANES), F32)],
        compiler_params=_cp(("arbitrary", "arbitrary")),
        name="sample_stats",
    )(page_table, qi_s, wi_s, ki_s, qa_s, *([pool_kidx] * pps), *([pool_mk] * pps))


def _dsa_sample_select_kernel(sc_ref, scn_ref, b_ref, bn_ref, *, n_sel, idx_bits):
    sc = sc_ref[...]
    scn = scn_ref[:, 0:1]
    db, past = sc.shape
    gidx = lax.broadcasted_iota(I32, sc.shape, 1)

    def count(pred, pred_new):
        return (jnp.sum(jnp.where(pred, 1.0, 0.0), axis=-1, keepdims=True) + jnp.where(pred_new, 1.0, 0.0))

    def bit_step(it, v):
        cand = v + jnp.left_shift(jnp.int32(1), 31 - it)
        cand_f = _threshold_from_code(cand)
        cnt = count(sc >= cand_f, scn >= cand_f)
        return jnp.where(cnt >= n_sel, cand, v)

    v = lax.fori_loop(0, 32, bit_step, jnp.full((db, 1), INT_MIN, I32))
    few = v == INT_MIN
    thr = jnp.where(few, NEG_INF, _threshold_from_code(v))
    cnt_gt = count(sc > thr, scn > thr)
    cnt_ge = count(sc >= thr, scn >= thr)
    need = n_sel - cnt_gt
    trim = (cnt_ge > n_sel) & jnp.logical_not(few)

    def idx_step(it, jcap):
        cand = jcap + jnp.left_shift(jnp.int32(1), idx_bits - 1 - it)
        cnt = count((sc == thr) & (gidx < cand), (scn == thr) & (past < cand))
        return jnp.where(cnt < need, cand, jcap)

    jcap = lax.fori_loop(0, idx_bits, idx_step, jnp.zeros((db, 1), I32))
    jcap = jnp.where(trim, jcap, 2 ** idx_bits)
    sel = (sc > thr) | ((sc == thr) & (gidx <= jcap))
    seln = (scn > thr) | ((scn == thr) & (past <= jcap))
    bn_ref[...] = jnp.broadcast_to(jnp.where(seln, 0.0, NEG_INF), bn_ref.shape)
    b_ref[...] = jnp.where(sel, 0.0, NEG_INF)


def _dsa_sample_select(scores, score_new, n_sel):
    db, past = scores.shape
    idx_bits = past.bit_length()
    return pl.pallas_call(
        functools.partial(_dsa_sample_select_kernel, n_sel=n_sel, idx_bits=idx_bits),
        out_shape=[jax.ShapeDtypeStruct((db, past), F32), jax.ShapeDtypeStruct((db, LANES), F32)],
        compiler_params=pltpu.CompilerParams(vmem_limit_bytes=VMEM_LIMIT),
        name="dsa_sample_select",
    )(scores, score_new)


def _paged_attention_kernel(pt_ref, slopes_ref, q_ref, kn_ref, vn_ref, bias_ref, bn_ref, *refs, page, block_bias):
    pps = PAGES_PER_STEP
    k_refs = refs[:pps]
    v_refs = refs[pps:2 * pps]
    o_ref, l_ref, mx_ref, m_ref, ps_ref, acc_ref, sl_ref, qbd_ref = refs[2 * pps:]
    s = pl.program_id(1)
    ns = pl.num_programs(1) // 2
    past = ns * pps * page
    lane = lax.broadcasted_iota(I32, (N_HEADS, LANES), 1)
    hrow = lax.broadcasted_iota(I32, (N_HEADS, LANES), 0)
    pages = lambda prefs: jnp.concatenate([r[0, 0].reshape(WIDTH, page).astype(BF) for r in prefs], axis=1)

    @pl.when(s == 0)
    def _():
        sl = jnp.zeros((N_HEADS, LANES), F32)
        for h in range(N_HEADS):
            sl = jnp.where(hrow == h, slopes_ref[h], sl)
        sl_ref[...] = sl
        mx_ref[...] = jnp.full(mx_ref.shape, NEG_INF, F32)
        qbd_ref[...] = _block_diag_rows(q_ref[0] * ATTN_SCALE).astype(BF)

    @pl.when(s < ns)
    def _():
        wide = pps * page
        pos = s * wide + lax.broadcasted_iota(I32, (1, wide), 1)
        logits = jnp.dot(qbd_ref[...], pages(k_refs), preferred_element_type=F32)
        logits = logits - sl_ref[:, 0:1] * (past - pos).astype(F32)
        if block_bias:
            cols = []
            for k in range(pps):
                blk = (s * pps + k) // (MOBA_BLOCK // page)
                col = jnp.max(jnp.where(lane == blk, bias_ref[0], NEG_INF), axis=-1, keepdims=True)
                cols.append(jnp.broadcast_to(col, (N_HEADS, page)))
            logits = logits + jnp.concatenate(cols, axis=1)
        else:
            logits = logits + bias_ref[0]
        l_ref[s] = logits
        mx_ref[...] = jnp.maximum(mx_ref[...], logits)

    @pl.when(s == ns - 1)
    def _():
        kn = kn_ref[0].astype(BF).astype(F32)
        ln = jnp.sum(qbd_ref[...].astype(F32) * kn, axis=-1, keepdims=True) + bn_ref[0, :, 0:1]
        m = jnp.maximum(jnp.max(mx_ref[...], axis=-1, keepdims=True), ln)
        m_ref[...] = jnp.broadcast_to(m, m_ref.shape)
        pn = jnp.exp(ln - m)
        ps_ref[...] = jnp.zeros(ps_ref.shape, F32)
        ps_ref[:, 0:1] = pn
        acc_ref[...] = pn.astype(BF).astype(F32) * vn_ref[0].astype(BF).astype(F32)

    @pl.when(s >= ns)
    def _():
        p = jnp.exp(l_ref[s - ns] - m_ref[:, 0:1])
        ps_ref[...] += p
        acc_ref[...] += _dot_nt(p.astype(BF), pages(v_refs))

    @pl.when(s == 2 * ns - 1)
    def _():
        o = acc_ref[...] / jnp.sum(ps_ref[...], axis=-1, keepdims=True)
        hr = lax.broadcasted_iota(I32, (N_HEADS, WIDTH), 0)
        hl = lax.broadcasted_iota(I32, (N_HEADS, WIDTH), 1) // HEAD_DIM
        o_ref[0] = jnp.sum(jnp.where(hr == hl, o, 0.0), axis=0, keepdims=True)


def _paged_attention(page_table, slopes, q_s, k_new, v_new, bias, bias_new, pool_k, pool_v, *, block_bias):
    db, n_pages = page_table.shape
    page = pool_k.shape[4]
    pps = PAGES_PER_STEP
    assert n_pages % pps == 0 and page == LANES
    ns = n_pages // pps
    pool_blk = (1, 1, N_HEADS, HEAD_DIM, page)
    k_spec = lambda k: pl.BlockSpec(pool_blk, lambda b, s, pt, sl: (0, pt[b, jnp.minimum(s, ns - 1) * pps + k], 0, 0, 0))
    v_spec = lambda k: pl.BlockSpec(pool_blk, lambda b, s, pt, sl: (0, pt[b, jnp.maximum(s - ns, 0) * pps + k], 0, 0, 0))
    per_seq = lambda shape: pl.BlockSpec((1,) + shape, lambda b, s, pt, sl: (b, 0, 0))
    if block_bias:
        bias_spec = per_seq((N_HEADS, LANES))
    else:
        bias_spec = pl.BlockSpec((1, 1, pps * page), lambda b, s, pt, sl: (b, 0, jnp.minimum(s, ns - 1)))
    grid_spec = pltpu.PrefetchScalarGridSpec(
        num_scalar_prefetch=2,
        grid=(db, 2 * ns),
        in_specs=([per_seq((1, WIDTH))] * 3 + [bias_spec, per_seq((1, LANES))]
                  + [k_spec(k) for k in range(pps)] + [v_spec(k) for k in range(pps)]),
        out_specs=per_seq((1, WIDTH)),
        scratch_shapes=[pltpu.VMEM((ns, N_HEADS, pps * page), F32), pltpu.VMEM((N_HEADS, pps * page), F32),
                        pltpu.VMEM((N_HEADS, LANES), F32), pltpu.VMEM((N_HEADS, pps * page), F32),
                        pltpu.VMEM((N_HEADS, WIDTH), F32), pltpu.VMEM((N_HEADS, LANES), F32),
                        pltpu.VMEM((N_HEADS, WIDTH), BF)],
    )
    return pl.pallas_call(
        functools.partial(_paged_attention_kernel, page=page, block_bias=block_bias),
        grid_spec=grid_spec,
        out_shape=jax.ShapeDtypeStruct((db, 1, WIDTH), F32),
        compiler_params=_cp(("arbitrary", "arbitrary")),
        name="paged_attention",
    )(page_table, slopes, q_s, k_new, v_new, bias, bias_new, *([pool_k] * pps), *([pool_v] * pps))


def _alibi_slopes():
    n = 2 * N_HEADS
    i = jnp.arange(1, n + 1, dtype=F32)
    m = jnp.exp2(-8.0 * i / n)
    return m[0::2], m[1::2]


def kernel(x_prompt, x_sample, cache_moba_k, cache_moba_v, cache_dsa_k, cache_dsa_v, cache_dsa_kidx, state_ffn_conv, page_table, c_prompt, c_sample, w_ada, b_ada, g_attn, w_in, qn_a, kn_a, qn_b, kn_b, w_out, g_ffn, w_up, conv_w, conv_b, w_down):
    bsz, t, d = x_prompt.shape
    db, ds, _ = x_sample.shape
    depth = w_in.shape[0]
    assert depth == 1 and ds == 1 and t % SB == 0
    f = w_down.shape[1]
    nchunk = f // FF_CHUNK
    assert nchunk * FF_CHUNK == f
    slopes_a, slopes_b = _alibi_slopes()
    l = 0

    ncols = N_QKV_CHUNKS * WIDTH
    wqkv = w_in[l][:, :ncols].astype(BF)
    wkw = jnp.pad(w_in[l][:, ncols:], ((0, 0), (0, LANES - (w_in.shape[2] - ncols)))).astype(BF)
    nrm = jnp.stack([jnp.tile(g[l], N_HEADS) for g in (qn_a, kn_a, qn_b, kn_b)])
    hid = jnp.arange(WIDTH) // HEAD_DIM
    gmat = (hid[:, None] == hid[None, :]).astype(BF)
    wout = w_out[l].astype(BF)
    wupa = w_up[l][:, :f].astype(BF).reshape(d, nchunk, FF_CHUNK).transpose(1, 0, 2)
    wupg = w_up[l][:, f:].astype(BF).reshape(d, nchunk, FF_CHUNK).transpose(1, 0, 2)
    wdn = w_down[l].astype(BF).reshape(nchunk, FF_CHUNK, d)
    g_attn_l = g_attn[l].reshape(1, d)
    g_ffn_l = g_ffn[l].reshape(1, d)
    cb = conv_b[l].reshape(1, f)

    rc = bsz + db
    rpad = -rc % SUBLANES
    c_all = jnp.concatenate([c_prompt, c_sample, jnp.zeros((rpad, d), F32)], axis=0)
    mod = _modulation(c_all, w_ada[l], b_ada[l])
    mods_p = [m.reshape(bsz, 1, d) for m in jnp.split(mod[:bsz], N_MOD, axis=-1)]
    mods_s = [m.reshape(1, db, d) for m in jnp.split(mod[bsz:rc], N_MOD, axis=-1)]

    n = bsz * t
    xp2 = x_prompt.reshape(n, d)
    tm = 512
    (qa_bf, qb_bf, qi, kw, kat, vat, kbt, vbt, kat_bf, vat_bf, kbt_bf, vbt_bf, kwt_bf, kit, kmean) = _pre_project_prompt(
        xp2, mods_p[0], mods_p[1], g_attn_l, wqkv, wkw, nrm, gmat, tm=tm, bsz=bsz, t=t)
    nb = t // TQ
    assert nb <= LANES
    kmt = kmean.reshape(bsz, nb, WIDTH).transpose(0, 2, 1).astype(BF)
    kmt = jnp.pad(kmt, ((0, 0), (0, 0), (0, LANES - nb)))
    o_a = _moba_prompt(slopes_a, qa_bf, kat_bf, vat_bf, kmt, bsz, t)
    n_sel = min(DSA_TOPK, t // 4)
    o_b = _dsa_prompt(slopes_b, qi, kw, kwt_bf, qb_bf, kbt_bf, vbt_bf, n_sel, bsz, t)
    y_p, conv_p = _finish(xp2, o_a, o_b, mods_p[2:], g_ffn_l, wout, wupa, wupg, conv_w[l], cb, wdn,
                          tm=tm, rows_per_group=t)

    kv_out = lambda a: a.reshape(1, bsz, N_HEADS, HEAD_DIM, t).transpose(0, 1, 4, 2, 3)
    outs_p = (kv_out(kat), kv_out(vat), kv_out(kbt), kv_out(vbt),
              kit.reshape(1, bsz, IDX_DIM, t).transpose(0, 1, 3, 2), conv_p[None])

    page = cache_moba_k.shape[2]
    n_pages = page_table.shape[1]
    past = n_pages * page
    assert past % MOBA_BLOCK == 0 and db % SUBLANES == 0 and cache_moba_k.shape[0] == 1
    xs2 = x_sample.reshape(db, d)
    qa_s, ka_s, va_s, qb_s, kb_s, vb_s, qi_s, kw_s = _pre_project(
        xs2, mods_s[0], mods_s[1], g_attn_l, wqkv, wkw, nrm, gmat, tm=db, rows_per_group=db, with_kmean=False)
    pool = lambda cch: cch.transpose(0, 1, 3, 4, 2)
    pool_kidx = cache_dsa_kidx.transpose(0, 1, 3, 2)
    row3 = lambda a: a.reshape(db, 1, a.shape[-1])
    scores, score_new, bias_blk = _sample_stats(
        page_table, qi_s.reshape(db, IDX_HEADS, IDX_DIM), kw_s[:, IDX_DIM:IDX_DIM + IDX_HEADS].reshape(db, IDX_HEADS, 1),
        row3(kw_s[:, :IDX_DIM]), row3(qa_s), pool_kidx, pool(cache_moba_k))
    n_sel_s = min(DSA_TOPK, (past + ds) // 4)
    bias_pos, bias_new = _dsa_sample_select(scores.reshape(db, past), score_new.reshape(db, LANES), n_sel_s)
    o_a_s = _paged_attention(page_table, slopes_a, row3(qa_s), row3(ka_s), row3(va_s), bias_blk,
                             jnp.zeros((db, 1, LANES), F32), pool(cache_moba_k), pool(cache_moba_v), block_bias=True)
    o_b_s = _paged_attention(page_table, slopes_b, row3(qb_s), row3(kb_s), row3(vb_s),
                             bias_pos.reshape(db, 1, past), bias_new.reshape(db, 1, LANES),
                             pool(cache_dsa_k), pool(cache_dsa_v), block_bias=False)
    state = state_ffn_conv[l]
    y_s, a_s = _finish(xs2, o_a_s.reshape(db, WIDTH), o_b_s.reshape(db, WIDTH), mods_s[2:], g_ffn_l, wout, wupa, wupg,
                       conv_w[l], cb, wdn, tm=db, rows_per_group=db, prefix=(state[:, 0], state[:, 1]))
    conv_s = jnp.stack([state[:, 1], a_s], axis=1)
    kvs_shape = (1, db, ds, N_HEADS, HEAD_DIM)
    outs_s = (ka_s.reshape(kvs_shape), va_s.reshape(kvs_shape), kb_s.reshape(kvs_shape), vb_s.reshape(kvs_shape),
              kw_s[:, :IDX_DIM].reshape(1, db, ds, IDX_DIM), conv_s[None])
    return (y_p.reshape(bsz, t, d), y_s.reshape(db, ds, d)) + outs_p + outs_s
```

```python
import functools

import jax
import jax.numpy as jnp
from jax import lax
from jax.experimental import pallas as pl
from jax.experimental.pallas import tpu as pltpu

HEAD_DIM = 64
N_HEADS = 8
WIDTH = N_HEADS * HEAD_DIM
MOBA_BLOCK = 256
MOBA_TOPK = 3
DSA_TOPK = 256
IDX_HEADS = 8
IDX_DIM = 64
CONV_WIDTH = 3
EPS = 1e-6
N_MOD = 6
ATTN_SCALE = HEAD_DIM ** -0.5
N_QKV_CHUNKS = 7
LANES = 128
SUBLANES = 8
TQ = MOBA_BLOCK
SB = 2048
SEL_CHUNK = 1024
BPS = SB // TQ
FF_CHUNK = 256
PAGES_PER_STEP = 8
VMEM_LIMIT = 56 * 1024 * 1024

BF = jnp.bfloat16
F32 = jnp.float32
I32 = jnp.int32
NEG_INF = float("-inf")
INT_MIN = -2 ** 31


def _cp(sem):
    return pltpu.CompilerParams(dimension_semantics=sem, vmem_limit_bytes=VMEM_LIMIT)


def _const_spec(shape):
    nd = len(shape)
    return pl.BlockSpec(shape, lambda *_: (0,) * nd)


def _dot_nt(a, b):
    return lax.dot_general(a, b, (((1,), (1,)), ((), ())), preferred_element_type=F32)


def _mod_kernel(c_ref, w_ref, b_ref, o_ref):
    s = jax.nn.silu(c_ref[...])
    o_ref[...] = jnp.dot(s.astype(BF), w_ref[...].astype(BF), preferred_element_type=F32) + b_ref[...]


def _modulation(c_all, w_ada, b_ada):
    r, d = c_all.shape
    n = w_ada.shape[1]
    tn = 1024
    return pl.pallas_call(
        _mod_kernel,
        grid=(n // tn,),
        in_specs=[pl.BlockSpec((r, d), lambda j: (0, 0)),
                  pl.BlockSpec((d, tn), lambda j: (0, j)),
                  pl.BlockSpec((1, tn), lambda j: (0, j))],
        out_specs=pl.BlockSpec((r, tn), lambda j: (0, j)),
        out_shape=jax.ShapeDtypeStruct((r, n), F32),
        compiler_params=_cp(("arbitrary",)),
        name="modulation",
    )(c_all, w_ada, b_ada.reshape(1, n))


def _pre_kernel(x_ref, shift_ref, scale_ref, g_ref, wqkv_ref, wkw_ref, nrm_ref, gmat_ref, *outs, tm, with_kmean):
    x = x_ref[...]
    ms = jnp.mean(x * x, axis=-1, keepdims=True)
    xn = x * lax.rsqrt(ms + EPS) * g_ref[...]
    h = (xn * (1.0 + scale_ref[0]) + shift_ref[0]).astype(BF)
    norm_row = {0: 0, 1: 1, 3: 2, 4: 3}
    for c in range(N_QKV_CHUNKS):
        p = jnp.dot(h, wqkv_ref[:, c * WIDTH:(c + 1) * WIDTH], preferred_element_type=F32)
        if c in norm_row:
            ss = jnp.dot((p * p).astype(BF), gmat_ref[...], preferred_element_type=F32)
            k = norm_row[c]
            p = p * lax.rsqrt(ss * (1.0 / HEAD_DIM) + EPS) * nrm_ref[k:k + 1, :]
        outs[c][...] = p
        if c == 1 and with_kmean:
            outs[N_QKV_CHUNKS + 1][0] = jnp.mean(p.reshape(tm // MOBA_BLOCK, MOBA_BLOCK, WIDTH), axis=1)
    outs[N_QKV_CHUNKS][...] = jnp.dot(h, wkw_ref[...], preferred_element_type=F32)


def _pre_prompt_kernel(x_ref, shift_ref, scale_ref, g_ref, wqkv_ref, wkw_ref, nrm_ref, gmat_ref,
                       qa_ref, qb_ref, qi_ref, kw_ref, kat_ref, vat_ref, kbt_ref, vbt_ref,
                       katb_ref, vatb_ref, kbtb_ref, vbtb_ref, kwtb_ref, kit_ref, km_ref, *, tm):
    x = x_ref[...]
    ms = jnp.mean(x * x, axis=-1, keepdims=True)
    xn = x * lax.rsqrt(ms + EPS) * g_ref[...]
    h = (xn * (1.0 + scale_ref[0]) + shift_ref[0]).astype(BF)

    def proj(c, norm):
        p = jnp.dot(h, wqkv_ref[:, c * WIDTH:(c + 1) * WIDTH], preferred_element_type=F32)
        if norm is not None:
            ss = jnp.dot((p * p).astype(BF), gmat_ref[...], preferred_element_type=F32)
            p = p * lax.rsqrt(ss * (1.0 / HEAD_DIM) + EPS) * nrm_ref[norm:norm + 1, :]
        return p

    qa_ref[...] = (proj(0, 0) * ATTN_SCALE).astype(BF)
    ka = proj(1, 1)
    km_ref[0] = jnp.mean(ka.reshape(tm // MOBA_BLOCK, MOBA_BLOCK, WIDTH), axis=1)
    for p, t_ref, tb_ref in ((ka, kat_ref, katb_ref), (proj(2, None), vat_ref, vatb_ref)):
        pt = p.T
        t_ref[0] = pt
        tb_ref[0] = pt.astype(BF)
    qb_ref[...] = (proj(3, 2) * ATTN_SCALE).astype(BF)
    for p, t_ref, tb_ref in ((proj(4, 3), kbt_ref, kbtb_ref), (proj(5, None), vbt_ref, vbtb_ref)):
        pt = p.T
        t_ref[0] = pt
        tb_ref[0] = pt.astype(BF)
    qi_ref[...] = proj(6, None)
    kw = jnp.dot(h, wkw_ref[...], preferred_element_type=F32)
    kw_ref[...] = kw
    kwt = kw.T
    kwtb_ref[0] = kwt.astype(BF)
    kit_ref[0] = kwt[:IDX_DIM, :]


def _pre_project_prompt(x2, shift, scale, g_attn, wqkv, wkw, nrm, gmat, *, tm, bsz, t):
    n, d = x2.shape
    tpb = t // tm
    mod_spec = pl.BlockSpec((1, 1, d), lambda i: (i // tpb, 0, 0))
    row_spec = lambda w: pl.BlockSpec((tm, w), lambda i: (i, 0))
    tr_spec = lambda r: pl.BlockSpec((1, r, tm), lambda i: (i // tpb, 0, i % tpb))
    nb = tm // MOBA_BLOCK
    sds = jax.ShapeDtypeStruct
    out_shape = ([sds((n, WIDTH), BF)] * 2 + [sds((n, WIDTH), F32), sds((n, LANES), F32)]
                 + [sds((bsz, WIDTH, t), F32)] * 4 + [sds((bsz, WIDTH, t), BF)] * 4
                 + [sds((bsz, LANES, t), BF), sds((bsz, IDX_DIM, t), F32), sds((n // tm, nb, WIDTH), F32)])
    out_specs = ([row_spec(WIDTH)] * 3 + [row_spec(LANES)] + [tr_spec(WIDTH)] * 8
                 + [tr_spec(LANES), tr_spec(IDX_DIM), pl.BlockSpec((1, nb, WIDTH), lambda i: (i, 0, 0))])
    return pl.pallas_call(
        functools.partial(_pre_prompt_kernel, tm=tm),
        grid=(n // tm,),
        in_specs=[row_spec(d), mod_spec, mod_spec, _const_spec((1, d)),
                  _const_spec(wqkv.shape), _const_spec(wkw.shape), _const_spec(nrm.shape), _const_spec(gmat.shape)],
        out_specs=out_specs,
        out_shape=out_shape,
        compiler_params=_cp(("arbitrary",)),
        name="pre_project_prompt",
    )(x2, shift, scale, g_attn, wqkv, wkw, nrm, gmat)


def _pre_project(x2, shift, scale, g_attn, wqkv, wkw, nrm, gmat, *, tm, rows_per_group, with_kmean):
    n, d = x2.shape
    grid = (n // tm,)
    mr = shift.shape[1]
    tiles_per_group = rows_per_group // tm
    mod_spec = pl.BlockSpec((1, mr, d), lambda i: (i // tiles_per_group, 0, 0))
    row_spec = lambda w: pl.BlockSpec((tm, w), lambda i: (i, 0))
    out_shape = [jax.ShapeDtypeStruct((n, WIDTH), F32)] * N_QKV_CHUNKS + [jax.ShapeDtypeStruct((n, LANES), F32)]
    out_specs = [row_spec(WIDTH)] * N_QKV_CHUNKS + [row_spec(LANES)]
    if with_kmean:
        nb = tm // MOBA_BLOCK
        out_shape.append(jax.ShapeDtypeStruct((n // tm, nb, WIDTH), F32))
        out_specs.append(pl.BlockSpec((1, nb, WIDTH), lambda i: (i, 0, 0)))
    return pl.pallas_call(
        functools.partial(_pre_kernel, tm=tm, with_kmean=with_kmean),
        grid=grid,
        in_specs=[row_spec(d), mod_spec, mod_spec, _const_spec((1, d)),
                  _const_spec(wqkv.shape), _const_spec(wkw.shape), _const_spec(nrm.shape), _const_spec(gmat.shape)],
        out_specs=out_specs,
        out_shape=out_shape,
        compiler_params=_cp(("arbitrary",)),
        name="pre_project",
    )(x2, shift, scale, g_attn, wqkv, wkw, nrm, gmat)


def _pair_half(shape, axis):
    return lax.broadcasted_iota(I32, shape, axis) // HEAD_DIM


def _two_pass_attention(q_pair, par, kt_ref, vt_ref, slope, i, bias_at, l_ref, m_ref, acc_ref):
    n_steps = i // BPS + 1
    q = jnp.where(_pair_half((TQ, LANES), 1) == par, q_pair, jnp.zeros_like(q_pair))
    own_rows = _pair_half((LANES, SB), 0) == par
    m_ref[...] = jnp.full((TQ, TQ), NEG_INF, F32)

    def keys(ref, j):
        return ref[0, :, pl.ds(pl.multiple_of(j * SB, SB), SB)]

    def pass1(j, carry):
        cols = lax.broadcasted_iota(I32, (1, SB), 1) + (j * SB - i * TQ)
        l = jnp.dot(q, keys(kt_ref, j), preferred_element_type=F32) + slope * cols.astype(F32) + bias_at(j)
        l_ref[j] = l
        mx = m_ref[...]
        for k in range(BPS):
            mx = jnp.maximum(mx, l[:, k * TQ:(k + 1) * TQ])
        m_ref[...] = mx
        return carry

    lax.fori_loop(0, n_steps, pass1, 0)
    m = jnp.max(m_ref[...], axis=-1, keepdims=True)
    acc_ref[...] = jnp.zeros((TQ, LANES), F32)

    def pass2(j, carry):
        p = jnp.exp((l_ref[j] - m).astype(BF))
        vt = jnp.where(own_rows, keys(vt_ref, j), jnp.ones((LANES, SB), BF))
        acc_ref[...] += _dot_nt(p, vt)
        return carry

    lax.fori_loop(0, n_steps, pass2, 0)
    acc = acc_ref[...]
    return acc / pltpu.roll(acc, HEAD_DIM, axis=1)


def _store_pair_half(o_ref, par, val):
    keep = _pair_half((TQ, LANES), 1) == par

    @pl.when(par == 0)
    def _():
        o_ref[...] = jnp.where(keep, val, 0.0)

    @pl.when(par != 0)
    def _():
        o_ref[...] = jnp.where(keep, val, o_ref[...])


def _moba_prompt_kernel(slopes_ref, q_ref, kt_ref, vt_ref, kmt_ref, o_ref, l_ref, m_ref, acc_ref, bias_ref, sel_ref):
    pr = pl.program_id(1)
    i = pl.program_id(2)
    par = pl.program_id(3)
    sub = i % BPS
    row0 = pl.multiple_of(sub * TQ, TQ)

    @pl.when(sub == 0)
    def _():
        qs = q_ref[...]
        qs = jnp.where(_pair_half(qs.shape, 1) == par, qs, jnp.zeros_like(qs))
        s = jnp.dot(qs, kmt_ref[0], preferred_element_type=F32)
        lane_s = lax.broadcasted_iota(I32, s.shape, 1)
        own = i + lax.broadcasted_iota(I32, s.shape, 0) // TQ
        tri = (lax.broadcasted_iota(I32, (LANES, LANES), 0) <= lax.broadcasted_iota(I32, (LANES, LANES), 1))
        tri = jnp.where(tri, 1.0, 0.0).astype(BF)
        s = jnp.where(lane_s < own, s, NEG_INF)
        bias = jnp.full(s.shape, NEG_INF, F32)
        for _ in range(MOBA_TOPK):
            mx = jnp.max(s, axis=-1, keepdims=True)
            eq = s == mx
            rank = jnp.dot(jnp.where(eq, 1.0, 0.0).astype(BF), tri, preferred_element_type=F32)
            pick = eq & (rank == 1.0) & (mx > NEG_INF)
            bias = jnp.where(pick, 0.0, bias)
            s = jnp.where(pick, NEG_INF, s)
        sel_ref[par] = bias

    q = q_ref[pl.ds(row0, TQ), :]
    bias_ref[...] = sel_ref[par, pl.ds(row0, TQ), :]
    lane = lax.broadcasted_iota(I32, (TQ, LANES), 1)
    r = lax.broadcasted_iota(I32, (TQ, TQ), 0)
    c = lax.broadcasted_iota(I32, (TQ, TQ), 1)
    causal = jnp.where(c <= r, 0.0, NEG_INF).astype(F32)

    def bias_at(j):
        parts = []
        for k in range(BPS):
            blk = j * BPS + k
            col = jnp.max(jnp.where(lane == blk, bias_ref[...], NEG_INF), axis=-1, keepdims=True)
            parts.append(jnp.where(blk == i, causal, jnp.broadcast_to(col, (TQ, TQ))))
        return jnp.concatenate(parts, axis=1)

    out = _two_pass_attention(q, par, kt_ref, vt_ref, slopes_ref[2 * pr + par], i, bias_at, l_ref, m_ref, acc_ref)
    _store_pair_half(o_ref, par, out)


def _moba_prompt(slopes, q, kt, vt, kmt, bsz, t):
    nq = t // TQ
    nsb = t // SB
    npair = N_HEADS // 2
    grid_spec = pltpu.PrefetchScalarGridSpec(
        num_scalar_prefetch=1,
        grid=(bsz, npair, nq, 2),
        in_specs=[pl.BlockSpec((SB, LANES), lambda bi, pr, i, par, s: (bi * nsb + i // BPS, pr)),
                  pl.BlockSpec((1, LANES, t), lambda bi, pr, i, par, s: (bi, pr, 0)),
                  pl.BlockSpec((1, LANES, t), lambda bi, pr, i, par, s: (bi, pr, 0)),
                  pl.BlockSpec((1, LANES, LANES), lambda bi, pr, i, par, s: (bi, pr, 0))],
        out_specs=pl.BlockSpec((TQ, LANES), lambda bi, pr, i, par, s: (bi * nq + i, pr)),
        scratch_shapes=[pltpu.VMEM((nsb, TQ, SB), F32), pltpu.VMEM((TQ, TQ), F32),
                        pltpu.VMEM((TQ, LANES), F32), pltpu.VMEM((TQ, LANES), F32), pltpu.VMEM((2, SB, LANES), F32)],
    )
    return pl.pallas_call(
        _moba_prompt_kernel,
        grid_spec=grid_spec,
        out_shape=jax.ShapeDtypeStruct((bsz * t, WIDTH), F32),
        compiler_params=_cp(("arbitrary",) * 4),
        name="moba_prompt",
    )(slopes, q, kt, vt, kmt)


def _threshold_from_code(code):
    bits = code ^ ((code >> 31) & 0x7FFFFFFF)
    return lax.bitcast_convert_type(bits, F32)


def _dsa_select(i, n_sel, idx_bits, sk_ref, b_ref, jc_ref):
    n_chunks = (i * TQ) // SEL_CHUNK + 1
    per_step = SB // SEL_CHUNK
    rc = TQ // 2
    c = lax.broadcasted_iota(I32, (rc, SEL_CHUNK), 1)

    def chunk(ref, j, rows):
        return ref.at[j // per_step, rows, pl.ds(pl.multiple_of((j % per_step) * SEL_CHUNK, SEL_CHUNK), SEL_CHUNK)]

    for r0 in range(0, TQ, rc):
        rows = slice(r0, r0 + rc)
        gr = lax.broadcasted_iota(I32, (rc, SEL_CHUNK), 0) + (i * TQ + r0)

        def count(pred, rows=rows):
            def body(j, acc):
                hit = jnp.where(pred(j, chunk(sk_ref, j, rows)[...]), 1.0, 0.0)
                for k in range(SEL_CHUNK // LANES):
                    acc = acc + hit[:, k * LANES:(k + 1) * LANES]
                return acc

            acc = lax.fori_loop(0, n_chunks, body, jnp.zeros((rc, LANES), F32))
            return jnp.sum(acc, axis=-1, keepdims=True)

        def bit_step(it, v, count=count):
            cand = v + jnp.left_shift(jnp.int32(1), 31 - it)
            cand_f = _threshold_from_code(cand)
            cnt = count(lambda j, sc: sc >= cand_f)
            return jnp.where(cnt >= n_sel, cand, v)

        v = lax.fori_loop(0, 32, bit_step, jnp.full((rc, 1), INT_MIN, I32))
        few = v == INT_MIN
        thr = jnp.where(few, NEG_INF, _threshold_from_code(v))
        cnt_gt = count(lambda j, sc: sc > thr)
        cnt_ge = count(lambda j, sc: sc >= thr)
        need = n_sel - cnt_gt
        trim = (cnt_ge > n_sel) & jnp.logical_not(few)
        jc_ref[rows, :] = jnp.full((rc, LANES), 2 ** idx_bits, I32)

        @pl.when(jnp.max(jnp.where(trim, 1.0, 0.0)) > 0.0)
        def _(rows=rows, count=count, thr=thr, need=need, trim=trim):
            def idx_step(it, jcap):
                cand = jcap + jnp.left_shift(jnp.int32(1), idx_bits - 1 - it)
                cnt = count(lambda j, sc: (sc == thr) & (c + j * SEL_CHUNK < cand))
                return jnp.where(cnt < need, cand, jcap)

            jcap = lax.fori_loop(0, idx_bits, idx_step, jnp.zeros((rc, 1), I32))
            jc_ref[rows, :] = jnp.broadcast_to(jnp.where(trim, jcap, 2 ** idx_bits), (rc, LANES))

        jcap = jc_ref[rows, 0:1]

        def write(j, carry, rows=rows, thr=thr, jcap=jcap, gr=gr):
            sc = chunk(sk_ref, j, rows)[...]
            gc = c + j * SEL_CHUNK
            sel = ((sc > thr) | ((sc == thr) & (gc <= jcap))) & (gc <= gr)
            chunk(b_ref, j, rows)[...] = jnp.where(sel, 0.0, NEG_INF)
            return carry

        lax.fori_loop(0, (i // BPS + 1) * per_step, write, 0)


def _dsa_prompt_kernel(slopes_ref, qi_ref, kw_ref, kwt_ref, q_ref, kt_ref, vt_ref, o_ref,
                       sk_ref, b_ref, l_ref, m_ref, acc_ref, jc_ref, qz_ref, *, n_sel, idx_bits):
    i = pl.program_id(1)
    h = pl.program_id(2)
    par = h % 2

    @pl.when(h == 0)
    def _():
        wi = kw_ref[...] * (IDX_HEADS * IDX_DIM) ** -0.5
        wcols = [wi[:, IDX_DIM + hh:IDX_DIM + hh + 1] for hh in range(IDX_HEADS)]
        gr = lax.broadcasted_iota(I32, (TQ, SB), 0) + i * TQ
        c = lax.broadcasted_iota(I32, (TQ, SB), 1)
        first = _pair_half((TQ, LANES), 1) == 0
        for hh in range(IDX_HEADS):
            qp = qi_ref[:, (hh // 2) * LANES:(hh // 2 + 1) * LANES]
            if hh % 2:
                qp = pltpu.roll(qp, HEAD_DIM, axis=1)
            qz_ref[hh] = jnp.where(first, qp, 0.0).astype(BF)

        def score_step(j, carry):
            kit = kwt_ref[0, :, pl.ds(pl.multiple_of(j * SB, SB), SB)]
            sc = jnp.maximum(jnp.dot(qz_ref[0], kit, preferred_element_type=F32), 0.0) * wcols[0]
            for hh in range(1, IDX_HEADS):
                sc = sc + jnp.maximum(jnp.dot(qz_ref[hh], kit, preferred_element_type=F32), 0.0) * wcols[hh]
            sk_ref[j] = jnp.where(c + j * SB <= gr, sc, NEG_INF)
            return carry

        lax.fori_loop(0, i // BPS + 1, score_step, 0)
        _dsa_select(i, n_sel, idx_bits, sk_ref, b_ref, jc_ref)

    out = _two_pass_attention(q_ref[...], par, kt_ref, vt_ref, slopes_ref[h], i, lambda j: b_ref[j],
                              l_ref, m_ref, acc_ref)
    _store_pair_half(o_ref, par, out)


def _dsa_prompt(slopes, qi, kw, kwt, q, kt, vt, n_sel, bsz, t):
    nq = t // TQ
    nsb = t // SB
    idx_bits = max(1, (t - 1).bit_length())
    blk = lambda shape, fn: pl.BlockSpec(shape, fn)
    grid_spec = pltpu.PrefetchScalarGridSpec(
        num_scalar_prefetch=1,
        grid=(bsz, nq, N_HEADS),
        in_specs=[blk((TQ, WIDTH), lambda bi, i, hi, s: (bi * nq + i, 0)),
                  blk((TQ, LANES), lambda bi, i, hi, s: (bi * nq + i, 0)),
                  blk((1, LANES, t), lambda bi, i, hi, s: (bi, 0, 0)),
                  blk((TQ, LANES), lambda bi, i, hi, s: (bi * nq + i, hi // 2)),
                  blk((1, LANES, t), lambda bi, i, hi, s: (bi, hi // 2, 0)),
                  blk((1, LANES, t), lambda bi, i, hi, s: (bi, hi // 2, 0))],
        out_specs=blk((TQ, LANES), lambda bi, i, hi, s: (bi * nq + i, hi // 2)),
        scratch_shapes=[pltpu.VMEM((nsb, TQ, SB), F32), pltpu.VMEM((nsb, TQ, SB), F32), pltpu.VMEM((nsb, TQ, SB), F32),
                        pltpu.VMEM((TQ, TQ), F32), pltpu.VMEM((TQ, LANES), F32), pltpu.VMEM((TQ, LANES), I32),
                        pltpu.VMEM((IDX_HEADS, TQ, LANES), BF)],
    )
    return pl.pallas_call(
        functools.partial(_dsa_prompt_kernel, n_sel=n_sel, idx_bits=idx_bits),
        grid_spec=grid_spec,
        out_shape=jax.ShapeDtypeStruct((bsz * t, WIDTH), F32),
        compiler_params=_cp(("arbitrary", "arbitrary", "arbitrary")),
        name="dsa_prompt",
    )(slopes, qi, kw, kwt, q, kt, vt)


def _finish_kernel(*refs, tm, nchunk, seq_mode, tiles_per_seq):
    (x_ref, oa_ref, ob_ref, gate_a_ref, shift_f_ref, scale_f_ref, gate_f_ref, g_ref, wout_ref,
     wupa_ref, wupg_ref, cw_ref, cb_ref, wdn_ref) = refs[:14]
    if seq_mode:
        y_ref, st_ref, abuf_ref, carry_ref, yacc_ref = refs[14:]
    else:
        pre0_ref, pre1_ref, y_ref, st_ref, yacc_ref = refs[14:]
    i = pl.program_id(0)
    o = (jnp.dot(oa_ref[...].astype(BF), wout_ref[0:WIDTH, :], preferred_element_type=F32)
         + jnp.dot(ob_ref[...].astype(BF), wout_ref[WIDTH:2 * WIDTH, :], preferred_element_type=F32))
    x1 = x_ref[...] + gate_a_ref[0] * o
    ms = jnp.mean(x1 * x1, axis=-1, keepdims=True)
    h = ((x1 * lax.rsqrt(ms + EPS) * g_ref[...]) * (1.0 + scale_f_ref[0]) + shift_f_ref[0]).astype(BF)
    yacc_ref[...] = jnp.zeros_like(yacc_ref)
    for c in range(nchunk):
        cols = slice(c * FF_CHUNK, (c + 1) * FF_CHUNK)
        a = jnp.dot(h, wupa_ref[c], preferred_element_type=F32)
        gt = jnp.dot(h, wupg_ref[c], preferred_element_type=F32)
        if seq_mode:
            first = (i % tiles_per_seq) == 0
            buf = abuf_ref.at[c % 2]
            buf[0:SUBLANES, :] = jnp.where(first, 0.0, carry_ref[c])
            buf[SUBLANES:SUBLANES + tm, :] = a
            a1 = buf[SUBLANES - 1:SUBLANES - 1 + tm, :]
            a2 = buf[SUBLANES - 2:SUBLANES - 2 + tm, :]
            carry_ref[c] = a[tm - SUBLANES:tm, :]
            st_ref[0, :, cols] = a[tm - (CONV_WIDTH - 1):tm, :]
        else:
            a2 = pre0_ref[:, cols]
            a1 = pre1_ref[:, cols]
            st_ref[:, cols] = a
        conv = cb_ref[:, cols] + a2 * cw_ref[0:1, cols]
        conv = conv + a1 * cw_ref[1:2, cols]
        conv = conv + a * cw_ref[2:3, cols]
        act = (jax.nn.gelu(conv) * gt).astype(BF)
        yacc_ref[...] += jnp.dot(act, wdn_ref[c], preferred_element_type=F32)
    y_ref[...] = x1 + gate_f_ref[0] * yacc_ref[...]


def _finish(x2, oa, ob, mods, g_ffn, wout, wupa, wupg, conv_w, conv_b, wdn, *, tm, rows_per_group, prefix=None):
    n, d = x2.shape
    nchunk = wupa.shape[0]
    f = nchunk * FF_CHUNK
    seq_mode = prefix is None
    tiles_per_seq = rows_per_group // tm
    mr = mods[0].shape[1]
    mod_spec = pl.BlockSpec((1, mr, d), lambda i: (i // tiles_per_seq, 0, 0))
    row_spec = lambda w: pl.BlockSpec((tm, w), lambda i: (i, 0))
    in_specs = ([row_spec(d), row_spec(WIDTH), row_spec(WIDTH)] + [mod_spec] * 4 +
                [_const_spec((1, d)), _const_spec(wout.shape), _const_spec(wupa.shape), _const_spec(wupg.shape),
                 _const_spec(conv_w.shape), _const_spec((1, f)), _const_spec(wdn.shape)])
    args = [x2, oa, ob, *mods, g_ffn, wout, wupa, wupg, conv_w, conv_b, wdn]
    scratch = []
    if seq_mode:
        nseq = n // rows_per_group
        st_shape = jax.ShapeDtypeStruct((nseq, CONV_WIDTH - 1, f), F32)
        st_spec = pl.BlockSpec((1, CONV_WIDTH - 1, f), lambda i: (i // tiles_per_seq, 0, 0))
        scratch += [pltpu.VMEM((2, SUBLANES + tm, FF_CHUNK), F32), pltpu.VMEM((nchunk, SUBLANES, FF_CHUNK), F32)]
    else:
        in_specs += [row_spec(f), row_spec(f)]
        args += list(prefix)
        st_shape = jax.ShapeDtypeStruct((n, f), F32)
        st_spec = row_spec(f)
    scratch.append(pltpu.VMEM((tm, d), F32))
    return pl.pallas_call(
        functools.partial(_finish_kernel, tm=tm, nchunk=nchunk, seq_mode=seq_mode, tiles_per_seq=tiles_per_seq),
        grid=(n // tm,),
        in_specs=in_specs,
        out_specs=[row_spec(d), st_spec],
        out_shape=[jax.ShapeDtypeStruct((n, d), F32), st_shape],
        scratch_shapes=scratch,
        compiler_params=_cp(("arbitrary",)),
        name="finish",
    )(*args)


def _block_diag_rows(row):
    hrow = lax.broadcasted_iota(I32, (N_HEADS, WIDTH), 0)
    hlane = lax.broadcasted_iota(I32, (N_HEADS, WIDTH), 1) // HEAD_DIM
    return jnp.where(hrow == hlane, jnp.broadcast_to(row, (N_HEADS, WIDTH)), 0.0)


def _sample_stats_kernel(pt_ref, qi_ref, wi_ref, kin_ref, qa_ref, *refs, page, nblk):
    pps = PAGES_PER_STEP
    kidx_refs = refs[:pps]
    kmoba_refs = refs[pps:2 * pps]
    sc_ref, scn_ref, bb_ref, km_ref = refs[2 * pps:]
    s = pl.program_id(1)
    qi = qi_ref[0].astype(BF)
    wi = wi_ref[0] * (IDX_HEADS * IDX_DIM) ** -0.5
    ppb = MOBA_BLOCK // page
    lane = lax.broadcasted_iota(I32, (WIDTH, LANES), 1)

    @pl.when(s == 0)
    def _():
        km_ref[...] = jnp.zeros(km_ref.shape, F32)

    kidx = jnp.concatenate([kidx_refs[k][0, 0].astype(BF) for k in range(pps)], axis=1)
    rel = jnp.maximum(jnp.dot(qi, kidx, preferred_element_type=F32), 0.0)
    sc_ref[0] = jnp.sum(rel * wi, axis=0, keepdims=True)
    for kb in range(pps // ppb):
        tot = kmoba_refs[kb * ppb][0, 0].reshape(WIDTH, page)
        for k in range(1, ppb):
            tot = tot + kmoba_refs[kb * ppb + k][0, 0].reshape(WIDTH, page)
        mean = jnp.sum(tot, axis=-1, keepdims=True) * (1.0 / MOBA_BLOCK)
        km_ref[...] = jnp.where(lane == s * (pps // ppb) + kb, mean, km_ref[...])

    @pl.when(s == pl.num_programs(1) - 1)
    def _():
        kin = jnp.broadcast_to(kin_ref[0], (SUBLANES, IDX_DIM)).astype(BF)
        reln = jnp.maximum(_dot_nt(qi, kin), 0.0)
        scn = jnp.sum(reln[:, 0:1] * wi, axis=0, keepdims=True)
        scn_ref[0] = jnp.broadcast_to(scn, (1, LANES))
        qbd = _block_diag_rows(qa_ref[0]).astype(BF)
        sb = jnp.dot(qbd, km_ref[...].astype(BF), preferred_element_type=F32)
        lane_b = lax.broadcasted_iota(I32, sb.shape, 1)
        lanef = lane_b.astype(F32)
        sb = jnp.where(lane_b < nblk, sb, NEG_INF)
        bias = jnp.full(sb.shape, NEG_INF, F32)
        for _ in range(MOBA_TOPK):
            mx = jnp.max(sb, axis=-1, keepdims=True)
            first = jnp.min(jnp.where(sb == mx, lanef, float(LANES)), axis=-1, keepdims=True)
            pick = (lanef == first) & (mx > NEG_INF)
            bias = jnp.where(pick, 0.0, bias)
            sb = jnp.where(pick, NEG_INF, sb)
        bb_ref[0] = bias


def _sample_stats(page_table, qi_s, wi_s, ki_s, qa_s, pool_kidx, pool_mk):
    db, n_pages = page_table.shape
    page = pool_kidx.shape[3]
    pps = PAGES_PER_STEP
    past = n_pages * page
    nblk = past // MOBA_BLOCK
    assert n_pages % pps == 0 and pps % (MOBA_BLOCK // page) == 0 and nblk <= LANES and page == LANES
    kidx_spec = lambda k: pl.BlockSpec((1, 1, IDX_DIM, page), lambda b, s, pt: (0, pt[b, s * pps + k], 0, 0))
    kmoba_spec = lambda k: pl.BlockSpec((1, 1, N_HEADS, HEAD_DIM, page),
                                        lambda b, s, pt: (0, pt[b, s * pps + k], 0, 0, 0))
    per_seq = lambda shape: pl.BlockSpec((1,) + shape, lambda b, s, pt: (b, 0, 0))
    grid_spec = pltpu.PrefetchScalarGridSpec(
        num_scalar_prefetch=1,
        grid=(db, n_pages // pps),
        in_specs=([per_seq((IDX_HEADS, IDX_DIM)), per_seq((IDX_HEADS, 1)), per_seq((1, IDX_DIM)), per_seq((1, WIDTH))]
                  + [kidx_spec(k) for k in range(pps)] + [kmoba_spec(k) for k in range(pps)]),
        out_specs=[pl.BlockSpec((1, 1, pps * page), lambda b, s, pt: (b, 0, s)),
                   per_seq((1, LANES)), per_seq((N_HEADS, LANES))],
        scratch_shapes=[pltpu.VMEM((WIDTH, LANES), F32)],
    )
    return pl.pallas_call(
        functools.partial(_sample_stats_kernel, page=page, nblk=nblk),
        grid_spec=grid_spec,
        out_shape=[jax.ShapeDtypeStruct((db, 1, past), F32), jax.ShapeDtypeStruct((db, 1, LANES), F32),
                   jax.ShapeDtypeStruct((db, N_HEADS, LANES), F32)],
        compiler_params=_cp(("arbitrary", "arbitrary")),
        name="sample_stats",
    )(page_table, qi_s, wi_s, ki_s, qa_s, *([pool_kidx] * pps), *([pool_mk] * pps))


def _dsa_sample_select_kernel(sc_ref, scn_ref, b_ref, bn_ref, *, n_sel, idx_bits):
    sc = sc_ref[...]
    scn = scn_ref[:, 0:1]
    db, past = sc.shape
    gidx = lax.broadcasted_iota(I32, sc.shape, 1)

    def count(pred, pred_new):
        return (jnp.sum(jnp.where(pred, 1.0, 0.0), axis=-1, keepdims=True) + jnp.where(pred_new, 1.0, 0.0))

    def bit_step(it, v):
        cand = v + jnp.left_shift(jnp.int32(1), 31 - it)
        cand_f = _threshold_from_code(cand)
        cnt = count(sc >= cand_f, scn >= cand_f)
        return jnp.where(cnt >= n_sel, cand, v)

    v = lax.fori_loop(0, 32, bit_step, jnp.full((db, 1), INT_MIN, I32))
    few = v == INT_MIN
    thr = jnp.where(few, NEG_INF, _threshold_from_code(v))
    cnt_gt = count(sc > thr, scn > thr)
    cnt_ge = count(sc >= thr, scn >= thr)
    need = n_sel - cnt_gt
    trim = (cnt_ge > n_sel) & jnp.logical_not(few)

    def idx_step(it, jcap):
        cand = jcap + jnp.left_shift(jnp.int32(1), idx_bits - 1 - it)
        cnt = count((sc == thr) & (gidx < cand), (scn == thr) & (past < cand))
        return jnp.where(cnt < need, cand, jcap)

    jcap = lax.fori_loop(0, idx_bits, idx_step, jnp.zeros((db, 1), I32))
    jcap = jnp.where(trim, jcap, 2 ** idx_bits)
    sel = (sc > thr) | ((sc == thr) & (gidx <= jcap))
    seln = (scn > thr) | ((scn == thr) & (past <= jcap))
    bn_ref[...] = jnp.broadcast_to(jnp.where(seln, 0.0, NEG_INF), bn_ref.shape)
    b_ref[...] = jnp.where(sel, 0.0, NEG_INF)


def _dsa_sample_select(scores, score_new, n_sel):
    db, past = scores.shape
    idx_bits = past.bit_length()
    return pl.pallas_call(
        functools.partial(_dsa_sample_select_kernel, n_sel=n_sel, idx_bits=idx_bits),
        out_shape=[jax.ShapeDtypeStruct((db, past), F32), jax.ShapeDtypeStruct((db, LANES), F32)],
        compiler_params=pltpu.CompilerParams(vmem_limit_bytes=VMEM_LIMIT),
        name="dsa_sample_select",
    )(scores, score_new)


def _paged_attention_kernel(pt_ref, slopes_ref, q_ref, kn_ref, vn_ref, bias_ref, bn_ref, *refs, page, block_bias):
    pps = PAGES_PER_STEP
    k_refs = refs[:pps]
    v_refs = refs[pps:2 * pps]
    o_ref, l_ref, mx_ref, m_ref, ps_ref, acc_ref, sl_ref, qbd_ref = refs[2 * pps:]
    s = pl.program_id(1)
    ns = pl.num_programs(1) // 2
    past = ns * pps * page
    lane = lax.broadcasted_iota(I32, (N_HEADS, LANES), 1)
    hrow = lax.broadcasted_iota(I32, (N_HEADS, LANES), 0)
    pages = lambda prefs: jnp.concatenate([r[0, 0].reshape(WIDTH, page).astype(BF) for r in prefs], axis=1)

    @pl.when(s == 0)
    def _():
        sl = jnp.zeros((N_HEADS, LANES), F32)
        for h in range(N_HEADS):
            sl = jnp.where(hrow == h, slopes_ref[h], sl)
        sl_ref[...] = sl
        mx_ref[...] = jnp.full(mx_ref.shape, NEG_INF, F32)
        qbd_ref[...] = _block_diag_rows(q_ref[0] * ATTN_SCALE).astype(BF)

    @pl.when(s < ns)
    def _():
        wide = pps * page
        pos = s * wide + lax.broadcasted_iota(I32, (1, wide), 1)
        logits = jnp.dot(qbd_ref[...], pages(k_refs), preferred_element_type=F32)
        logits = logits - sl_ref[:, 0:1] * (past - pos).astype(F32)
        if block_bias:
            cols = []
            for k in range(pps):
                blk = (s * pps + k) // (MOBA_BLOCK // page)
                col = jnp.max(jnp.where(lane == blk, bias_ref[0], NEG_INF), axis=-1, keepdims=True)
                cols.append(jnp.broadcast_to(col, (N_HEADS, page)))
            logits = logits + jnp.concatenate(cols, axis=1)
        else:
            logits = logits + bias_ref[0]
        l_ref[s] = logits
        mx_ref[...] = jnp.maximum(mx_ref[...], logits)

    @pl.when(s == ns - 1)
    def _():
        kn = kn_ref[0].astype(BF).astype(F32)
        ln = jnp.sum(qbd_ref[...].astype(F32) * kn, axis=-1, keepdims=True) + bn_ref[0, :, 0:1]
        m = jnp.maximum(jnp.max(mx_ref[...], axis=-1, keepdims=True), ln)
        m_ref[...] = jnp.broadcast_to(m, m_ref.shape)
        pn = jnp.exp(ln - m)
        ps_ref[...] = jnp.zeros(ps_ref.shape, F32)
        ps_ref[:, 0:1] = pn
        acc_ref[...] = pn.astype(BF).astype(F32) * vn_ref[0].astype(BF).astype(F32)

    @pl.when(s >= ns)
    def _():
        p = jnp.exp(l_ref[s - ns] - m_ref[:, 0:1])
        ps_ref[...] += p
        acc_ref[...] += _dot_nt(p.astype(BF), pages(v_refs))

    @pl.when(s == 2 * ns - 1)
    def _():
        o = acc_ref[...] / jnp.sum(ps_ref[...], axis=-1, keepdims=True)
        hr = lax.broadcasted_iota(I32, (N_HEADS, WIDTH), 0)
        hl = lax.broadcasted_iota(I32, (N_HEADS, WIDTH), 1) // HEAD_DIM
        o_ref[0] = jnp.sum(jnp.where(hr == hl, o, 0.0), axis=0, keepdims=True)


def _paged_attention(page_table, slopes, q_s, k_new, v_new, bias, bias_new, pool_k, pool_v, *, block_bias):
    db, n_pages = page_table.shape
    page = pool_k.shape[4]
    pps = PAGES_PER_STEP
    assert n_pages % pps == 0 and page == LANES
    ns = n_pages // pps
    pool_blk = (1, 1, N_HEADS, HEAD_DIM, page)
    k_spec = lambda k: pl.BlockSpec(pool_blk, lambda b, s, pt, sl: (0, pt[b, jnp.minimum(s, ns - 1) * pps + k], 0, 0, 0))
    v_spec = lambda k: pl.BlockSpec(pool_blk, lambda b, s, pt, sl: (0, pt[b, jnp.maximum(s - ns, 0) * pps + k], 0, 0, 0))
    per_seq = lambda shape: pl.BlockSpec((1,) + shape, lambda b, s, pt, sl: (b, 0, 0))
    if block_bias:
        bias_spec = per_seq((N_HEADS, LANES))
    else:
        bias_spec = pl.BlockSpec((1, 1, pps * page), lambda b, s, pt, sl: (b, 0, jnp.minimum(s, ns - 1)))
    grid_spec = pltpu.PrefetchScalarGridSpec(
        num_scalar_prefetch=2,
        grid=(db, 2 * ns),
        in_specs=([per_seq((1, WIDTH))] * 3 + [bias_spec, per_seq((1, LANES))]
                  + [k_spec(k) for k in range(pps)] + [v_spec(k) for k in range(pps)]),
        out_specs=per_seq((1, WIDTH)),
        scratch_shapes=[pltpu.VMEM((ns, N_HEADS, pps * page), F32), pltpu.VMEM((N_HEADS, pps * page), F32),
                        pltpu.VMEM((N_HEADS, LANES), F32), pltpu.VMEM((N_HEADS, pps * page), F32),
                        pltpu.VMEM((N_HEADS, WIDTH), F32), pltpu.VMEM((N_HEADS, LANES), F32),
                        pltpu.VMEM((N_HEADS, WIDTH), BF)],
    )
    return pl.pallas_call(
        functools.partial(_paged_attention_kernel, page=page, block_bias=block_bias),
        grid_spec=grid_spec,
        out_shape=jax.ShapeDtypeStruct((db, 1, WIDTH), F32),
        compiler_params=_cp(("arbitrary", "arbitrary")),
        name="paged_attention",
    )(page_table, slopes, q_s, k_new, v_new, bias, bias_new, *([pool_k] * pps), *([pool_v] * pps))


def _alibi_slopes():
    n = 2 * N_HEADS
    i = jnp.arange(1, n + 1, dtype=F32)
    m = jnp.exp2(-8.0 * i / n)
    return m[0::2], m[1::2]


def kernel(x_prompt, x_sample, cache_moba_k, cache_moba_v, cache_dsa_k, cache_dsa_v, cache_dsa_kidx, state_ffn_conv, page_table, c_prompt, c_sample, w_ada, b_ada, g_attn, w_in, qn_a, kn_a, qn_b, kn_b, w_out, g_ffn, w_up, conv_w, conv_b, w_down):
    bsz, t, d = x_prompt.shape
    db, ds, _ = x_sample.shape
    depth = w_in.shape[0]
    assert depth == 1 and ds == 1 and t % SB == 0
    f = w_down.shape[1]
    nchunk = f // FF_CHUNK
    assert nchunk * FF_CHUNK == f
    slopes_a, slopes_b = _alibi_slopes()
    l = 0

    ncols = N_QKV_CHUNKS * WIDTH
    wqkv = w_in[l][:, :ncols].astype(BF)
    wkw = jnp.pad(w_in[l][:, ncols:], ((0, 0), (0, LANES - (w_in.shape[2] - ncols)))).astype(BF)
    nrm = jnp.stack([jnp.tile(g[l], N_HEADS) for g in (qn_a, kn_a, qn_b, kn_b)])
    hid = jnp.arange(WIDTH) // HEAD_DIM
    gmat = (hid[:, None] == hid[None, :]).astype(BF)
    wout = w_out[l].astype(BF)
    wupa = w_up[l][:, :f].astype(BF).reshape(d, nchunk, FF_CHUNK).transpose(1, 0, 2)
    wupg = w_up[l][:, f:].astype(BF).reshape(d, nchunk, FF_CHUNK).transpose(1, 0, 2)
    wdn = w_down[l].astype(BF).reshape(nchunk, FF_CHUNK, d)
    g_attn_l = g_attn[l].reshape(1, d)
    g_ffn_l = g_ffn[l].reshape(1, d)
    cb = conv_b[l].reshape(1, f)

    rc = bsz + db
    rpad = -rc % SUBLANES
    c_all = jnp.concatenate([c_prompt, c_sample, jnp.zeros((rpad, d), F32)], axis=0)
    mod = _modulation(c_all, w_ada[l], b_ada[l])
    mods_p = [m.reshape(bsz, 1, d) for m in jnp.split(mod[:bsz], N_MOD, axis=-1)]
    mods_s = [m.reshape(1, db, d) for m in jnp.split(mod[bsz:rc], N_MOD, axis=-1)]

    n = bsz * t
    xp2 = x_prompt.reshape(n, d)
    tm = 512
    (qa_bf, qb_bf, qi, kw, kat, vat, kbt, vbt, kat_bf, vat_bf, kbt_bf, vbt_bf, kwt_bf, kit, kmean) = _pre_project_prompt(
        xp2, mods_p[0], mods_p[1], g_attn_l, wqkv, wkw, nrm, gmat, tm=tm, bsz=bsz, t=t)
    nb = t // TQ
    assert nb <= LANES
    kmt = kmean.reshape(bsz, nb, WIDTH).transpose(0, 2, 1).astype(BF)
    kmt = jnp.pad(kmt, ((0, 0), (0, 0), (0, LANES - nb)))
    o_a = _moba_prompt(slopes_a, qa_bf, kat_bf, vat_bf, kmt, bsz, t)
    n_sel = min(DSA_TOPK, t // 4)
    o_b = _dsa_prompt(slopes_b, qi, kw, kwt_bf, qb_bf, kbt_bf, vbt_bf, n_sel, bsz, t)
    y_p, conv_p = _finish(xp2, o_a, o_b, mods_p[2:], g_ffn_l, wout, wupa, wupg, conv_w[l], cb, wdn,
                          tm=tm, rows_per_group=t)

    kv_out = lambda a: a.reshape(1, bsz, N_HEADS, HEAD_DIM, t).transpose(0, 1, 4, 2, 3)
    outs_p = (kv_out(kat), kv_out(vat), kv_out(kbt), kv_out(vbt),
              kit.reshape(1, bsz, IDX_DIM, t).transpose(0, 1, 3, 2), conv_p[None])

    page = cache_moba_k.shape[2]
    n_pages = page_table.shape[1]
    past = n_pages * page
    assert past % MOBA_BLOCK == 0 and db % SUBLANES == 0 and cache_moba_k.shape[0] == 1
    xs2 = x_sample.reshape(db, d)
    qa_s, ka_s, va_s, qb_s, kb_s, vb_s, qi_s, kw_s = _pre_project(
        xs2, mods_s[0], mods_s[1], g_attn_l, wqkv, wkw, nrm, gmat, tm=db, rows_per_group=db, with_kmean=False)
    pool = lambda cch: cch.transpose(0, 1, 3, 4, 2)
    pool_kidx = cache_dsa_kidx.transpose(0, 1, 3, 2)
    row3 = lambda a: a.reshape(db, 1, a.shape[-1])
    scores, score_new, bias_blk = _sample_stats(
        page_table, qi_s.reshape(db, IDX_HEADS, IDX_DIM), kw_s[:, IDX_DIM:IDX_DIM + IDX_HEADS].reshape(db, IDX_HEADS, 1),
        row3(kw_s[:, :IDX_DIM]), row3(qa_s), pool_kidx, pool(cache_moba_k))
    n_sel_s = min(DSA_TOPK, (past + ds) // 4)
    bias_pos, bias_new = _dsa_sample_select(scores.reshape(db, past), score_new.reshape(db, LANES), n_sel_s)
    o_a_s = _paged_attention(page_table, slopes_a, row3(qa_s), row3(ka_s), row3(va_s), bias_blk,
                             jnp.zeros((db, 1, LANES), F32), pool(cache_moba_k), pool(cache_moba_v), block_bias=True)
    o_b_s = _paged_attention(page_table, slopes_b, row3(qb_s), row3(kb_s), row3(vb_s),
                             bias_pos.reshape(db, 1, past), bias_new.reshape(db, 1, LANES),
                             pool(cache_dsa_k), pool(cache_dsa_v), block_bias=False)
    state = state_ffn_conv[l]
    y_s, a_s = _finish(xs2, o_a_s.reshape(db, WIDTH), o_b_s.reshape(db, WIDTH), mods_s[2:], g_ffn_l, wout, wupa, wupg,
                       conv_w[l], cb, wdn, tm=db, rows_per_group=db, prefix=(state[:, 0], state[:, 1]))
    conv_s = jnp.stack([state[:, 1], a_s], axis=1)
    kvs_shape = (1, db, ds, N_HEADS, HEAD_DIM)
    outs_s = (ka_s.reshape(kvs_shape), va_s.reshape(kvs_shape), kb_s.reshape(kvs_shape), vb_s.reshape(kvs_shape),
              kw_s[:, :IDX_DIM].reshape(1, db, ds, IDX_DIM), conv_s[None])
    return (y_p.reshape(bsz, t, d), y_s.reshape(db, ds, d)) + outs_p + outs_s
```

```python
import functools

import jax
import jax.numpy as jnp
from jax import lax
from jax.experimental import pallas as pl
from jax.experimental.pallas import tpu as pltpu

HEAD_DIM = 64
N_HEADS = 8
WIDTH = N_HEADS * HEAD_DIM
MOBA_BLOCK = 256
MOBA_TOPK = 3
DSA_TOPK = 256
IDX_HEADS = 8
IDX_DIM = 64
CONV_WIDTH = 3
EPS = 1e-6
N_MOD = 6
ATTN_SCALE = HEAD_DIM ** -0.5
N_QKV_CHUNKS = 7
LANES = 128
SUBLANES = 8
TQ = MOBA_BLOCK
SB = 2048
SEL_CHUNK = 1024
BPS = SB // TQ
FF_CHUNK = 256
PAGES_PER_STEP = 8
VMEM_LIMIT = 56 * 1024 * 1024

BF = jnp.bfloat16
F32 = jnp.float32
I32 = jnp.int32
NEG_INF = float("-inf")
INT_MIN = -2 ** 31


def _cp(sem):
    return pltpu.CompilerParams(dimension_semantics=sem, vmem_limit_bytes=VMEM_LIMIT)


def _const_spec(shape):
    nd = len(shape)
    return pl.BlockSpec(shape, lambda *_: (0,) * nd)


def _dot_nt(a, b):
    return lax.dot_general(a, b, (((1,), (1,)), ((), ())), preferred_element_type=F32)


def _mod_kernel(c_ref, w_ref, b_ref, o_ref):
    s = jax.nn.silu(c_ref[...])
    o_ref[...] = jnp.dot(s.astype(BF), w_ref[...].astype(BF), preferred_element_type=F32) + b_ref[...]


def _modulation(c_all, w_ada, b_ada):
    r, d = c_all.shape
    n = w_ada.shape[1]
    tn = 1024
    return pl.pallas_call(
        _mod_kernel,
        grid=(n // tn,),
        in_specs=[pl.BlockSpec((r, d), lambda j: (0, 0)),
                  pl.BlockSpec((d, tn), lambda j: (0, j)),
                  pl.BlockSpec((1, tn), lambda j: (0, j))],
        out_specs=pl.BlockSpec((r, tn), lambda j: (0, j)),
        out_shape=jax.ShapeDtypeStruct((r, n), F32),
        compiler_params=_cp(("arbitrary",)),
        name="modulation",
    )(c_all, w_ada, b_ada.reshape(1, n))


def _pre_kernel(x_ref, shift_ref, scale_ref, g_ref, wqkv_ref, wkw_ref, nrm_ref, gmat_ref, *outs, tm, with_kmean):
    x = x_ref[...]
    ms = jnp.mean(x * x, axis=-1, keepdims=True)
    xn = x * lax.rsqrt(ms + EPS) * g_ref[...]
    h = (xn * (1.0 + scale_ref[0]) + shift_ref[0]).astype(BF)
    norm_row = {0: 0, 1: 1, 3: 2, 4: 3}
    for c in range(N_QKV_CHUNKS):
        p = jnp.dot(h, wqkv_ref[:, c * WIDTH:(c + 1) * WIDTH], preferred_element_type=F32)
        if c in norm_row:
            ss = jnp.dot((p * p).astype(BF), gmat_ref[...], preferred_element_type=F32)
            k = norm_row[c]
            p = p * lax.rsqrt(ss * (1.0 / HEAD_DIM) + EPS) * nrm_ref[k:k + 1, :]
        outs[c][...] = p
        if c == 1 and with_kmean:
            outs[N_QKV_CHUNKS + 1][0] = jnp.mean(p.reshape(tm // MOBA_BLOCK, MOBA_BLOCK, WIDTH), axis=1)
    outs[N_QKV_CHUNKS][...] = jnp.dot(h, wkw_ref[...], preferred_element_type=F32)


def _pre_prompt_kernel(x_ref, shift_ref, scale_ref, g_ref, wqkv_ref, wkw_ref, nrm_ref, gmat_ref,
                       qa_ref, qb_ref, qi_ref, kw_ref, kat_ref, vat_ref, kbt_ref, vbt_ref,
                       katb_ref, vatb_ref, kbtb_ref, vbtb_ref, kwtb_ref, kit_ref, km_ref, *, tm):
    x = x_ref[...]
    ms = jnp.mean(x * x, axis=-1, keepdims=True)
    xn = x * lax.rsqrt(ms + EPS) * g_ref[...]
    h = (xn * (1.0 + scale_ref[0]) + shift_ref[0]).astype(BF)

    def proj(c, norm):
        p = jnp.dot(h, wqkv_ref[:, c * WIDTH:(c + 1) * WIDTH], preferred_element_type=F32)
        if norm is not None:
            ss = jnp.dot((p * p).astype(BF), gmat_ref[...], preferred_element_type=F32)
            p = p * lax.rsqrt(ss * (1.0 / HEAD_DIM) + EPS) * nrm_ref[norm:norm + 1, :]
        return p

    qa_ref[...] = (proj(0, 0) * ATTN_SCALE).astype(BF)
    ka = proj(1, 1)
    km_ref[0] = jnp.mean(ka.reshape(tm // MOBA_BLOCK, MOBA_BLOCK, WIDTH), axis=1)
    for p, t_ref, tb_ref in ((ka, kat_ref, katb_ref), (proj(2, None), vat_ref, vatb_ref)):
        pt = p.T
        t_ref[0] = pt
        tb_ref[0] = pt.astype(BF)
    qb_ref[...] = (proj(3, 2) * ATTN_SCALE).astype(BF)
    for p, t_ref, tb_ref in ((proj(4, 3), kbt_ref, kbtb_ref), (proj(5, None), vbt_ref, vbtb_ref)):
        pt = p.T
        t_ref[0] = pt
        tb_ref[0] = pt.astype(BF)
    qi_ref[...] = proj(6, None)
    kw = jnp.dot(h, wkw_ref[...], preferred_element_type=F32)
    kw_ref[...] = kw
    kwt = kw.T
    kwtb_ref[0] = kwt.astype(BF)
    kit_ref[0] = kwt[:IDX_DIM, :]


def _pre_project_prompt(x2, shift, scale, g_attn, wqkv, wkw, nrm, gmat, *, tm, bsz, t):
    n, d = x2.shape
    tpb = t // tm
    mod_spec = pl.BlockSpec((1, 1, d), lambda i: (i // tpb, 0, 0))
    row_spec = lambda w: pl.BlockSpec((tm, w), lambda i: (i, 0))
    tr_spec = lambda r: pl.BlockSpec((1, r, tm), lambda i: (i // tpb, 0, i % tpb))
    nb = tm // MOBA_BLOCK
    sds = jax.ShapeDtypeStruct
    out_shape = ([sds((n, WIDTH), BF)] * 2 + [sds((n, WIDTH), F32), sds((n, LANES), F32)]
                 + [sds((bsz, WIDTH, t), F32)] * 4 + [sds((bsz, WIDTH, t), BF)] * 4
                 + [sds((bsz, LANES, t), BF), sds((bsz, IDX_DIM, t), F32), sds((n // tm, nb, WIDTH), F32)])
    out_specs = ([row_spec(WIDTH)] * 3 + [row_spec(LANES)] + [tr_spec(WIDTH)] * 8
                 + [tr_spec(LANES), tr_spec(IDX_DIM), pl.BlockSpec((1, nb, WIDTH), lambda i: (i, 0, 0))])
    return pl.pallas_call(
        functools.partial(_pre_prompt_kernel, tm=tm),
        grid=(n // tm,),
        in_specs=[row_spec(d), mod_spec, mod_spec, _const_spec((1, d)),
                  _const_spec(wqkv.shape), _const_spec(wkw.shape), _const_spec(nrm.shape), _const_spec(gmat.shape)],
        out_specs=out_specs,
        out_shape=out_shape,
        compiler_params=_cp(("arbitrary",)),
        name="pre_project_prompt",
    )(x2, shift, scale, g_attn, wqkv, wkw, nrm, gmat)


def _pre_project(x2, shift, scale, g_attn, wqkv, wkw, nrm, gmat, *, tm, rows_per_group, with_kmean):
    n, d = x2.shape
    grid = (n // tm,)
    mr = shift.shape[1]
    tiles_per_group = rows_per_group // tm
    mod_spec = pl.BlockSpec((1, mr, d), lambda i: (i // tiles_per_group, 0, 0))
    row_spec = lambda w: pl.BlockSpec((tm, w), lambda i: (i, 0))
    out_shape = [jax.ShapeDtypeStruct((n, WIDTH), F32)] * N_QKV_CHUNKS + [jax.ShapeDtypeStruct((n, LANES), F32)]
    out_specs = [row_spec(WIDTH)] * N_QKV_CHUNKS + [row_spec(LANES)]
    if with_kmean:
        nb = tm // MOBA_BLOCK
        out_shape.append(jax.ShapeDtypeStruct((n // tm, nb, WIDTH), F32))
        out_specs.append(pl.BlockSpec((1, nb, WIDTH), lambda i: (i, 0, 0)))
    return pl.pallas_call(
        functools.partial(_pre_kernel, tm=tm, with_kmean=with_kmean),
        grid=grid,
        in_specs=[row_spec(d), mod_spec, mod_spec, _const_spec((1, d)),
                  _const_spec(wqkv.shape), _const_spec(wkw.shape), _const_spec(nrm.shape), _const_spec(gmat.shape)],
        out_specs=out_specs,
        out_shape=out_shape,
        compiler_params=_cp(("arbitrary",)),
        name="pre_project",
    )(x2, shift, scale, g_attn, wqkv, wkw, nrm, gmat)


def _pair_half(shape, axis):
    return lax.broadcasted_iota(I32, shape, axis) // HEAD_DIM


def _two_pass_attention(q_pair, par, kt_ref, vt_ref, slope, i, bias_at, l_ref, m_ref, acc_ref):
    n_steps = i // BPS + 1
    q = jnp.where(_pair_half((TQ, LANES), 1) == par, q_pair, jnp.zeros_like(q_pair))
    own_rows = _pair_half((LANES, SB), 0) == par
    m_ref[...] = jnp.full((TQ, TQ), NEG_INF, F32)

    def keys(ref, j):
        return ref[0, :, pl.ds(pl.multiple_of(j * SB, SB), SB)]

    def pass1(j, carry):
        cols = lax.broadcasted_iota(I32, (1, SB), 1) + (j * SB - i * TQ)
        l = jnp.dot(q, keys(kt_ref, j), preferred_element_type=F32) + slope * cols.astype(F32) + bias_at(j)
        l_ref[j] = l
        mx = m_ref[...]
        for k in range(BPS):
            mx = jnp.maximum(mx, l[:, k * TQ:(k + 1) * TQ])
        m_ref[...] = mx
        return carry

    lax.fori_loop(0, n_steps, pass1, 0)
    m = jnp.max(m_ref[...], axis=-1, keepdims=True)
    acc_ref[...] = jnp.zeros((TQ, LANES), F32)

    def pass2(j, carry):
        p = jnp.exp((l_ref[j] - m).astype(BF))
        vt = jnp.where(own_rows, keys(vt_ref, j), jnp.ones((LANES, SB), BF))
        acc_ref[...] += _dot_nt(p, vt)
        return carry

    lax.fori_loop(0, n_steps, pass2, 0)
    acc = acc_ref[...]
    return acc / pltpu.roll(acc, HEAD_DIM, axis=1)


def _store_pair_half(o_ref, par, val):
    keep = _pair_half((TQ, LANES), 1) == par

    @pl.when(par == 0)
    def _():
        o_ref[...] = jnp.where(keep, val, 0.0)

    @pl.when(par != 0)
    def _():
        o_ref[...] = jnp.where(keep, val, o_ref[...])


def _moba_prompt_kernel(slopes_ref, q_ref, kt_ref, vt_ref, kmt_ref, o_ref, l_ref, m_ref, acc_ref, bias_ref, sel_ref):
    pr = pl.program_id(1)
    i = pl.program_id(2)
    par = pl.program_id(3)
    sub = i % BPS
    row0 = pl.multiple_of(sub * TQ, TQ)

    @pl.when(sub == 0)
    def _():
        qs = q_ref[...]
        qs = jnp.where(_pair_half(qs.shape, 1) == par, qs, jnp.zeros_like(qs))
        s = jnp.dot(qs, kmt_ref[0], preferred_element_type=F32)
        lane_s = lax.broadcasted_iota(I32, s.shape, 1)
        own = i + lax.broadcasted_iota(I32, s.shape, 0) // TQ
        tri = (lax.broadcasted_iota(I32, (LANES, LANES), 0) <= lax.broadcasted_iota(I32, (LANES, LANES), 1))
        tri = jnp.where(tri, 1.0, 0.0).astype(BF)
        s = jnp.where(lane_s < own, s, NEG_INF)
        bias = jnp.full(s.shape, NEG_INF, F32)
        for _ in range(MOBA_TOPK):
            mx = jnp.max(s, axis=-1, keepdims=True)
            eq = s == mx
            rank = jnp.dot(jnp.where(eq, 1.0, 0.0).astype(BF), tri, preferred_element_type=F32)
            pick = eq & (rank == 1.0) & (mx > NEG_INF)
            bias = jnp.where(pick, 0.0, bias)
            s = jnp.where(pick, NEG_INF, s)
        sel_ref[par] = bias

    q = q_ref[pl.ds(row0, TQ), :]
    bias_ref[...] = sel_ref[par, pl.ds(row0, TQ), :]
    lane = lax.broadcasted_iota(I32, (TQ, LANES), 1)
    r = lax.broadcasted_iota(I32, (TQ, TQ), 0)
    c = lax.broadcasted_iota(I32, (TQ, TQ), 1)
    causal = jnp.where(c <= r, 0.0, NEG_INF).astype(F32)

    def bias_at(j):
        parts = []
        for k in range(BPS):
            blk = j * BPS + k
            col = jnp.max(jnp.where(lane == blk, bias_ref[...], NEG_INF), axis=-1, keepdims=True)
            parts.append(jnp.where(blk == i, causal, jnp.broadcast_to(col, (TQ, TQ))))
        return jnp.concatenate(parts, axis=1)

    out = _two_pass_attention(q, par, kt_ref, vt_ref, slopes_ref[2 * pr + par], i, bias_at, l_ref, m_ref, acc_ref)
    _store_pair_half(o_ref, par, out)


def _moba_prompt(slopes, q, kt, vt, kmt, bsz, t):
    nq = t // TQ
    nsb = t // SB
    npair = N_HEADS // 2
    grid_spec = pltpu.PrefetchScalarGridSpec(
        num_scalar_prefetch=1,
        grid=(bsz, npair, nq, 2),
        in_specs=[pl.BlockSpec((SB, LANES), lambda bi, pr, i, par, s: (bi * nsb + i // BPS, pr)),
                  pl.BlockSpec((1, LANES, t), lambda bi, pr, i, par, s: (bi, pr, 0)),
                  pl.BlockSpec((1, LANES, t), lambda bi, pr, i, par, s: (bi, pr, 0)),
                  pl.BlockSpec((1, LANES, LANES), lambda bi, pr, i, par, s: (bi, pr, 0))],
        out_specs=pl.BlockSpec((TQ, LANES), lambda bi, pr, i, par, s: (bi * nq + i, pr)),
        scratch_shapes=[pltpu.VMEM((nsb, TQ, SB), F32), pltpu.VMEM((TQ, TQ), F32),
                        pltpu.VMEM((TQ, LANES), F32), pltpu.VMEM((TQ, LANES), F32), pltpu.VMEM((2, SB, LANES), F32)],
    )
    return pl.pallas_call(
        _moba_prompt_kernel,
        grid_spec=grid_spec,
        out_shape=jax.ShapeDtypeStruct((bsz * t, WIDTH), F32),
        compiler_params=_cp(("arbitrary",) * 4),
        name="moba_prompt",
    )(slopes, q, kt, vt, kmt)


def _threshold_from_code(code):
    bits = code ^ ((code >> 31) & 0x7FFFFFFF)
    return lax.bitcast_convert_type(bits, F32)


def _dsa_select(i, n_sel, idx_bits, sk_ref, b_ref, jc_ref):
    n_chunks = (i * TQ) // SEL_CHUNK + 1
    per_step = SB // SEL_CHUNK
    rc = TQ
    c = lax.broadcasted_iota(I32, (rc, SEL_CHUNK), 1)

    def chunk(ref, j, rows):
        return ref.at[j // per_step, rows, pl.ds(pl.multiple_of((j % per_step) * SEL_CHUNK, SEL_CHUNK), SEL_CHUNK)]

    for r0 in range(0, TQ, rc):
        rows = slice(r0, r0 + rc)
        gr = lax.broadcasted_iota(I32, (rc, SEL_CHUNK), 0) + (i * TQ + r0)

        def count(pred, rows=rows):
            def body(j, acc):
                hit = jnp.where(pred(j, chunk(sk_ref, j, rows)[...]), 1.0, 0.0)
                for k in range(SEL_CHUNK // LANES):
                    acc = acc + hit[:, k * LANES:(k + 1) * LANES]
                return acc

            acc = lax.fori_loop(0, n_chunks, body, jnp.zeros((rc, LANES), F32))
            return jnp.sum(acc, axis=-1, keepdims=True)

        def bit_step(it, carry, count=count):
            v, cnt_v = carry
            cand = v + jnp.left_shift(jnp.int32(1), 31 - it)
            cand_f = _threshold_from_code(cand)
            cnt = count(lambda j, sc: sc >= cand_f)
            keep = cnt >= n_sel
            return jnp.where(keep, cand, v), jnp.where(keep, cnt, cnt_v)

        v, cnt_ge = lax.fori_loop(0, 32, bit_step, (jnp.full((rc, 1), INT_MIN, I32), jnp.zeros((rc, 1), F32)))
        few = v == INT_MIN
        thr = jnp.where(few, NEG_INF, _threshold_from_code(v))
        cnt_gt = count(lambda j, sc: sc > thr)
        need = n_sel - cnt_gt
        trim = (cnt_ge > n_sel) & jnp.logical_not(few)
        jc_ref[rows, :] = jnp.full((rc, LANES), 2 ** idx_bits, I32)

        @pl.when(jnp.max(jnp.where(trim, 1.0, 0.0)) > 0.0)
        def _(rows=rows, count=count, thr=thr, need=need, trim=trim):
            def idx_step(it, jcap):
                cand = jcap + jnp.left_shift(jnp.int32(1), idx_bits - 1 - it)
                cnt = count(lambda j, sc: (sc == thr) & (c + j * SEL_CHUNK < cand))
                return jnp.where(cnt < need, cand, jcap)

            jcap = lax.fori_loop(0, idx_bits, idx_step, jnp.zeros((rc, 1), I32))
            jc_ref[rows, :] = jnp.broadcast_to(jnp.where(trim, jcap, 2 ** idx_bits), (rc, LANES))

        jcap = jc_ref[rows, 0:1]

        def write(j, carry, rows=rows, thr=thr, jcap=jcap, gr=gr):
            sc = chunk(sk_ref, j, rows)[...]
            gc = c + j * SEL_CHUNK
            sel = ((sc > thr) | ((sc == thr) & (gc <= jcap))) & (gc <= gr)
            chunk(b_ref, j, rows)[...] = jnp.where(sel, 0.0, NEG_INF)
            return carry

        lax.fori_loop(0, (i // BPS + 1) * per_step, write, 0)


def _dsa_prompt_kernel(slopes_ref, qi_ref, kw_ref, kwt_ref, q_ref, kt_ref, vt_ref, o_ref,
                       sk_ref, b_ref, l_ref, m_ref, acc_ref, jc_ref, qz_ref, *, n_sel, idx_bits):
    i = pl.program_id(1)
    h = pl.program_id(2)
    par = h % 2

    @pl.when(h == 0)
    def _():
        wi = kw_ref[...] * (IDX_HEADS * IDX_DIM) ** -0.5
        wcols = [wi[:, IDX_DIM + hh:IDX_DIM + hh + 1] for hh in range(IDX_HEADS)]
        gr = lax.broadcasted_iota(I32, (TQ, SB), 0) + i * TQ
        c = lax.broadcasted_iota(I32, (TQ, SB), 1)
        first = _pair_half((TQ, LANES), 1) == 0
        for hh in range(IDX_HEADS):
            qp = qi_ref[:, (hh // 2) * LANES:(hh // 2 + 1) * LANES]
            if hh % 2:
                qp = pltpu.roll(qp, HEAD_DIM, axis=1)
            qz_ref[hh] = jnp.where(first, qp, 0.0).astype(BF)

        def score_step(j, carry):
            kit = kwt_ref[0, :, pl.ds(pl.multiple_of(j * SB, SB), SB)]
            sc = jnp.maximum(jnp.dot(qz_ref[0], kit, preferred_element_type=F32), 0.0) * wcols[0]
            for hh in range(1, IDX_HEADS):
                sc = sc + jnp.maximum(jnp.dot(qz_ref[hh], kit, preferred_element_type=F32), 0.0) * wcols[hh]
            sk_ref[j] = jnp.where(c + j * SB <= gr, sc, NEG_INF)
            return carry

        lax.fori_loop(0, i // BPS + 1, score_step, 0)
        _dsa_select(i, n_sel, idx_bits, sk_ref, b_ref, jc_ref)

    out = _two_pass_attention(q_ref[...], par, kt_ref, vt_ref, slopes_ref[h], i, lambda j: b_ref[j],
                              l_ref, m_ref, acc_ref)
    _store_pair_half(o_ref, par, out)


def _dsa_prompt(slopes, qi, kw, kwt, q, kt, vt, n_sel, bsz, t):
    nq = t // TQ
    nsb = t // SB
    idx_bits = max(1, (t - 1).bit_length())
    blk = lambda shape, fn: pl.BlockSpec(shape, fn)
    grid_spec = pltpu.PrefetchScalarGridSpec(
        num_scalar_prefetch=1,
        grid=(bsz, nq, N_HEADS),
        in_specs=[blk((TQ, WIDTH), lambda bi, i, hi, s: (bi * nq + i, 0)),
                  blk((TQ, LANES), lambda bi, i, hi, s: (bi * nq + i, 0)),
                  blk((1, LANES, t), lambda bi, i, hi, s: (bi, 0, 0)),
                  blk((TQ, LANES), lambda bi, i, hi, s: (bi * nq + i, hi // 2)),
                  blk((1, LANES, t), lambda bi, i, hi, s: (bi, hi // 2, 0)),
                  blk((1, LANES, t), lambda bi, i, hi, s: (bi, hi // 2, 0))],
        out_specs=blk((TQ, LANES), lambda bi, i, hi, s: (bi * nq + i, hi // 2)),
        scratch_shapes=[pltpu.VMEM((nsb, TQ, SB), F32), pltpu.VMEM((nsb, TQ, SB), F32), pltpu.VMEM((nsb, TQ, SB), F32),
                        pltpu.VMEM((TQ, TQ), F32), pltpu.VMEM((TQ, LANES), F32), pltpu.VMEM((TQ, LANES), I32),
                        pltpu.VMEM((IDX_HEADS, TQ, LANES), BF)],
    )
    return pl.pallas_call(
        functools.partial(_dsa_prompt_kernel, n_sel=n_sel, idx_bits=idx_bits),
        grid_spec=grid_spec,
        out_shape=jax.ShapeDtypeStruct((bsz * t, WIDTH), F32),
        compiler_params=_cp(("arbitrary", "arbitrary", "arbitrary")),
        name="dsa_prompt",
    )(slopes, qi, kw, kwt, q, kt, vt)


def _finish_kernel(*refs, tm, nchunk, seq_mode, tiles_per_seq):
    (x_ref, oa_ref, ob_ref, gate_a_ref, shift_f_ref, scale_f_ref, gate_f_ref, g_ref, wout_ref,
     wupa_ref, wupg_ref, cw_ref, cb_ref, wdn_ref) = refs[:14]
    if seq_mode:
        y_ref, st_ref, abuf_ref, carry_ref, yacc_ref = refs[14:]
    else:
        pre0_ref, pre1_ref, y_ref, st_ref, yacc_ref = refs[14:]
    i = pl.program_id(0)
    o = (jnp.dot(oa_ref[...].astype(BF), wout_ref[0:WIDTH, :], preferred_element_type=F32)
         + jnp.dot(ob_ref[...].astype(BF), wout_ref[WIDTH:2 * WIDTH, :], preferred_element_type=F32))
    x1 = x_ref[...] + gate_a_ref[0] * o
    ms = jnp.mean(x1 * x1, axis=-1, keepdims=True)
    h = ((x1 * lax.rsqrt(ms + EPS) * g_ref[...]) * (1.0 + scale_f_ref[0]) + shift_f_ref[0]).astype(BF)
    yacc_ref[...] = jnp.zeros_like(yacc_ref)
    for c in range(nchunk):
        cols = slice(c * FF_CHUNK, (c + 1) * FF_CHUNK)
        a = jnp.dot(h, wupa_ref[c], preferred_element_type=F32)
        gt = jnp.dot(h, wupg_ref[c], preferred_element_type=F32)
        if seq_mode:
            first = (i % tiles_per_seq) == 0
            buf = abuf_ref.at[c % 2]
            buf[0:SUBLANES, :] = jnp.where(first, 0.0, carry_ref[c])
            buf[SUBLANES:SUBLANES + tm, :] = a
            a1 = buf[SUBLANES - 1:SUBLANES - 1 + tm, :]
            a2 = buf[SUBLANES - 2:SUBLANES - 2 + tm, :]
            carry_ref[c] = a[tm - SUBLANES:tm, :]
            st_ref[0, :, cols] = a[tm - (CONV_WIDTH - 1):tm, :]
        else:
            a2 = pre0_ref[:, cols]
            a1 = pre1_ref[:, cols]
            st_ref[:, cols] = a
        conv = cb_ref[:, cols] + a2 * cw_ref[0:1, cols]
        conv = conv + a1 * cw_ref[1:2, cols]
        conv = conv + a * cw_ref[2:3, cols]
        act = (jax.nn.gelu(conv) * gt).astype(BF)
        yacc_ref[...] += jnp.dot(act, wdn_ref[c], preferred_element_type=F32)
    y_ref[...] = x1 + gate_f_ref[0] * yacc_ref[...]


def _finish(x2, oa, ob, mods, g_ffn, wout, wupa, wupg, conv_w, conv_b, wdn, *, tm, rows_per_group, prefix=None):
    n, d = x2.shape
    nchunk = wupa.shape[0]
    f = nchunk * FF_CHUNK
    seq_mode = prefix is None
    tiles_per_seq = rows_per_group // tm
    mr = mods[0].shape[1]
    mod_spec = pl.BlockSpec((1, mr, d), lambda i: (i // tiles_per_seq, 0, 0))
    row_spec = lambda w: pl.BlockSpec((tm, w), lambda i: (i, 0))
    in_specs = ([row_spec(d), row_spec(WIDTH), row_spec(WIDTH)] + [mod_spec] * 4 +
                [_const_spec((1, d)), _const_spec(wout.shape), _const_spec(wupa.shape), _const_spec(wupg.shape),
                 _const_spec(conv_w.shape), _const_spec((1, f)), _const_spec(wdn.shape)])
    args = [x2, oa, ob, *mods, g_ffn, wout, wupa, wupg, conv_w, conv_b, wdn]
    scratch = []
    if seq_mode:
        nseq = n // rows_per_group
        st_shape = jax.ShapeDtypeStruct((nseq, CONV_WIDTH - 1, f), F32)
        st_spec = pl.BlockSpec((1, CONV_WIDTH - 1, f), lambda i: (i // tiles_per_seq, 0, 0))
        scratch += [pltpu.VMEM((2, SUBLANES + tm, FF_CHUNK), F32), pltpu.VMEM((nchunk, SUBLANES, FF_CHUNK), F32)]
    else:
        in_specs += [row_spec(f), row_spec(f)]
        args += list(prefix)
        st_shape = jax.ShapeDtypeStruct((n, f), F32)
        st_spec = row_spec(f)
    scratch.append(pltpu.VMEM((tm, d), F32))
    return pl.pallas_call(
        functools.partial(_finish_kernel, tm=tm, nchunk=nchunk, seq_mode=seq_mode, tiles_per_seq=tiles_per_seq),
        grid=(n // tm,),
        in_specs=in_specs,
        out_specs=[row_spec(d), st_spec],
        out_shape=[jax.ShapeDtypeStruct((n, d), F32), st_shape],
        scratch_shapes=scratch,
        compiler_params=_cp(("arbitrary",)),
        name="finish",
    )(*args)


def _block_diag_rows(row):
    hrow = lax.broadcasted_iota(I32, (N_HEADS, WIDTH), 0)
    hlane = lax.broadcasted_iota(I32, (N_HEADS, WIDTH), 1) // HEAD_DIM
    return jnp.where(hrow == hlane, jnp.broadcast_to(row, (N_HEADS, WIDTH)), 0.0)


def _sample_stats_kernel(pt_ref, qi_ref, wi_ref, kin_ref, qa_ref, *refs, page, nblk):
    pps = PAGES_PER_STEP
    kidx_refs = refs[:pps]
    kmoba_refs = refs[pps:2 * pps]
    sc_ref, scn_ref, bb_ref, lg_ref, km_ref = refs[2 * pps:]
    s = pl.program_id(1)
    qbd = _block_diag_rows(qa_ref[0] * ATTN_SCALE).astype(BF)
    qi = qi_ref[0].astype(BF)
    wi = wi_ref[0] * (IDX_HEADS * IDX_DIM) ** -0.5
    ppb = MOBA_BLOCK // page
    lane = lax.broadcasted_iota(I32, (WIDTH, LANES), 1)

    @pl.when(s == 0)
    def _():
        km_ref[...] = jnp.zeros(km_ref.shape, F32)

    kidx = jnp.concatenate([kidx_refs[k][0, 0].astype(BF) for k in range(pps)], axis=1)
    rel = jnp.maximum(jnp.dot(qi, kidx, preferred_element_type=F32), 0.0)
    sc_ref[0] = jnp.sum(rel * wi, axis=0, keepdims=True)
    for kb in range(pps // ppb):
        tot = kmoba_refs[kb * ppb][0, 0].reshape(WIDTH, page)
        for k in range(1, ppb):
            tot = tot + kmoba_refs[kb * ppb + k][0, 0].reshape(WIDTH, page)
        mean = jnp.sum(tot, axis=-1, keepdims=True) * (1.0 / MOBA_BLOCK)
        km_ref[...] = jnp.where(lane == s * (pps // ppb) + kb, mean, km_ref[...])
    kt = jnp.concatenate([r[0, 0].reshape(WIDTH, page).astype(BF) for r in kmoba_refs], axis=1)
    lg_ref[0] = jnp.dot(qbd, kt, preferred_element_type=F32)

    @pl.when(s == pl.num_programs(1) - 1)
    def _():
        kin = jnp.broadcast_to(kin_ref[0], (SUBLANES, IDX_DIM)).astype(BF)
        reln = jnp.maximum(_dot_nt(qi, kin), 0.0)
        scn = jnp.sum(reln[:, 0:1] * wi, axis=0, keepdims=True)
        scn_ref[0] = jnp.broadcast_to(scn, (1, LANES))
        sb = jnp.dot(qbd, km_ref[...].astype(BF), preferred_element_type=F32)
        lane_b = lax.broadcasted_iota(I32, sb.shape, 1)
        lanef = lane_b.astype(F32)
        sb = jnp.where(lane_b < nblk, sb, NEG_INF)
        bias = jnp.full(sb.shape, NEG_INF, F32)
        for _ in range(MOBA_TOPK):
            mx = jnp.max(sb, axis=-1, keepdims=True)
            first = jnp.min(jnp.where(sb == mx, lanef, float(LANES)), axis=-1, keepdims=True)
            pick = (lanef == first) & (mx > NEG_INF)
            bias = jnp.where(pick, 0.0, bias)
            sb = jnp.where(pick, NEG_INF, sb)
        bb_ref[0] = bias


def _sample_stats(page_table, qi_s, wi_s, ki_s, qa_s, pool_kidx, pool_mk):
    db, n_pages = page_table.shape
    page = pool_kidx.shape[3]
    pps = PAGES_PER_STEP
    past = n_pages * page
    nblk = past // MOBA_BLOCK
    assert n_pages % pps == 0 and pps % (MOBA_BLOCK // page) == 0 and nblk <= LANES and page == LANES
    kidx_spec = lambda k: pl.BlockSpec((1, 1, IDX_DIM, page), lambda b, s, pt: (0, pt[b, s * pps + k], 0, 0))
    kmoba_spec = lambda k: pl.BlockSpec((1, 1, N_HEADS, HEAD_DIM, page),
                                        lambda b, s, pt: (0, pt[b, s * pps + k], 0, 0, 0))
    per_seq = lambda shape: pl.BlockSpec((1,) + shape, lambda b, s, pt: (b, 0, 0))
    grid_spec = pltpu.PrefetchScalarGridSpec(
        num_scalar_prefetch=1,
        grid=(db, n_pages // pps),
        in_specs=([per_seq((IDX_HEADS, IDX_DIM)), per_seq((IDX_HEADS, 1)), per_seq((1, IDX_DIM)), per_seq((1, WIDTH))]
                  + [kidx_spec(k) for k in range(pps)] + [kmoba_spec(k) for k in range(pps)]),
        out_specs=[pl.BlockSpec((1, 1, pps * page), lambda b, s, pt: (b, 0, s)),
                   per_seq((1, LANES)), per_seq((N_HEADS, LANES)),
                   pl.BlockSpec((1, N_HEADS, pps * page), lambda b, s, pt: (b, 0, s))],
        scratch_shapes=[pltpu.VMEM((WIDTH, LANES), F32)],
    )
    return pl.pallas_call(
        functools.partial(_sample_stats_kernel, page=page, nblk=nblk),
        grid_spec=grid_spec,
        out_shape=[jax.ShapeDtypeStruct((db, 1, past), F32), jax.ShapeDtypeStruct((db, 1, LANES), F32),
                   jax.ShapeDtypeStruct((db, N_HEADS, LANES), F32), jax.ShapeDtypeStruct((db, N_HEADS, past), F32)],
        compiler_params=_cp(("arbitrary", "arbitrary")),
        name="sample_stats",
    )(page_table, qi_s, wi_s, ki_s, qa_s, *([pool_kidx] * pps), *([pool_mk] * pps))


def _dsa_sample_select_kernel(sc_ref, scn_ref, b_ref, bn_ref, *, n_sel, idx_bits):
    sc = sc_ref[...]
    scn = scn_ref[:, 0:1]
    db, past = sc.shape
    gidx = lax.broadcasted_iota(I32, sc.shape, 1)

    def count(pred, pred_new):
        return (jnp.sum(jnp.where(pred, 1.0, 0.0), axis=-1, keepdims=True) + jnp.where(pred_new, 1.0, 0.0))

    def bit_step(it, v):
        cand = v + jnp.left_shift(jnp.int32(1), 31 - it)
        cand_f = _threshold_from_code(cand)
        cnt = count(sc >= cand_f, scn >= cand_f)
        return jnp.where(cnt >= n_sel, cand, v)

    v = lax.fori_loop(0, 32, bit_step, jnp.full((db, 1), INT_MIN, I32))
    few = v == INT_MIN
    thr = jnp.where(few, NEG_INF, _threshold_from_code(v))
    cnt_gt = count(sc > thr, scn > thr)
    cnt_ge = count(sc >= thr, scn >= thr)
    need = n_sel - cnt_gt
    trim = (cnt_ge > n_sel) & jnp.logical_not(few)

    def idx_step(it, jcap):
        cand = jcap + jnp.left_shift(jnp.int32(1), idx_bits - 1 - it)
        cnt = count((sc == thr) & (gidx < cand), (scn == thr) & (past < cand))
        return jnp.where(cnt < need, cand, jcap)

    jcap = lax.fori_loop(0, idx_bits, idx_step, jnp.zeros((db, 1), I32))
    jcap = jnp.where(trim, jcap, 2 ** idx_bits)
    sel = (sc > thr) | ((sc == thr) & (gidx <= jcap))
    seln = (scn > thr) | ((scn == thr) & (past <= jcap))
    bn_ref[...] = jnp.broadcast_to(jnp.where(seln, 0.0, NEG_INF), bn_ref.shape)
    b_ref[...] = jnp.where(sel, 0.0, NEG_INF)


def _dsa_sample_select(scores, score_new, n_sel):
    db, past = scores.shape
    idx_bits = past.bit_length()
    return pl.pallas_call(
        functools.partial(_dsa_sample_select_kernel, n_sel=n_sel, idx_bits=idx_bits),
        out_shape=[jax.ShapeDtypeStruct((db, past), F32), jax.ShapeDtypeStruct((db, LANES), F32)],
        compiler_params=pltpu.CompilerParams(vmem_limit_bytes=VMEM_LIMIT),
        name="dsa_sample_select",
    )(scores, score_new)


def _paged_attention_kernel(pt_ref, slopes_ref, q_ref, kn_ref, vn_ref, bias_ref, bn_ref, *refs, page):
    pps = PAGES_PER_STEP
    k_refs = refs[:pps]
    v_refs = refs[pps:2 * pps]
    o_ref, l_ref, mx_ref, m_ref, ps_ref, acc_ref, sl_ref, qbd_ref = refs[2 * pps:]
    s = pl.program_id(1)
    ns = pl.num_programs(1) // 2
    past = ns * pps * page
    hrow = lax.broadcasted_iota(I32, (N_HEADS, LANES), 0)
    pages = lambda prefs: jnp.concatenate([r[0, 0].reshape(WIDTH, page).astype(BF) for r in prefs], axis=1)

    @pl.when(s == 0)
    def _():
        sl = jnp.zeros((N_HEADS, LANES), F32)
        for h in range(N_HEADS):
            sl = jnp.where(hrow == h, slopes_ref[h], sl)
        sl_ref[...] = sl
        mx_ref[...] = jnp.full(mx_ref.shape, NEG_INF, F32)
        qbd_ref[...] = _block_diag_rows(q_ref[0] * ATTN_SCALE).astype(BF)

    @pl.when(s < ns)
    def _():
        wide = pps * page
        pos = s * wide + lax.broadcasted_iota(I32, (1, wide), 1)
        logits = jnp.dot(qbd_ref[...], pages(k_refs), preferred_element_type=F32)
        logits = logits - sl_ref[:, 0:1] * (past - pos).astype(F32)
        logits = logits + bias_ref[0]
        l_ref[s] = logits
        mx_ref[...] = jnp.maximum(mx_ref[...], logits)

    @pl.when(s == ns - 1)
    def _():
        kn = kn_ref[0].astype(BF).astype(F32)
        ln = jnp.sum(qbd_ref[...].astype(F32) * kn, axis=-1, keepdims=True) + bn_ref[0, :, 0:1]
        m = jnp.maximum(jnp.max(mx_ref[...], axis=-1, keepdims=True), ln)
        m_ref[...] = jnp.broadcast_to(m, m_ref.shape)
        pn = jnp.exp(ln - m)
        ps_ref[...] = jnp.zeros(ps_ref.shape, F32)
        ps_ref[:, 0:1] = pn
        acc_ref[...] = pn.astype(BF).astype(F32) * vn_ref[0].astype(BF).astype(F32)

    @pl.when(s >= ns)
    def _():
        p = jnp.exp(l_ref[s - ns] - m_ref[:, 0:1])
        ps_ref[...] += p
        acc_ref[...] += _dot_nt(p.astype(BF), pages(v_refs))

    @pl.when(s == 2 * ns - 1)
    def _():
        o = acc_ref[...] / jnp.sum(ps_ref[...], axis=-1, keepdims=True)
        hr = lax.broadcasted_iota(I32, (N_HEADS, WIDTH), 0)
        hl = lax.broadcasted_iota(I32, (N_HEADS, WIDTH), 1) // HEAD_DIM
        o_ref[0] = jnp.sum(jnp.where(hr == hl, o, 0.0), axis=0, keepdims=True)


def _paged_attention(page_table, slopes, q_s, k_new, v_new, bias, bias_new, pool_k, pool_v):
    db, n_pages = page_table.shape
    page = pool_k.shape[4]
    pps = PAGES_PER_STEP
    assert n_pages % pps == 0 and page == LANES
    ns = n_pages // pps
    pool_blk = (1, 1, N_HEADS, HEAD_DIM, page)
    k_spec = lambda k: pl.BlockSpec(pool_blk, lambda b, s, pt, sl: (0, pt[b, jnp.minimum(s, ns - 1) * pps + k], 0, 0, 0))
    v_spec = lambda k: pl.BlockSpec(pool_blk, lambda b, s, pt, sl: (0, pt[b, jnp.maximum(s - ns, 0) * pps + k], 0, 0, 0))
    per_seq = lambda shape: pl.BlockSpec((1,) + shape, lambda b, s, pt, sl: (b, 0, 0))
    bias_spec = pl.BlockSpec((1, 1, pps * page), lambda b, s, pt, sl: (b, 0, jnp.minimum(s, ns - 1)))
    grid_spec = pltpu.PrefetchScalarGridSpec(
        num_scalar_prefetch=2,
        grid=(db, 2 * ns),
        in_specs=([per_seq((1, WIDTH))] * 3 + [bias_spec, per_seq((1, LANES))]
                  + [k_spec(k) for k in range(pps)] + [v_spec(k) for k in range(pps)]),
        out_specs=per_seq((1, WIDTH)),
        scratch_shapes=[pltpu.VMEM((ns, N_HEADS, pps * page), F32), pltpu.VMEM((N_HEADS, pps * page), F32),
                        pltpu.VMEM((N_HEADS, LANES), F32), pltpu.VMEM((N_HEADS, pps * page), F32),
                        pltpu.VMEM((N_HEADS, WIDTH), F32), pltpu.VMEM((N_HEADS, LANES), F32),
                        pltpu.VMEM((N_HEADS, WIDTH), BF)],
    )
    return pl.pallas_call(
        functools.partial(_paged_attention_kernel, page=page),
        grid_spec=grid_spec,
        out_shape=jax.ShapeDtypeStruct((db, 1, WIDTH), F32),
        compiler_params=_cp(("arbitrary", "arbitrary")),
        name="paged_attention",
    )(page_table, slopes, q_s, k_new, v_new, bias, bias_new, *([pool_k] * pps), *([pool_v] * pps))


def _moba_sample_kernel(pt_ref, slopes_ref, lg_ref, q_ref, kn_ref, vn_ref, bias_ref, *refs, page):
    pps = PAGES_PER_STEP
    v_refs = refs[:pps]
    o_ref, p_ref, ps_ref, acc_ref = refs[pps:]
    s = pl.program_id(1)
    ns = pl.num_programs(1) - 1
    wide = pps * page
    past = ns * wide
    ppb = MOBA_BLOCK // page

    @pl.when(s == 0)
    def _():
        hrow = lax.broadcasted_iota(I32, (N_HEADS, 1), 0)
        sl = jnp.zeros((N_HEADS, 1), F32)
        for h in range(N_HEADS):
            sl = jnp.where(hrow == h, slopes_ref[h], sl)
        qbd = _block_diag_rows(q_ref[0] * ATTN_SCALE).astype(BF)
        kn = kn_ref[0].astype(BF).astype(F32)
        ln = jnp.sum(qbd.astype(F32) * kn, axis=-1, keepdims=True)
        bias = bias_ref[0]
        mx = ln
        for k in range(ns):
            pos = k * wide + lax.broadcasted_iota(I32, (1, wide), 1)
            cols = [jnp.broadcast_to(bias[:, (k * pps + kk) // ppb:(k * pps + kk) // ppb + 1], (N_HEADS, page))
                    for kk in range(pps)]
            l = lg_ref[0, :, k * wide:(k + 1) * wide] - sl * (past - pos).astype(F32) + jnp.concatenate(cols, axis=1)
            p_ref[k] = l
            mx = jnp.maximum(mx, jnp.max(l, axis=-1, keepdims=True))
        pn = jnp.exp(ln - mx)
        tot = pn
        for k in range(ns):
            p = jnp.exp(p_ref[k] - mx)
            p_ref[k] = p
            tot = tot + jnp.sum(p, axis=-1, keepdims=True)
        ps_ref[...] = jnp.broadcast_to(tot, ps_ref.shape)
        acc_ref[...] = pn.astype(BF).astype(F32) * vn_ref[0].astype(BF).astype(F32)

    @pl.when(s > 0)
    def _():
        vt = jnp.concatenate([r[0, 0].reshape(WIDTH, page).astype(BF) for r in v_refs], axis=1)
        acc_ref[...] += _dot_nt(p_ref[s - 1].astype(BF), vt)

    @pl.when(s == ns)
    def _():
        o = acc_ref[...] / ps_ref[:, 0:1]
        hr = lax.broadcasted_iota(I32, (N_HEADS, WIDTH), 0)
        hl = lax.broadcasted_iota(I32, (N_HEADS, WIDTH), 1) // HEAD_DIM
        o_ref[0] = jnp.sum(jnp.where(hr == hl, o, 0.0), axis=0, keepdims=True)


def _moba_sample_attention(page_table, slopes, logits, q_s, k_new, v_new, bias_blk, pool_v):
    db, n_pages = page_table.shape
    page = pool_v.shape[4]
    pps = PAGES_PER_STEP
    assert n_pages % pps == 0 and page == LANES
    ns = n_pages // pps
    past = n_pages * page
    v_spec = lambda k: pl.BlockSpec((1, 1, N_HEADS, HEAD_DIM, page),
                                    lambda b, s, pt, sl: (0, pt[b, jnp.maximum(s - 1, 0) * pps + k], 0, 0, 0))
    per_seq = lambda shape: pl.BlockSpec((1,) + shape, lambda b, s, pt, sl: (b, 0, 0))
    grid_spec = pltpu.PrefetchScalarGridSpec(
        num_scalar_prefetch=2,
        grid=(db, ns + 1),
        in_specs=([per_seq((N_HEADS, past))] + [per_seq((1, WIDTH))] * 3 + [per_seq((N_HEADS, LANES))]
                  + [v_spec(k) for k in range(pps)]),
        out_specs=per_seq((1, WIDTH)),
        scratch_shapes=[pltpu.VMEM((ns, N_HEADS, pps * page), F32), pltpu.VMEM((N_HEADS, LANES), F32),
                        pltpu.VMEM((N_HEADS, WIDTH), F32)],
    )
    return pl.pallas_call(
        functools.partial(_moba_sample_kernel, page=page),
        grid_spec=grid_spec,
        out_shape=jax.ShapeDtypeStruct((db, 1, WIDTH), F32),
        compiler_params=_cp(("arbitrary", "arbitrary")),
        name="moba_sample_attention",
    )(page_table, slopes, logits, q_s, k_new, v_new, bias_blk, *([pool_v] * pps))


def _alibi_slopes():
    n = 2 * N_HEADS
    i = jnp.arange(1, n + 1, dtype=F32)
    m = jnp.exp2(-8.0 * i / n)
    return m[0::2], m[1::2]


def kernel(x_prompt, x_sample, cache_moba_k, cache_moba_v, cache_dsa_k, cache_dsa_v, cache_dsa_kidx, state_ffn_conv, page_table, c_prompt, c_sample, w_ada, b_ada, g_attn, w_in, qn_a, kn_a, qn_b, kn_b, w_out, g_ffn, w_up, conv_w, conv_b, w_down):
    bsz, t, d = x_prompt.shape
    db, ds, _ = x_sample.shape
    depth = w_in.shape[0]
    assert depth == 1 and ds == 1 and t % SB == 0
    f = w_down.shape[1]
    nchunk = f // FF_CHUNK
    assert nchunk * FF_CHUNK == f
    slopes_a, slopes_b = _alibi_slopes()
    l = 0

    ncols = N_QKV_CHUNKS * WIDTH
    wqkv = w_in[l][:, :ncols].astype(BF)
    wkw = jnp.pad(w_in[l][:, ncols:], ((0, 0), (0, LANES - (w_in.shape[2] - ncols)))).astype(BF)
    nrm = jnp.stack([jnp.tile(g[l], N_HEADS) for g in (qn_a, kn_a, qn_b, kn_b)])
    hid = jnp.arange(WIDTH) // HEAD_DIM
    gmat = (hid[:, None] == hid[None, :]).astype(BF)
    wout = w_out[l].astype(BF)
    wupa = w_up[l][:, :f].astype(BF).reshape(d, nchunk, FF_CHUNK).transpose(1, 0, 2)
    wupg = w_up[l][:, f:].astype(BF).reshape(d, nchunk, FF_CHUNK).transpose(1, 0, 2)
    wdn = w_down[l].astype(BF).reshape(nchunk, FF_CHUNK, d)
    g_attn_l = g_attn[l].reshape(1, d)
    g_ffn_l = g_ffn[l].reshape(1, d)
    cb = conv_b[l].reshape(1, f)

    rc = bsz + db
    rpad = -rc % SUBLANES
    c_all = jnp.concatenate([c_prompt, c_sample, jnp.zeros((rpad, d), F32)], axis=0)
    mod = _modulation(c_all, w_ada[l], b_ada[l])
    mods_p = [m.reshape(bsz, 1, d) for m in jnp.split(mod[:bsz], N_MOD, axis=-1)]
    mods_s = [m.reshape(1, db, d) for m in jnp.split(mod[bsz:rc], N_MOD, axis=-1)]

    n = bsz * t
    xp2 = x_prompt.reshape(n, d)
    tm = 512
    (qa_bf, qb_bf, qi, kw, kat, vat, kbt, vbt, kat_bf, vat_bf, kbt_bf, vbt_bf, kwt_bf, kit, kmean) = _pre_project_prompt(
        xp2, mods_p[0], mods_p[1], g_attn_l, wqkv, wkw, nrm, gmat, tm=tm, bsz=bsz, t=t)
    nb = t // TQ
    assert nb <= LANES
    kmt = kmean.reshape(bsz, nb, WIDTH).transpose(0, 2, 1).astype(BF)
    kmt = jnp.pad(kmt, ((0, 0), (0, 0), (0, LANES - nb)))
    o_a = _moba_prompt(slopes_a, qa_bf, kat_bf, vat_bf, kmt, bsz, t)
    n_sel = min(DSA_TOPK, t // 4)
    o_b = _dsa_prompt(slopes_b, qi, kw, kwt_bf, qb_bf, kbt_bf, vbt_bf, n_sel, bsz, t)
    y_p, conv_p = _finish(xp2, o_a, o_b, mods_p[2:], g_ffn_l, wout, wupa, wupg, conv_w[l], cb, wdn,
                          tm=tm, rows_per_group=t)

    kv_out = lambda a: a.reshape(1, bsz, N_HEADS, HEAD_DIM, t).transpose(0, 1, 4, 2, 3)
    outs_p = (kv_out(kat), kv_out(vat), kv_out(kbt), kv_out(vbt),
              kit.reshape(1, bsz, IDX_DIM, t).transpose(0, 1, 3, 2), conv_p[None])

    page = cache_moba_k.shape[2]
    n_pages = page_table.shape[1]
    past = n_pages * page
    assert past % MOBA_BLOCK == 0 and db % SUBLANES == 0 and cache_moba_k.shape[0] == 1
    xs2 = x_sample.reshape(db, d)
    qa_s, ka_s, va_s, qb_s, kb_s, vb_s, qi_s, kw_s = _pre_project(
        xs2, mods_s[0], mods_s[1], g_attn_l, wqkv, wkw, nrm, gmat, tm=db, rows_per_group=db, with_kmean=False)
    pool = lambda cch: cch.transpose(0, 1, 3, 4, 2)
    pool_kidx = cache_dsa_kidx.transpose(0, 1, 3, 2)
    row3 = lambda a: a.reshape(db, 1, a.shape[-1])
    scores, score_new, bias_blk, logits_a = _sample_stats(
        page_table, qi_s.reshape(db, IDX_HEADS, IDX_DIM), kw_s[:, IDX_DIM:IDX_DIM + IDX_HEADS].reshape(db, IDX_HEADS, 1),
        row3(kw_s[:, :IDX_DIM]), row3(qa_s), pool_kidx, pool(cache_moba_k))
    n_sel_s = min(DSA_TOPK, (past + ds) // 4)
    bias_pos, bias_new = _dsa_sample_select(scores.reshape(db, past), score_new.reshape(db, LANES), n_sel_s)
    o_a_s = _moba_sample_attention(page_table, slopes_a, logits_a, row3(qa_s), row3(ka_s), row3(va_s), bias_blk,
                                   pool(cache_moba_v))
    o_b_s = _paged_attention(page_table, slopes_b, row3(qb_s), row3(kb_s), row3(vb_s),
                             bias_pos.reshape(db, 1, past), bias_new.reshape(db, 1, LANES),
                             pool(cache_dsa_k), pool(cache_dsa_v))
    state = state_ffn_conv[l]
    y_s, a_s = _finish(xs2, o_a_s.reshape(db, WIDTH), o_b_s.reshape(db, WIDTH), mods_s[2:], g_ffn_l, wout, wupa, wupg,
                       conv_w[l], cb, wdn, tm=db, rows_per_group=db, prefix=(state[:, 0], state[:, 1]))
    conv_s = jnp.stack([state[:, 1], a_s], axis=1)
    kvs_shape = (1, db, ds, N_HEADS, HEAD_DIM)
    outs_s = (ka_s.reshape(kvs_shape), va_s.reshape(kvs_shape), kb_s.reshape(kvs_shape), vb_s.reshape(kvs_shape),
              kw_s[:, :IDX_DIM].reshape(1, db, ds, IDX_DIM), conv_s[None])
    return (y_p.reshape(bsz, t, d), y_s.reshape(db, ds, d)) + outs_p + outs_s
```

```python
import functools

import jax
import jax.numpy as jnp
from jax import lax
from jax.experimental import pallas as pl
from jax.experimental.pallas import tpu as pltpu

HEAD_DIM = 64
N_HEADS = 8
WIDTH = N_HEADS * HEAD_DIM
MOBA_BLOCK = 256
MOBA_TOPK = 3
DSA_TOPK = 256
IDX_HEADS = 8
IDX_DIM = 64
CONV_WIDTH = 3
EPS = 1e-6
N_MOD = 6
ATTN_SCALE = HEAD_DIM ** -0.5
N_QKV_CHUNKS = 7
LANES = 128
SUBLANES = 8
TQ = MOBA_BLOCK
SB = 2048
SEL_CHUNK = 1024
BPS = SB // TQ
FF_CHUNK = 256
PAGES_PER_STEP = 8
VMEM_LIMIT = 56 * 1024 * 1024

BF = jnp.bfloat16
F32 = jnp.float32
I32 = jnp.int32
NEG_INF = float("-inf")
INT_MIN = -2 ** 31


def _cp(sem):
    return pltpu.CompilerParams(dimension_semantics=sem, vmem_limit_bytes=VMEM_LIMIT)


def _const_spec(shape):
    nd = len(shape)
    return pl.BlockSpec(shape, lambda *_: (0,) * nd)


def _dot_nt(a, b):
    return lax.dot_general(a, b, (((1,), (1,)), ((), ())), preferred_element_type=F32)


def _mod_kernel(c_ref, w_ref, b_ref, o_ref):
    s = jax.nn.silu(c_ref[...])
    o_ref[...] = jnp.dot(s.astype(BF), w_ref[...].astype(BF), preferred_element_type=F32) + b_ref[...]


def _modulation(c_all, w_ada, b_ada):
    r, d = c_all.shape
    n = w_ada.shape[1]
    tn = 1024
    return pl.pallas_call(
        _mod_kernel,
        grid=(n // tn,),
        in_specs=[pl.BlockSpec((r, d), lambda j: (0, 0)),
                  pl.BlockSpec((d, tn), lambda j: (0, j)),
                  pl.BlockSpec((1, tn), lambda j: (0, j))],
        out_specs=pl.BlockSpec((r, tn), lambda j: (0, j)),
        out_shape=jax.ShapeDtypeStruct((r, n), F32),
        compiler_params=_cp(("arbitrary",)),
        name="modulation",
    )(c_all, w_ada, b_ada.reshape(1, n))


def _pre_kernel(x_ref, shift_ref, scale_ref, g_ref, wqkv_ref, wkw_ref, nrm_ref, gmat_ref, *outs, tm, with_kmean):
    x = x_ref[...]
    ms = jnp.mean(x * x, axis=-1, keepdims=True)
    xn = x * lax.rsqrt(ms + EPS) * g_ref[...]
    h = (xn * (1.0 + scale_ref[0]) + shift_ref[0]).astype(BF)
    norm_row = {0: 0, 1: 1, 3: 2, 4: 3}
    for c in range(N_QKV_CHUNKS):
        p = jnp.dot(h, wqkv_ref[:, c * WIDTH:(c + 1) * WIDTH], preferred_element_type=F32)
        if c in norm_row:
            ss = jnp.dot((p * p).astype(BF), gmat_ref[...], preferred_element_type=F32)
            k = norm_row[c]
            p = p * lax.rsqrt(ss * (1.0 / HEAD_DIM) + EPS) * nrm_ref[k:k + 1, :]
        outs[c][...] = p
        if c == 1 and with_kmean:
            outs[N_QKV_CHUNKS + 1][0] = jnp.mean(p.reshape(tm // MOBA_BLOCK, MOBA_BLOCK, WIDTH), axis=1)
    outs[N_QKV_CHUNKS][...] = jnp.dot(h, wkw_ref[...], preferred_element_type=F32)


def _pre_prompt_kernel(x_ref, shift_ref, scale_ref, g_ref, wqkv_ref, wkw_ref, nrm_ref, gmat_ref,
                       qa_ref, qb_ref, qi_ref, kw_ref, kat_ref, vat_ref, kbt_ref, vbt_ref,
                       katb_ref, vatb_ref, kbtb_ref, vbtb_ref, kwtb_ref, kit_ref, km_ref, *, tm):
    x = x_ref[...]
    ms = jnp.mean(x * x, axis=-1, keepdims=True)
    xn = x * lax.rsqrt(ms + EPS) * g_ref[...]
    h = (xn * (1.0 + scale_ref[0]) + shift_ref[0]).astype(BF)

    def proj(c, norm):
        p = jnp.dot(h, wqkv_ref[:, c * WIDTH:(c + 1) * WIDTH], preferred_element_type=F32)
        if norm is not None:
            ss = jnp.dot((p * p).astype(BF), gmat_ref[...], preferred_element_type=F32)
            p = p * lax.rsqrt(ss * (1.0 / HEAD_DIM) + EPS) * nrm_ref[norm:norm + 1, :]
        return p

    qa_ref[...] = (proj(0, 0) * ATTN_SCALE).astype(BF)
    ka = proj(1, 1)
    km_ref[0] = jnp.mean(ka.reshape(tm // MOBA_BLOCK, MOBA_BLOCK, WIDTH), axis=1)
    for p, t_ref, tb_ref in ((ka, kat_ref, katb_ref), (proj(2, None), vat_ref, vatb_ref)):
        pt = p.T
        t_ref[0] = pt
        tb_ref[0] = pt.astype(BF)
    qb_ref[...] = (proj(3, 2) * ATTN_SCALE).astype(BF)
    for p, t_ref, tb_ref in ((proj(4, 3), kbt_ref, kbtb_ref), (proj(5, None), vbt_ref, vbtb_ref)):
        pt = p.T
        t_ref[0] = pt
        tb_ref[0] = pt.astype(BF)
    qi_ref[...] = proj(6, None)
    kw = jnp.dot(h, wkw_ref[...], preferred_element_type=F32)
    kw_ref[...] = kw
    kwt = kw.T
    kwtb_ref[0] = kwt.astype(BF)
    kit_ref[0] = kwt[:IDX_DIM, :]


def _pre_project_prompt(x2, shift, scale, g_attn, wqkv, wkw, nrm, gmat, *, tm, bsz, t):
    n, d = x2.shape
    tpb = t // tm
    mod_spec = pl.BlockSpec((1, 1, d), lambda i: (i // tpb, 0, 0))
    row_spec = lambda w: pl.BlockSpec((tm, w), lambda i: (i, 0))
    tr_spec = lambda r: pl.BlockSpec((1, r, tm), lambda i: (i // tpb, 0, i % tpb))
    nb = tm // MOBA_BLOCK
    sds = jax.ShapeDtypeStruct
    out_shape = ([sds((n, WIDTH), BF)] * 2 + [sds((n, WIDTH), F32), sds((n, LANES), F32)]
                 + [sds((bsz, WIDTH, t), F32)] * 4 + [sds((bsz, WIDTH, t), BF)] * 4
                 + [sds((bsz, LANES, t), BF), sds((bsz, IDX_DIM, t), F32), sds((n // tm, nb, WIDTH), F32)])
    out_specs = ([row_spec(WIDTH)] * 3 + [row_spec(LANES)] + [tr_spec(WIDTH)] * 8
                 + [tr_spec(LANES), tr_spec(IDX_DIM), pl.BlockSpec((1, nb, WIDTH), lambda i: (i, 0, 0))])
    return pl.pallas_call(
        functools.partial(_pre_prompt_kernel, tm=tm),
        grid=(n // tm,),
        in_specs=[row_spec(d), mod_spec, mod_spec, _const_spec((1, d)),
                  _const_spec(wqkv.shape), _const_spec(wkw.shape), _const_spec(nrm.shape), _const_spec(gmat.shape)],
        out_specs=out_specs,
        out_shape=out_shape,
        compiler_params=_cp(("arbitrary",)),
        name="pre_project_prompt",
    )(x2, shift, scale, g_attn, wqkv, wkw, nrm, gmat)


def _pre_project(x2, shift, scale, g_attn, wqkv, wkw, nrm, gmat, *, tm, rows_per_group, with_kmean):
    n, d = x2.shape
    grid = (n // tm,)
    mr = shift.shape[1]
    tiles_per_group = rows_per_group // tm
    mod_spec = pl.BlockSpec((1, mr, d), lambda i: (i // tiles_per_group, 0, 0))
    row_spec = lambda w: pl.BlockSpec((tm, w), lambda i: (i, 0))
    out_shape = [jax.ShapeDtypeStruct((n, WIDTH), F32)] * N_QKV_CHUNKS + [jax.ShapeDtypeStruct((n, LANES), F32)]
    out_specs = [row_spec(WIDTH)] * N_QKV_CHUNKS + [row_spec(LANES)]
    if with_kmean:
        nb = tm // MOBA_BLOCK
        out_shape.append(jax.ShapeDtypeStruct((n // tm, nb, WIDTH), F32))
        out_specs.append(pl.BlockSpec((1, nb, WIDTH), lambda i: (i, 0, 0)))
    return pl.pallas_call(
        functools.partial(_pre_kernel, tm=tm, with_kmean=with_kmean),
        grid=grid,
        in_specs=[row_spec(d), mod_spec, mod_spec, _const_spec((1, d)),
                  _const_spec(wqkv.shape), _const_spec(wkw.shape), _const_spec(nrm.shape), _const_spec(gmat.shape)],
        out_specs=out_specs,
        out_shape=out_shape,
        compiler_params=_cp(("arbitrary",)),
        name="pre_project",
    )(x2, shift, scale, g_attn, wqkv, wkw, nrm, gmat)


def _pair_half(shape, axis):
    return lax.broadcasted_iota(I32, shape, axis) // HEAD_DIM


def _two_pass_attention(q_pair, par, kt_ref, vt_ref, slope, i, bias_at, l_ref, m_ref, acc_ref):
    n_steps = i // BPS + 1
    q = jnp.where(_pair_half((TQ, LANES), 1) == par, q_pair, jnp.zeros_like(q_pair))
    own_rows = _pair_half((LANES, SB), 0) == par
    m_ref[...] = jnp.full((TQ, TQ), NEG_INF, F32)

    def keys(ref, j):
        return ref[0, :, pl.ds(pl.multiple_of(j * SB, SB), SB)]

    def pass1(j, carry):
        cols = lax.broadcasted_iota(I32, (1, SB), 1) + (j * SB - i * TQ)
        l = jnp.dot(q, keys(kt_ref, j), preferred_element_type=F32) + slope * cols.astype(F32) + bias_at(j)
        l_ref[j] = l
        mx = m_ref[...]
        for k in range(BPS):
            mx = jnp.maximum(mx, l[:, k * TQ:(k + 1) * TQ])
        m_ref[...] = mx
        return carry

    lax.fori_loop(0, n_steps, pass1, 0)
    m = jnp.max(m_ref[...], axis=-1, keepdims=True)
    acc_ref[...] = jnp.zeros((TQ, LANES), F32)

    def pass2(j, carry):
        p = jnp.exp((l_ref[j] - m).astype(BF))
        vt = jnp.where(own_rows, keys(vt_ref, j), jnp.ones((LANES, SB), BF))
        acc_ref[...] += _dot_nt(p, vt)
        return carry

    lax.fori_loop(0, n_steps, pass2, 0)
    acc = acc_ref[...]
    return acc / pltpu.roll(acc, HEAD_DIM, axis=1)


def _store_pair_half(o_ref, par, val):
    keep = _pair_half((TQ, LANES), 1) == par

    @pl.when(par == 0)
    def _():
        o_ref[...] = jnp.where(keep, val, 0.0)

    @pl.when(par != 0)
    def _():
        o_ref[...] = jnp.where(keep, val, o_ref[...])


def _moba_prompt_kernel(slopes_ref, q_ref, kt_ref, vt_ref, kmt_ref, o_ref, l_ref, m_ref, acc_ref, bias_ref, sel_ref):
    pr = pl.program_id(1)
    i = pl.program_id(2)
    par = pl.program_id(3)
    sub = i % BPS
    row0 = pl.multiple_of(sub * TQ, TQ)

    @pl.when(sub == 0)
    def _():
        qs = q_ref[...]
        qs = jnp.where(_pair_half(qs.shape, 1) == par, qs, jnp.zeros_like(qs))
        s = jnp.dot(qs, kmt_ref[0], preferred_element_type=F32)
        lane_s = lax.broadcasted_iota(I32, s.shape, 1)
        own = i + lax.broadcasted_iota(I32, s.shape, 0) // TQ
        tri = (lax.broadcasted_iota(I32, (LANES, LANES), 0) <= lax.broadcasted_iota(I32, (LANES, LANES), 1))
        tri = jnp.where(tri, 1.0, 0.0).astype(BF)
        s = jnp.where(lane_s < own, s, NEG_INF)
        bias = jnp.full(s.shape, NEG_INF, F32)
        for _ in range(MOBA_TOPK):
            mx = jnp.max(s, axis=-1, keepdims=True)
            eq = s == mx
            rank = jnp.dot(jnp.where(eq, 1.0, 0.0).astype(BF), tri, preferred_element_type=F32)
            pick = eq & (rank == 1.0) & (mx > NEG_INF)
            bias = jnp.where(pick, 0.0, bias)
            s = jnp.where(pick, NEG_INF, s)
        sel_ref[par] = bias

    q = q_ref[pl.ds(row0, TQ), :]
    bias_ref[...] = sel_ref[par, pl.ds(row0, TQ), :]
    lane = lax.broadcasted_iota(I32, (TQ, LANES), 1)
    r = lax.broadcasted_iota(I32, (TQ, TQ), 0)
    c = lax.broadcasted_iota(I32, (TQ, TQ), 1)
    causal = jnp.where(c <= r, 0.0, NEG_INF).astype(F32)

    def bias_at(j):
        parts = []
        for k in range(BPS):
            blk = j * BPS + k
            col = jnp.max(jnp.where(lane == blk, bias_ref[...], NEG_INF), axis=-1, keepdims=True)
            parts.append(jnp.where(blk == i, causal, jnp.broadcast_to(col, (TQ, TQ))))
        return jnp.concatenate(parts, axis=1)

    out = _two_pass_attention(q, par, kt_ref, vt_ref, slopes_ref[2 * pr + par], i, bias_at, l_ref, m_ref, acc_ref)
    _store_pair_half(o_ref, par, out)


def _moba_prompt(slopes, q, kt, vt, kmt, bsz, t):
    nq = t // TQ
    nsb = t // SB
    npair = N_HEADS // 2
    grid_spec = pltpu.PrefetchScalarGridSpec(
        num_scalar_prefetch=1,
        grid=(bsz, npair, nq, 2),
        in_specs=[pl.BlockSpec((SB, LANES), lambda bi, pr, i, par, s: (bi * nsb + i // BPS, pr)),
                  pl.BlockSpec((1, LANES, t), lambda bi, pr, i, par, s: (bi, pr, 0)),
                  pl.BlockSpec((1, LANES, t), lambda bi, pr, i, par, s: (bi, pr, 0)),
                  pl.BlockSpec((1, LANES, LANES), lambda bi, pr, i, par, s: (bi, pr, 0))],
        out_specs=pl.BlockSpec((TQ, LANES), lambda bi, pr, i, par, s: (bi * nq + i, pr)),
        scratch_shapes=[pltpu.VMEM((nsb, TQ, SB), F32), pltpu.VMEM((TQ, TQ), F32),
                        pltpu.VMEM((TQ, LANES), F32), pltpu.VMEM((TQ, LANES), F32), pltpu.VMEM((2, SB, LANES), F32)],
    )
    return pl.pallas_call(
        _moba_prompt_kernel,
        grid_spec=grid_spec,
        out_shape=jax.ShapeDtypeStruct((bsz * t, WIDTH), F32),
        compiler_params=_cp(("arbitrary",) * 4),
        name="moba_prompt",
    )(slopes, q, kt, vt, kmt)


def _threshold_from_code(code):
    bits = code ^ ((code >> 31) & 0x7FFFFFFF)
    return lax.bitcast_convert_type(bits, F32)


def _dsa_select(i, n_sel, idx_bits, sk_ref, b_ref, jc_ref):
    n_chunks = (i * TQ) // SEL_CHUNK + 1
    per_step = SB // SEL_CHUNK
    rc = TQ // 2
    c = lax.broadcasted_iota(I32, (rc, SEL_CHUNK), 1)

    def chunk(ref, j, rows):
        return ref.at[j // per_step, rows, pl.ds(pl.multiple_of((j % per_step) * SEL_CHUNK, SEL_CHUNK), SEL_CHUNK)]

    for r0 in range(0, TQ, rc):
        rows = slice(r0, r0 + rc)
        gr = lax.broadcasted_iota(I32, (rc, SEL_CHUNK), 0) + (i * TQ + r0)

        def count(pred, rows=rows):
            def body(j, acc):
                hit = jnp.where(pred(j, chunk(sk_ref, j, rows)[...]), 1.0, 0.0)
                for k in range(SEL_CHUNK // LANES):
                    acc = acc + hit[:, k * LANES:(k + 1) * LANES]
                return acc

            acc = lax.fori_loop(0, n_chunks, body, jnp.zeros((rc, LANES), F32))
            return jnp.sum(acc, axis=-1, keepdims=True)

        def bit_step(it, carry, count=count):
            v, cnt_v = carry
            cand = v + jnp.left_shift(jnp.int32(1), 31 - it)
            cand_f = _threshold_from_code(cand)
            cnt = count(lambda j, sc: sc >= cand_f)
            keep = cnt >= n_sel
            return jnp.where(keep, cand, v), jnp.where(keep, cnt, cnt_v)

        v, cnt_ge = lax.fori_loop(0, 32, bit_step, (jnp.full((rc, 1), INT_MIN, I32), jnp.zeros((rc, 1), F32)))
        few = v == INT_MIN
        thr = jnp.where(few, NEG_INF, _threshold_from_code(v))
        cnt_gt = count(lambda j, sc: sc > thr)
        need = n_sel - cnt_gt
        trim = (cnt_ge > n_sel) & jnp.logical_not(few)
        jc_ref[rows, :] = jnp.full((rc, LANES), 2 ** idx_bits, I32)

        @pl.when(jnp.max(jnp.where(trim, 1.0, 0.0)) > 0.0)
        def _(rows=rows, count=count, thr=thr, need=need, trim=trim):
            def idx_step(it, jcap):
                cand = jcap + jnp.left_shift(jnp.int32(1), idx_bits - 1 - it)
                cnt = count(lambda j, sc: (sc == thr) & (c + j * SEL_CHUNK < cand))
                return jnp.where(cnt < need, cand, jcap)

            jcap = lax.fori_loop(0, idx_bits, idx_step, jnp.zeros((rc, 1), I32))
            jc_ref[rows, :] = jnp.broadcast_to(jnp.where(trim, jcap, 2 ** idx_bits), (rc, LANES))

        jcap = jc_ref[rows, 0:1]

        def write(j, carry, rows=rows, thr=thr, jcap=jcap, gr=gr):
            sc = chunk(sk_ref, j, rows)[...]
            gc = c + j * SEL_CHUNK
            sel = ((sc > thr) | ((sc == thr) & (gc <= jcap))) & (gc <= gr)
            chunk(b_ref, j, rows)[...] = jnp.where(sel, 0.0, NEG_INF)
            return carry

        lax.fori_loop(0, (i // BPS + 1) * per_step, write, 0)


def _dsa_prompt_kernel(slopes_ref, qi_ref, kw_ref, kwt_ref, q_ref, kt_ref, vt_ref, o_ref,
                       sk_ref, b_ref, l_ref, m_ref, acc_ref, jc_ref, qz_ref, *, n_sel, idx_bits):
    i = pl.program_id(1)
    h = pl.program_id(2)
    par = h % 2

    @pl.when(h == 0)
    def _():
        wi = kw_ref[...] * (IDX_HEADS * IDX_DIM) ** -0.5
        wcols = [wi[:, IDX_DIM + hh:IDX_DIM + hh + 1] for hh in range(IDX_HEADS)]
        gr = lax.broadcasted_iota(I32, (TQ, SB), 0) + i * TQ
        c = lax.broadcasted_iota(I32, (TQ, SB), 1)
        first = _pair_half((TQ, LANES), 1) == 0
        for hh in range(IDX_HEADS):
            qp = qi_ref[:, (hh // 2) * LANES:(hh // 2 + 1) * LANES]
            if hh % 2:
                qp = pltpu.roll(qp, HEAD_DIM, axis=1)
            qz_ref[hh] = jnp.where(first, qp, 0.0).astype(BF)

        def score_step(j, carry):
            kit = kwt_ref[0, :, pl.ds(pl.multiple_of(j * SB, SB), SB)]
            sc = jnp.maximum(jnp.dot(qz_ref[0], kit, preferred_element_type=F32), 0.0) * wcols[0]
            for hh in range(1, IDX_HEADS):
                sc = sc + jnp.maximum(jnp.dot(qz_ref[hh], kit, preferred_element_type=F32), 0.0) * wcols[hh]
            sk_ref[j] = jnp.where(c + j * SB <= gr, sc, NEG_INF)
            return carry

        lax.fori_loop(0, i // BPS + 1, score_step, 0)
        _dsa_select(i, n_sel, idx_bits, sk_ref, b_ref, jc_ref)

    out = _two_pass_attention(q_ref[...], par, kt_ref, vt_ref, slopes_ref[h], i, lambda j: b_ref[j],
                              l_ref, m_ref, acc_ref)
    _store_pair_half(o_ref, par, out)


def _dsa_prompt(slopes, qi, kw, kwt, q, kt, vt, n_sel, bsz, t):
    nq = t // TQ
    nsb = t // SB
    idx_bits = max(1, (t - 1).bit_length())
    blk = lambda shape, fn: pl.BlockSpec(shape, fn)
    grid_spec = pltpu.PrefetchScalarGridSpec(
        num_scalar_prefetch=1,
        grid=(bsz, nq, N_HEADS),
        in_specs=[blk((TQ, WIDTH), lambda bi, i, hi, s: (bi * nq + i, 0)),
                  blk((TQ, LANES), lambda bi, i, hi, s: (bi * nq + i, 0)),
                  blk((1, LANES, t), lambda bi, i, hi, s: (bi, 0, 0)),
                  blk((TQ, LANES), lambda bi, i, hi, s: (bi * nq + i, hi // 2)),
                  blk((1, LANES, t), lambda bi, i, hi, s: (bi, hi // 2, 0)),
                  blk((1, LANES, t), lambda bi, i, hi, s: (bi, hi // 2, 0))],
        out_specs=blk((TQ, LANES), lambda bi, i, hi, s: (bi * nq + i, hi // 2)),
        scratch_shapes=[pltpu.VMEM((nsb, TQ, SB), F32), pltpu.VMEM((nsb, TQ, SB), F32), pltpu.VMEM((nsb, TQ, SB), F32),
                        pltpu.VMEM((TQ, TQ), F32), pltpu.VMEM((TQ, LANES), F32), pltpu.VMEM((TQ, LANES), I32),
                        pltpu.VMEM((IDX_HEADS, TQ, LANES), BF)],
    )
    return pl.pallas_call(
        functools.partial(_dsa_prompt_kernel, n_sel=n_sel, idx_bits=idx_bits),
        grid_spec=grid_spec,
        out_shape=jax.ShapeDtypeStruct((bsz * t, WIDTH), F32),
        compiler_params=_cp(("arbitrary", "arbitrary", "arbitrary")),
        name="dsa_prompt",
    )(slopes, qi, kw, kwt, q, kt, vt)


def _finish_kernel(*refs, tm, nchunk, seq_mode, tiles_per_seq):
    (x_ref, oa_ref, ob_ref, gate_a_ref, shift_f_ref, scale_f_ref, gate_f_ref, g_ref, wout_ref,
     wupa_ref, wupg_ref, cw_ref, cb_ref, wdn_ref) = refs[:14]
    if seq_mode:
        y_ref, st_ref, abuf_ref, carry_ref, yacc_ref = refs[14:]
    else:
        pre0_ref, pre1_ref, y_ref, st_ref, yacc_ref = refs[14:]
    i = pl.program_id(0)
    o = (jnp.dot(oa_ref[...].astype(BF), wout_ref[0:WIDTH, :], preferred_element_type=F32)
         + jnp.dot(ob_ref[...].astype(BF), wout_ref[WIDTH:2 * WIDTH, :], preferred_element_type=F32))
    x1 = x_ref[...] + gate_a_ref[0] * o
    ms = jnp.mean(x1 * x1, axis=-1, keepdims=True)
    h = ((x1 * lax.rsqrt(ms + EPS) * g_ref[...]) * (1.0 + scale_f_ref[0]) + shift_f_ref[0]).astype(BF)
    yacc_ref[...] = jnp.zeros_like(yacc_ref)
    for c in range(nchunk):
        cols = slice(c * FF_CHUNK, (c + 1) * FF_CHUNK)
        a = jnp.dot(h, wupa_ref[c], preferred_element_type=F32)
        gt = jnp.dot(h, wupg_ref[c], preferred_element_type=F32)
        if seq_mode:
            first = (i % tiles_per_seq) == 0
            buf = abuf_ref.at[c % 2]
            buf[0:SUBLANES, :] = jnp.where(first, 0.0, carry_ref[c])
            buf[SUBLANES:SUBLANES + tm, :] = a
            a1 = buf[SUBLANES - 1:SUBLANES - 1 + tm, :]
            a2 = buf[SUBLANES - 2:SUBLANES - 2 + tm, :]
            carry_ref[c] = a[tm - SUBLANES:tm, :]
            st_ref[0, :, cols] = a[tm - (CONV_WIDTH - 1):tm, :]
        else:
            a2 = pre0_ref[:, cols]
            a1 = pre1_ref[:, cols]
            st_ref[:, cols] = a
        conv = cb_ref[:, cols] + a2 * cw_ref[0:1, cols]
        conv = conv + a1 * cw_ref[1:2, cols]
        conv = conv + a * cw_ref[2:3, cols]
        act = (jax.nn.gelu(conv) * gt).astype(BF)
        yacc_ref[...] += jnp.dot(act, wdn_ref[c], preferred_element_type=F32)
    y_ref[...] = x1 + gate_f_ref[0] * yacc_ref[...]


def _finish(x2, oa, ob, mods, g_ffn, wout, wupa, wupg, conv_w, conv_b, wdn, *, tm, rows_per_group, prefix=None):
    n, d = x2.shape
    nchunk = wupa.shape[0]
    f = nchunk * FF_CHUNK
    seq_mode = prefix is None
    tiles_per_seq = rows_per_group // tm
    mr = mods[0].shape[1]
    mod_spec = pl.BlockSpec((1, mr, d), lambda i: (i // tiles_per_seq, 0, 0))
    row_spec = lambda w: pl.BlockSpec((tm, w), lambda i: (i, 0))
    in_specs = ([row_spec(d), row_spec(WIDTH), row_spec(WIDTH)] + [mod_spec] * 4 +
                [_const_spec((1, d)), _const_spec(wout.shape), _const_spec(wupa.shape), _const_spec(wupg.shape),
                 _const_spec(conv_w.shape), _const_spec((1, f)), _const_spec(wdn.shape)])
    args = [x2, oa, ob, *mods, g_ffn, wout, wupa, wupg, conv_w, conv_b, wdn]
    scratch = []
    if seq_mode:
        nseq = n // rows_per_group
        st_shape = jax.ShapeDtypeStruct((nseq, CONV_WIDTH - 1, f), F32)
        st_spec = pl.BlockSpec((1, CONV_WIDTH - 1, f), lambda i: (i // tiles_per_seq, 0, 0))
        scratch += [pltpu.VMEM((2, SUBLANES + tm, FF_CHUNK), F32), pltpu.VMEM((nchunk, SUBLANES, FF_CHUNK), F32)]
    else:
        in_specs += [row_spec(f), row_spec(f)]
        args += list(prefix)
        st_shape = jax.ShapeDtypeStruct((n, f), F32)
        st_spec = row_spec(f)
    scratch.append(pltpu.VMEM((tm, d), F32))
    return pl.pallas_call(
        functools.partial(_finish_kernel, tm=tm, nchunk=nchunk, seq_mode=seq_mode, tiles_per_seq=tiles_per_seq),
        grid=(n // tm,),
        in_specs=in_specs,
        out_specs=[row_spec(d), st_spec],
        out_shape=[jax.ShapeDtypeStruct((n, d), F32), st_shape],
        scratch_shapes=scratch,
        compiler_params=_cp(("arbitrary",)),
        name="finish",
    )(*args)


def _block_diag_rows(row):
    hrow = lax.broadcasted_iota(I32, (N_HEADS, WIDTH), 0)
    hlane = lax.broadcasted_iota(I32, (N_HEADS, WIDTH), 1) // HEAD_DIM
    return jnp.where(hrow == hlane, jnp.broadcast_to(row, (N_HEADS, WIDTH)), 0.0)


def _sample_stats_kernel(pt_ref, qi_ref, wi_ref, kin_ref, qa_ref, *refs, page, nblk):
    pps = PAGES_PER_STEP
    kidx_refs = refs[:pps]
    kmoba_refs = refs[pps:2 * pps]
    sc_ref, scn_ref, bb_ref, lg_ref, km_ref = refs[2 * pps:]
    s = pl.program_id(1)
    qbd = _block_diag_rows(qa_ref[0] * ATTN_SCALE).astype(BF)
    qi = qi_ref[0].astype(BF)
    wi = wi_ref[0] * (IDX_HEADS * IDX_DIM) ** -0.5
    ppb = MOBA_BLOCK // page
    lane = lax.broadcasted_iota(I32, (WIDTH, LANES), 1)

    @pl.when(s == 0)
    def _():
        km_ref[...] = jnp.zeros(km_ref.shape, F32)

    kidx = jnp.concatenate([kidx_refs[k][0, 0].astype(BF) for k in range(pps)], axis=1)
    rel = jnp.maximum(jnp.dot(qi, kidx, preferred_element_type=F32), 0.0)
    sc_ref[0] = jnp.sum(rel * wi, axis=0, keepdims=True)
    for kb in range(pps // ppb):
        tot = kmoba_refs[kb * ppb][0, 0].reshape(WIDTH, page)
        for k in range(1, ppb):
            tot = tot + kmoba_refs[kb * ppb + k][0, 0].reshape(WIDTH, page)
        mean = jnp.sum(tot, axis=-1, keepdims=True) * (1.0 / MOBA_BLOCK)
        km_ref[...] = jnp.where(lane == s * (pps // ppb) + kb, mean, km_ref[...])
    kt = jnp.concatenate([r[0, 0].reshape(WIDTH, page).astype(BF) for r in kmoba_refs], axis=1)
    lg_ref[0] = jnp.dot(qbd, kt, preferred_element_type=F32)

    @pl.when(s == pl.num_programs(1) - 1)
    def _():
        kin = jnp.broadcast_to(kin_ref[0], (SUBLANES, IDX_DIM)).astype(BF)
        reln = jnp.maximum(_dot_nt(qi, kin), 0.0)
        scn = jnp.sum(reln[:, 0:1] * wi, axis=0, keepdims=True)
        scn_ref[0] = jnp.broadcast_to(scn, (1, LANES))
        sb = jnp.dot(qbd, km_ref[...].astype(BF), preferred_element_type=F32)
        lane_b = lax.broadcasted_iota(I32, sb.shape, 1)
        lanef = lane_b.astype(F32)
        sb = jnp.where(lane_b < nblk, sb, NEG_INF)
        bias = jnp.full(sb.shape, NEG_INF, F32)
        for _ in range(MOBA_TOPK):
            mx = jnp.max(sb, axis=-1, keepdims=True)
            first = jnp.min(jnp.where(sb == mx, lanef, float(LANES)), axis=-1, keepdims=True)
            pick = (lanef == first) & (mx > NEG_INF)
            bias = jnp.where(pick, 0.0, bias)
            sb = jnp.where(pick, NEG_INF, sb)
        bb_ref[0] = bias


def _sample_stats(page_table, qi_s, wi_s, ki_s, qa_s, pool_kidx, pool_mk):
    db, n_pages = page_table.shape
    page = pool_kidx.shape[3]
    pps = PAGES_PER_STEP
    past = n_pages * page
    nblk = past // MOBA_BLOCK
    assert n_pages % pps == 0 and pps % (MOBA_BLOCK // page) == 0 and nblk <= LANES and page == LANES
    kidx_spec = lambda k: pl.BlockSpec((1, 1, IDX_DIM, page), lambda b, s, pt: (0, pt[b, s * pps + k], 0, 0))
    kmoba_spec = lambda k: pl.BlockSpec((1, 1, N_HEADS, HEAD_DIM, page),
                                        lambda b, s, pt: (0, pt[b, s * pps + k], 0, 0, 0))
    per_seq = lambda shape: pl.BlockSpec((1,) + shape, lambda b, s, pt: (b, 0, 0))
    grid_spec = pltpu.PrefetchScalarGridSpec(
        num_scalar_prefetch=1,
        grid=(db, n_pages // pps),
        in_specs=([per_seq((IDX_HEADS, IDX_DIM)), per_seq((IDX_HEADS, 1)), per_seq((1, IDX_DIM)), per_seq((1, WIDTH))]
                  + [kidx_spec(k) for k in range(pps)] + [kmoba_spec(k) for k in range(pps)]),
        out_specs=[pl.BlockSpec((1, 1, pps * page), lambda b, s, pt: (b, 0, s)),
                   per_seq((1, LANES)), per_seq((N_HEADS, LANES)),
                   pl.BlockSpec((1, N_HEADS, pps * page), lambda b, s, pt: (b, 0, s))],
        scratch_shapes=[pltpu.VMEM((WIDTH, LANES), F32)],
    )
    return pl.pallas_call(
        functools.partial(_sample_stats_kernel, page=page, nblk=nblk),
        grid_spec=grid_spec,
        out_shape=[jax.ShapeDtypeStruct((db, 1, past), F32), jax.ShapeDtypeStruct((db, 1, LANES), F32),
                   jax.ShapeDtypeStruct((db, N_HEADS, LANES), F32), jax.ShapeDtypeStruct((db, N_HEADS, past), F32)],
        compiler_params=_cp(("arbitrary", "arbitrary")),
        name="sample_stats",
    )(page_table, qi_s, wi_s, ki_s, qa_s, *([pool_kidx] * pps), *([pool_mk] * pps))


def _dsa_sample_select_kernel(sc_ref, scn_ref, b_ref, bn_ref, *, n_sel, idx_bits):
    sc = sc_ref[...]
    scn = scn_ref[:, 0:1]
    db, past = sc.shape
    gidx = lax.broadcasted_iota(I32, sc.shape, 1)

    def count(pred, pred_new):
        return (jnp.sum(jnp.where(pred, 1.0, 0.0), axis=-1, keepdims=True) + jnp.where(pred_new, 1.0, 0.0))

    def bit_step(it, v):
        cand = v + jnp.left_shift(jnp.int32(1), 31 - it)
        cand_f = _threshold_from_code(cand)
        cnt = count(sc >= cand_f, scn >= cand_f)
        return jnp.where(cnt >= n_sel, cand, v)

    v = lax.fori_loop(0, 32, bit_step, jnp.full((db, 1), INT_MIN, I32))
    few = v == INT_MIN
    thr = jnp.where(few, NEG_INF, _threshold_from_code(v))
    cnt_gt = count(sc > thr, scn > thr)
    cnt_ge = count(sc >= thr, scn >= thr)
    need = n_sel - cnt_gt
    trim = (cnt_ge > n_sel) & jnp.logical_not(few)

    def idx_step(it, jcap):
        cand = jcap + jnp.left_shift(jnp.int32(1), idx_bits - 1 - it)
        cnt = count((sc == thr) & (gidx < cand), (scn == thr) & (past < cand))
        return jnp.where(cnt < need, cand, jcap)

    jcap = lax.fori_loop(0, idx_bits, idx_step, jnp.zeros((db, 1), I32))
    jcap = jnp.where(trim, jcap, 2 ** idx_bits)
    sel = (sc > thr) | ((sc == thr) & (gidx <= jcap))
    seln = (scn > thr) | ((scn == thr) & (past <= jcap))
    bn_ref[...] = jnp.broadcast_to(jnp.where(seln, 0.0, NEG_INF), bn_ref.shape)
    b_ref[...] = jnp.where(sel, 0.0, NEG_INF)


def _dsa_sample_select(scores, score_new, n_sel):
    db, past = scores.shape
    idx_bits = past.bit_length()
    return pl.pallas_call(
        functools.partial(_dsa_sample_select_kernel, n_sel=n_sel, idx_bits=idx_bits),
        out_shape=[jax.ShapeDtypeStruct((db, past), F32), jax.ShapeDtypeStruct((db, LANES), F32)],
        compiler_params=pltpu.CompilerParams(vmem_limit_bytes=VMEM_LIMIT),
        name="dsa_sample_select",
    )(scores, score_new)


def _paged_attention_kernel(pt_ref, slopes_ref, q_ref, kn_ref, vn_ref, bias_ref, bn_ref, *refs, page):
    pps = PAGES_PER_STEP
    k_refs = refs[:pps]
    v_refs = refs[pps:2 * pps]
    o_ref, l_ref, mx_ref, m_ref, ps_ref, acc_ref, sl_ref, qbd_ref = refs[2 * pps:]
    s = pl.program_id(1)
    ns = pl.num_programs(1) // 2
    past = ns * pps * page
    hrow = lax.broadcasted_iota(I32, (N_HEADS, LANES), 0)
    pages = lambda prefs: jnp.concatenate([r[0, 0].reshape(WIDTH, page).astype(BF) for r in prefs], axis=1)

    @pl.when(s == 0)
    def _():
        sl = jnp.zeros((N_HEADS, LANES), F32)
        for h in range(N_HEADS):
            sl = jnp.where(hrow == h, slopes_ref[h], sl)
        sl_ref[...] = sl
        mx_ref[...] = jnp.full(mx_ref.shape, NEG_INF, F32)
        qbd_ref[...] = _block_diag_rows(q_ref[0] * ATTN_SCALE).astype(BF)

    @pl.when(s < ns)
    def _():
        wide = pps * page
        pos = s * wide + lax.broadcasted_iota(I32, (1, wide), 1)
        logits = jnp.dot(qbd_ref[...], pages(k_refs), preferred_element_type=F32)
        logits = logits - sl_ref[:, 0:1] * (past - pos).astype(F32)
        logits = logits + bias_ref[0]
        l_ref[s] = logits
        mx_ref[...] = jnp.maximum(mx_ref[...], logits)

    @pl.when(s == ns - 1)
    def _():
        kn = kn_ref[0].astype(BF).astype(F32)
        ln = jnp.sum(qbd_ref[...].astype(F32) * kn, axis=-1, keepdims=True) + bn_ref[0, :, 0:1]
        m = jnp.maximum(jnp.max(mx_ref[...], axis=-1, keepdims=True), ln)
        m_ref[...] = jnp.broadcast_to(m, m_ref.shape)
        pn = jnp.exp(ln - m)
        ps_ref[...] = jnp.zeros(ps_ref.shape, F32)
        ps_ref[:, 0:1] = pn
        acc_ref[...] = pn.astype(BF).astype(F32) * vn_ref[0].astype(BF).astype(F32)

    @pl.when(s >= ns)
    def _():
        p = jnp.exp(l_ref[s - ns] - m_ref[:, 0:1])
        ps_ref[...] += p
        acc_ref[...] += _dot_nt(p.astype(BF), pages(v_refs))

    @pl.when(s == 2 * ns - 1)
    def _():
        o = acc_ref[...] / jnp.sum(ps_ref[...], axis=-1, keepdims=True)
        hr = lax.broadcasted_iota(I32, (N_HEADS, WIDTH), 0)
        hl = lax.broadcasted_iota(I32, (N_HEADS, WIDTH), 1) // HEAD_DIM
        o_ref[0] = jnp.sum(jnp.where(hr == hl, o, 0.0), axis=0, keepdims=True)


def _paged_attention(page_table, slopes, q_s, k_new, v_new, bias, bias_new, pool_k, pool_v):
    db, n_pages = page_table.shape
    page = pool_k.shape[4]
    pps = PAGES_PER_STEP
    assert n_pages % pps == 0 and page == LANES
    ns = n_pages // pps
    pool_blk = (1, 1, N_HEADS, HEAD_DIM, page)
    k_spec = lambda k: pl.BlockSpec(pool_blk, lambda b, s, pt, sl: (0, pt[b, jnp.minimum(s, ns - 1) * pps + k], 0, 0, 0))
    v_spec = lambda k: pl.BlockSpec(pool_blk, lambda b, s, pt, sl: (0, pt[b, jnp.maximum(s - ns, 0) * pps + k], 0, 0, 0))
    per_seq = lambda shape: pl.BlockSpec((1,) + shape, lambda b, s, pt, sl: (b, 0, 0))
    bias_spec = pl.BlockSpec((1, 1, pps * page), lambda b, s, pt, sl: (b, 0, jnp.minimum(s, ns - 1)))
    grid_spec = pltpu.PrefetchScalarGridSpec(
        num_scalar_prefetch=2,
        grid=(db, 2 * ns),
        in_specs=([per_seq((1, WIDTH))] * 3 + [bias_spec, per_seq((1, LANES))]
                  + [k_spec(k) for k in range(pps)] + [v_spec(k) for k in range(pps)]),
        out_specs=per_seq((1, WIDTH)),
        scratch_shapes=[pltpu.VMEM((ns, N_HEADS, pps * page), F32), pltpu.VMEM((N_HEADS, pps * page), F32),
                        pltpu.VMEM((N_HEADS, LANES), F32), pltpu.VMEM((N_HEADS, pps * page), F32),
                        pltpu.VMEM((N_HEADS, WIDTH), F32), pltpu.VMEM((N_HEADS, LANES), F32),
                        pltpu.VMEM((N_HEADS, WIDTH), BF)],
    )
    return pl.pallas_call(
        functools.partial(_paged_attention_kernel, page=page),
        grid_spec=grid_spec,
        out_shape=jax.ShapeDtypeStruct((db, 1, WIDTH), F32),
        compiler_params=_cp(("arbitrary", "arbitrary")),
        name="paged_attention",
    )(page_table, slopes, q_s, k_new, v_new, bias, bias_new, *([pool_k] * pps), *([pool_v] * pps))


def _moba_sample_kernel(pt_ref, slopes_ref, lg_ref, q_ref, kn_ref, vn_ref, bias_ref, *refs, page):
    pps = PAGES_PER_STEP
    v_refs = refs[:pps]
    o_ref, p_ref, ps_ref, acc_ref = refs[pps:]
    s = pl.program_id(1)
    ns = pl.num_programs(1) - 1
    wide = pps * page
    past = ns * wide
    ppb = MOBA_BLOCK // page

    @pl.when(s == 0)
    def _():
        hrow = lax.broadcasted_iota(I32, (N_HEADS, 1), 0)
        sl = jnp.zeros((N_HEADS, 1), F32)
        for h in range(N_HEADS):
            sl = jnp.where(hrow == h, slopes_ref[h], sl)
        qbd = _block_diag_rows(q_ref[0] * ATTN_SCALE).astype(BF)
        kn = kn_ref[0].astype(BF).astype(F32)
        ln = jnp.sum(qbd.astype(F32) * kn, axis=-1, keepdims=True)
        bias = bias_ref[0]
        mx = ln
        for k in range(ns):
            pos = k * wide + lax.broadcasted_iota(I32, (1, wide), 1)
            cols = [jnp.broadcast_to(bias[:, (k * pps + kk) // ppb:(k * pps + kk) // ppb + 1], (N_HEADS, page))
                    for kk in range(pps)]
            l = lg_ref[0, :, k * wide:(k + 1) * wide] - sl * (past - pos).astype(F32) + jnp.concatenate(cols, axis=1)
            p_ref[k] = l
            mx = jnp.maximum(mx, jnp.max(l, axis=-1, keepdims=True))
        pn = jnp.exp(ln - mx)
        tot = pn
        for k in range(ns):
            p = jnp.exp(p_ref[k] - mx)
            p_ref[k] = p
            tot = tot + jnp.sum(p, axis=-1, keepdims=True)
        ps_ref[...] = jnp.broadcast_to(tot, ps_ref.shape)
        acc_ref[...] = pn.astype(BF).astype(F32) * vn_ref[0].astype(BF).astype(F32)

    @pl.when(s > 0)
    def _():
        vt = jnp.concatenate([r[0, 0].reshape(WIDTH, page).astype(BF) for r in v_refs], axis=1)
        acc_ref[...] += _dot_nt(p_ref[s - 1].astype(BF), vt)

    @pl.when(s == ns)
    def _():
        o = acc_ref[...] / ps_ref[:, 0:1]
        hr = lax.broadcasted_iota(I32, (N_HEADS, WIDTH), 0)
        hl = lax.broadcasted_iota(I32, (N_HEADS, WIDTH), 1) // HEAD_DIM
        o_ref[0] = jnp.sum(jnp.where(hr == hl, o, 0.0), axis=0, keepdims=True)


def _moba_sample_attention(page_table, slopes, logits, q_s, k_new, v_new, bias_blk, pool_v):
    db, n_pages = page_table.shape
    page = pool_v.shape[4]
    pps = PAGES_PER_STEP
    assert n_pages % pps == 0 and page == LANES
    ns = n_pages // pps
    past = n_pages * page
    v_spec = lambda k: pl.BlockSpec((1, 1, N_HEADS, HEAD_DIM, page),
                                    lambda b, s, pt, sl: (0, pt[b, jnp.maximum(s - 1, 0) * pps + k], 0, 0, 0))
    per_seq = lambda shape: pl.BlockSpec((1,) + shape, lambda b, s, pt, sl: (b, 0, 0))
    grid_spec = pltpu.PrefetchScalarGridSpec(
        num_scalar_prefetch=2,
        grid=(db, ns + 1),
        in_specs=([per_seq((N_HEADS, past))] + [per_seq((1, WIDTH))] * 3 + [per_seq((N_HEADS, LANES))]
                  + [v_spec(k) for k in range(pps)]),
        out_specs=per_seq((1, WIDTH)),
        scratch_shapes=[pltpu.VMEM((ns, N_HEADS, pps * page), F32), pltpu.VMEM((N_HEADS, LANES), F32),
                        pltpu.VMEM((N_HEADS, WIDTH), F32)],
    )
    return pl.pallas_call(
        functools.partial(_moba_sample_kernel, page=page),
        grid_spec=grid_spec,
        out_shape=jax.ShapeDtypeStruct((db, 1, WIDTH), F32),
        compiler_params=_cp(("arbitrary", "arbitrary")),
        name="moba_sample_attention",
    )(page_table, slopes, logits, q_s, k_new, v_new, bias_blk, *([pool_v] * pps))


def _alibi_slopes():
    n = 2 * N_HEADS
    i = jnp.arange(1, n + 1, dtype=F32)
    m = jnp.exp2(-8.0 * i / n)
    return m[0::2], m[1::2]


def kernel(x_prompt, x_sample, cache_moba_k, cache_moba_v, cache_dsa_k, cache_dsa_v, cache_dsa_kidx, state_ffn_conv, page_table, c_prompt, c_sample, w_ada, b_ada, g_attn, w_in, qn_a, kn_a, qn_b, kn_b, w_out, g_ffn, w_up, conv_w, conv_b, w_down):
    bsz, t, d = x_prompt.shape
    db, ds, _ = x_sample.shape
    depth = w_in.shape[0]
    assert depth == 1 and ds == 1 and t % SB == 0
    f = w_down.shape[1]
    nchunk = f // FF_CHUNK
    assert nchunk * FF_CHUNK == f
    slopes_a, slopes_b = _alibi_slopes()
    l = 0

    ncols = N_QKV_CHUNKS * WIDTH
    wqkv = w_in[l][:, :ncols].astype(BF)
    wkw = jnp.pad(w_in[l][:, ncols:], ((0, 0), (0, LANES - (w_in.shape[2] - ncols)))).astype(BF)
    nrm = jnp.stack([jnp.tile(g[l], N_HEADS) for g in (qn_a, kn_a, qn_b, kn_b)])
    hid = jnp.arange(WIDTH) // HEAD_DIM
    gmat = (hid[:, None] == hid[None, :]).astype(BF)
    wout = w_out[l].astype(BF)
    wupa = w_up[l][:, :f].astype(BF).reshape(d, nchunk, FF_CHUNK).transpose(1, 0, 2)
    wupg = w_up[l][:, f:].astype(BF).reshape(d, nchunk, FF_CHUNK).transpose(1, 0, 2)
    wdn = w_down[l].astype(BF).reshape(nchunk, FF_CHUNK, d)
    g_attn_l = g_attn[l].reshape(1, d)
    g_ffn_l = g_ffn[l].reshape(1, d)
    cb = conv_b[l].reshape(1, f)

    rc = bsz + db
    rpad = -rc % SUBLANES
    c_all = jnp.concatenate([c_prompt, c_sample, jnp.zeros((rpad, d), F32)], axis=0)
    mod = _modulation(c_all, w_ada[l], b_ada[l])
    mods_p = [m.reshape(bsz, 1, d) for m in jnp.split(mod[:bsz], N_MOD, axis=-1)]
    mods_s = [m.reshape(1, db, d) for m in jnp.split(mod[bsz:rc], N_MOD, axis=-1)]

    n = bsz * t
    xp2 = x_prompt.reshape(n, d)
    tm = 512
    (qa_bf, qb_bf, qi, kw, kat, vat, kbt, vbt, kat_bf, vat_bf, kbt_bf, vbt_bf, kwt_bf, kit, kmean) = _pre_project_prompt(
        xp2, mods_p[0], mods_p[1], g_attn_l, wqkv, wkw, nrm, gmat, tm=tm, bsz=bsz, t=t)
    nb = t // TQ
    assert nb <= LANES
    kmt = kmean.reshape(bsz, nb, WIDTH).transpose(0, 2, 1).astype(BF)
    kmt = jnp.pad(kmt, ((0, 0), (0, 0), (0, LANES - nb)))
    o_a = _moba_prompt(slopes_a, qa_bf, kat_bf, vat_bf, kmt, bsz, t)
    n_sel = min(DSA_TOPK, t // 4)
    o_b = _dsa_prompt(slopes_b, qi, kw, kwt_bf, qb_bf, kbt_bf, vbt_bf, n_sel, bsz, t)
    y_p, conv_p = _finish(xp2, o_a, o_b, mods_p[2:], g_ffn_l, wout, wupa, wupg, conv_w[l], cb, wdn,
                          tm=tm, rows_per_group=t)

    kv_out = lambda a: a.reshape(1, bsz, N_HEADS, HEAD_DIM, t).transpose(0, 1, 4, 2, 3)
    outs_p = (kv_out(kat), kv_out(vat), kv_out(kbt), kv_out(vbt),
              kit.reshape(1, bsz, IDX_DIM, t).transpose(0, 1, 3, 2), conv_p[None])

    page = cache_moba_k.shape[2]
    n_pages = page_table.shape[1]
    past = n_pages * page
    assert past % MOBA_BLOCK == 0 and db % SUBLANES == 0 and cache_moba_k.shape[0] == 1
    xs2 = x_sample.reshape(db, d)
    qa_s, ka_s, va_s, qb_s, kb_s, vb_s, qi_s, kw_s = _pre_project(
        xs2, mods_s[0], mods_s[1], g_attn_l, wqkv, wkw, nrm, gmat, tm=db, rows_per_group=db, with_kmean=False)
    pool = lambda cch: cch.transpose(0, 1, 3, 4, 2)
    pool_kidx = cache_dsa_kidx.transpose(0, 1, 3, 2)
    row3 = lambda a: a.reshape(db, 1, a.shape[-1])
    scores, score_new, bias_blk, logits_a = _sample_stats(
        page_table, qi_s.reshape(db, IDX_HEADS, IDX_DIM), kw_s[:, IDX_DIM:IDX_DIM + IDX_HEADS].reshape(db, IDX_HEADS, 1),
        row3(kw_s[:, :IDX_DIM]), row3(qa_s), pool_kidx, pool(cache_moba_k))
    n_sel_s = min(DSA_TOPK, (past + ds) // 4)
    bias_pos, bias_new = _dsa_sample_select(scores.reshape(db, past), score_new.reshape(db, LANES), n_sel_s)
    o_a_s = _moba_sample_attention(page_table, slopes_a, logits_a, row3(qa_s), row3(ka_s), row3(va_s), bias_blk,
                                   pool(cache_moba_v))
    o_b_s = _paged_attention(page_table, slopes_b, row3(qb_s), row3(kb_s), row3(vb_s),
                             bias_pos.reshape(db, 1, past), bias_new.reshape(db, 1, LANES),
                             pool(cache_dsa_k), pool(cache_dsa_v))
    state = state_ffn_conv[l]
    y_s, a_s = _finish(xs2, o_a_s.reshape(db, WIDTH), o_b_s.reshape(db, WIDTH), mods_s[2:], g_ffn_l, wout, wupa, wupg,
                       conv_w[l], cb, wdn, tm=db, rows_per_group=db, prefix=(state[:, 0], state[:, 1]))
    conv_s = jnp.stack([state[:, 1], a_s], axis=1)
    kvs_shape = (1, db, ds, N_HEADS, HEAD_DIM)
    outs_s = (ka_s.reshape(kvs_shape), va_s.reshape(kvs_shape), kb_s.reshape(kvs_shape), vb_s.reshape(kvs_shape),
              kw_s[:, :IDX_DIM].reshape(1, db, ds, IDX_DIM), conv_s[None])
    return (y_p.reshape(bsz, t, d), y_s.reshape(db, ds, d)) + outs_p + outs_s
```

```python
import functools

import jax
import jax.numpy as jnp
from jax import lax
from jax.experimental import pallas as pl
from jax.experimental.pallas import tpu as pltpu

HEAD_DIM = 64
N_HEADS = 8
WIDTH = N_HEADS * HEAD_DIM
MOBA_BLOCK = 256
MOBA_TOPK = 3
DSA_TOPK = 256
IDX_HEADS = 8
IDX_DIM = 64
CONV_WIDTH = 3
EPS = 1e-6
N_MOD = 6
ATTN_SCALE = HEAD_DIM ** -0.5
N_QKV_CHUNKS = 7
LANES = 128
SUBLANES = 8
TQ = MOBA_BLOCK
SB = 2048
SEL_CHUNK = 1024
BPS = SB // TQ
FF_CHUNK = 256
PAGES_PER_STEP = 16
VMEM_LIMIT = 56 * 1024 * 1024

BF = jnp.bfloat16
F32 = jnp.float32
I32 = jnp.int32
NEG_INF = float("-inf")
INT_MIN = -2 ** 31


def _cp(sem):
    return pltpu.CompilerParams(dimension_semantics=sem, vmem_limit_bytes=VMEM_LIMIT)


def _const_spec(shape):
    nd = len(shape)
    return pl.BlockSpec(shape, lambda *_: (0,) * nd)


def _dot_nt(a, b):
    return lax.dot_general(a, b, (((1,), (1,)), ((), ())), preferred_element_type=F32)


def _mod_kernel(c_ref, w_ref, b_ref, o_ref):
    s = jax.nn.silu(c_ref[...])
    o_ref[...] = jnp.dot(s.astype(BF), w_ref[...].astype(BF), preferred_element_type=F32) + b_ref[...]


def _modulation(c_all, w_ada, b_ada):
    r, d = c_all.shape
    n = w_ada.shape[1]
    tn = 1024
    return pl.pallas_call(
        _mod_kernel,
        grid=(n // tn,),
        in_specs=[pl.BlockSpec((r, d), lambda j: (0, 0)),
                  pl.BlockSpec((d, tn), lambda j: (0, j)),
                  pl.BlockSpec((1, tn), lambda j: (0, j))],
        out_specs=pl.BlockSpec((r, tn), lambda j: (0, j)),
        out_shape=jax.ShapeDtypeStruct((r, n), F32),
        compiler_params=_cp(("arbitrary",)),
        name="modulation",
    )(c_all, w_ada, b_ada.reshape(1, n))


def _pre_kernel(x_ref, shift_ref, scale_ref, g_ref, wqkv_ref, wkw_ref, nrm_ref, gmat_ref, *outs, tm, with_kmean):
    x = x_ref[...]
    ms = jnp.mean(x * x, axis=-1, keepdims=True)
    xn = x * lax.rsqrt(ms + EPS) * g_ref[...]
    h = (xn * (1.0 + scale_ref[0]) + shift_ref[0]).astype(BF)
    norm_row = {0: 0, 1: 1, 3: 2, 4: 3}
    for c in range(N_QKV_CHUNKS):
        p = jnp.dot(h, wqkv_ref[:, c * WIDTH:(c + 1) * WIDTH], preferred_element_type=F32)
        if c in norm_row:
            ss = jnp.dot((p * p).astype(BF), gmat_ref[...], preferred_element_type=F32)
            k = norm_row[c]
            p = p * lax.rsqrt(ss * (1.0 / HEAD_DIM) + EPS) * nrm_ref[k:k + 1, :]
        outs[c][...] = p
        if c == 1 and with_kmean:
            outs[N_QKV_CHUNKS + 1][0] = jnp.mean(p.reshape(tm // MOBA_BLOCK, MOBA_BLOCK, WIDTH), axis=1)
    outs[N_QKV_CHUNKS][...] = jnp.dot(h, wkw_ref[...], preferred_element_type=F32)


def _pre_prompt_kernel(x_ref, shift_ref, scale_ref, g_ref, wqkv_ref, wkw_ref, nrm_ref, gmat_ref,
                       qa_ref, qb_ref, qi_ref, kw_ref, kat_ref, vat_ref, kbt_ref, vbt_ref,
                       katb_ref, vatb_ref, kbtb_ref, vbtb_ref, kwtb_ref, kit_ref, km_ref, *, tm):
    x = x_ref[...]
    ms = jnp.mean(x * x, axis=-1, keepdims=True)
    xn = x * lax.rsqrt(ms + EPS) * g_ref[...]
    h = (xn * (1.0 + scale_ref[0]) + shift_ref[0]).astype(BF)

    def proj(c, norm):
        p = jnp.dot(h, wqkv_ref[:, c * WIDTH:(c + 1) * WIDTH], preferred_element_type=F32)
        if norm is not None:
            ss = jnp.dot((p * p).astype(BF), gmat_ref[...], preferred_element_type=F32)
            p = p * lax.rsqrt(ss * (1.0 / HEAD_DIM) + EPS) * nrm_ref[norm:norm + 1, :]
        return p

    qa_ref[...] = (proj(0, 0) * ATTN_SCALE).astype(BF)
    ka = proj(1, 1)
    km_ref[0] = jnp.mean(ka.reshape(tm // MOBA_BLOCK, MOBA_BLOCK, WIDTH), axis=1)
    for p, t_ref, tb_ref in ((ka, kat_ref, katb_ref), (proj(2, None), vat_ref, vatb_ref)):
        pt = p.T
        t_ref[0] = pt
        tb_ref[0] = pt.astype(BF)
    qb_ref[...] = (proj(3, 2) * ATTN_SCALE).astype(BF)
    for p, t_ref, tb_ref in ((proj(4, 3), kbt_ref, kbtb_ref), (proj(5, None), vbt_ref, vbtb_ref)):
        pt = p.T
        t_ref[0] = pt
        tb_ref[0] = pt.astype(BF)
    qi_ref[...] = proj(6, None)
    kw = jnp.dot(h, wkw_ref[...], preferred_element_type=F32)
    kw_ref[...] = kw
    kwt = kw.T
    kwtb_ref[0] = kwt.astype(BF)
    kit_ref[0] = kwt[:IDX_DIM, :]


def _pre_project_prompt(x2, shift, scale, g_attn, wqkv, wkw, nrm, gmat, *, tm, bsz, t):
    n, d = x2.shape
    tpb = t // tm
    mod_spec = pl.BlockSpec((1, 1, d), lambda i: (i // tpb, 0, 0))
    row_spec = lambda w: pl.BlockSpec((tm, w), lambda i: (i, 0))
    tr_spec = lambda r: pl.BlockSpec((1, r, tm), lambda i: (i // tpb, 0, i % tpb))
    nb = tm // MOBA_BLOCK
    sds = jax.ShapeDtypeStruct
    out_shape = ([sds((n, WIDTH), BF)] * 2 + [sds((n, WIDTH), F32), sds((n, LANES), F32)]
                 + [sds((bsz, WIDTH, t), F32)] * 4 + [sds((bsz, WIDTH, t), BF)] * 4
                 + [sds((bsz, LANES, t), BF), sds((bsz, IDX_DIM, t), F32), sds((n // tm, nb, WIDTH), F32)])
    out_specs = ([row_spec(WIDTH)] * 3 + [row_spec(LANES)] + [tr_spec(WIDTH)] * 8
                 + [tr_spec(LANES), tr_spec(IDX_DIM), pl.BlockSpec((1, nb, WIDTH), lambda i: (i, 0, 0))])
    return pl.pallas_call(
        functools.partial(_pre_prompt_kernel, tm=tm),
        grid=(n // tm,),
        in_specs=[row_spec(d), mod_spec, mod_spec, _const_spec((1, d)),
                  _const_spec(wqkv.shape), _const_spec(wkw.shape), _const_spec(nrm.shape), _const_spec(gmat.shape)],
        out_specs=out_specs,
        out_shape=out_shape,
        compiler_params=_cp(("arbitrary",)),
        name="pre_project_prompt",
    )(x2, shift, scale, g_attn, wqkv, wkw, nrm, gmat)


def _pre_project(x2, shift, scale, g_attn, wqkv, wkw, nrm, gmat, *, tm, rows_per_group, with_kmean):
    n, d = x2.shape
    grid = (n // tm,)
    mr = shift.shape[1]
    tiles_per_group = rows_per_group // tm
    mod_spec = pl.BlockSpec((1, mr, d), lambda i: (i // tiles_per_group, 0, 0))
    row_spec = lambda w: pl.BlockSpec((tm, w), lambda i: (i, 0))
    out_shape = [jax.ShapeDtypeStruct((n, WIDTH), F32)] * N_QKV_CHUNKS + [jax.ShapeDtypeStruct((n, LANES), F32)]
    out_specs = [row_spec(WIDTH)] * N_QKV_CHUNKS + [row_spec(LANES)]
    if with_kmean:
        nb = tm // MOBA_BLOCK
        out_shape.append(jax.ShapeDtypeStruct((n // tm, nb, WIDTH), F32))
        out_specs.append(pl.BlockSpec((1, nb, WIDTH), lambda i: (i, 0, 0)))
    return pl.pallas_call(
        functools.partial(_pre_kernel, tm=tm, with_kmean=with_kmean),
        grid=grid,
        in_specs=[row_spec(d), mod_spec, mod_spec, _const_spec((1, d)),
                  _const_spec(wqkv.shape), _const_spec(wkw.shape), _const_spec(nrm.shape), _const_spec(gmat.shape)],
        out_specs=out_specs,
        out_shape=out_shape,
        compiler_params=_cp(("arbitrary",)),
        name="pre_project",
    )(x2, shift, scale, g_attn, wqkv, wkw, nrm, gmat)


def _pair_half(shape, axis):
    return lax.broadcasted_iota(I32, shape, axis) // HEAD_DIM


def _two_pass_attention(q_pair, par, kt_ref, vt_ref, slope, i, bias_at, l_ref, m_ref, acc_ref):
    n_steps = i // BPS + 1
    q = jnp.where(_pair_half((TQ, LANES), 1) == par, q_pair, jnp.zeros_like(q_pair))
    own_rows = _pair_half((LANES, SB), 0) == par
    m_ref[...] = jnp.full((TQ, TQ), NEG_INF, F32)

    def keys(ref, j):
        return ref[0, :, pl.ds(pl.multiple_of(j * SB, SB), SB)]

    def pass1(j, carry):
        cols = lax.broadcasted_iota(I32, (1, SB), 1) + (j * SB - i * TQ)
        l = jnp.dot(q, keys(kt_ref, j), preferred_element_type=F32) + slope * cols.astype(F32) + bias_at(j)
        l_ref[j] = l
        mx = m_ref[...]
        for k in range(BPS):
            mx = jnp.maximum(mx, l[:, k * TQ:(k + 1) * TQ])
        m_ref[...] = mx
        return carry

    lax.fori_loop(0, n_steps, pass1, 0)
    m = jnp.max(m_ref[...], axis=-1, keepdims=True)
    acc_ref[...] = jnp.zeros((TQ, LANES), F32)

    def pass2(j, carry):
        p = jnp.exp((l_ref[j] - m).astype(BF))
        vt = jnp.where(own_rows, keys(vt_ref, j), jnp.ones((LANES, SB), BF))
        acc_ref[...] += _dot_nt(p, vt)
        return carry

    lax.fori_loop(0, n_steps, pass2, 0)
    acc = acc_ref[...]
    return acc / pltpu.roll(acc, HEAD_DIM, axis=1)


def _store_pair_half(o_ref, par, val):
    keep = _pair_half((TQ, LANES), 1) == par

    @pl.when(par == 0)
    def _():
        o_ref[...] = jnp.where(keep, val, 0.0)

    @pl.when(par != 0)
    def _():
        o_ref[...] = jnp.where(keep, val, o_ref[...])


def _moba_prompt_kernel(slopes_ref, q_ref, kt_ref, vt_ref, kmt_ref, o_ref, l_ref, m_ref, acc_ref, bias_ref, sel_ref):
    pr = pl.program_id(1)
    i = pl.program_id(2)
    par = pl.program_id(3)
    sub = i % BPS
    row0 = pl.multiple_of(sub * TQ, TQ)

    @pl.when(sub == 0)
    def _():
        qs = q_ref[...]
        qs = jnp.where(_pair_half(qs.shape, 1) == par, qs, jnp.zeros_like(qs))
        s = jnp.dot(qs, kmt_ref[0], preferred_element_type=F32)
        lane_s = lax.broadcasted_iota(I32, s.shape, 1)
        own = i + lax.broadcasted_iota(I32, s.shape, 0) // TQ
        tri = (lax.broadcasted_iota(I32, (LANES, LANES), 0) <= lax.broadcasted_iota(I32, (LANES, LANES), 1))
        tri = jnp.where(tri, 1.0, 0.0).astype(BF)
        s = jnp.where(lane_s < own, s, NEG_INF)
        bias = jnp.full(s.shape, NEG_INF, F32)
        for _ in range(MOBA_TOPK):
            mx = jnp.max(s, axis=-1, keepdims=True)
            eq = s == mx
            rank = jnp.dot(jnp.where(eq, 1.0, 0.0).astype(BF), tri, preferred_element_type=F32)
            pick = eq & (rank == 1.0) & (mx > NEG_INF)
            bias = jnp.where(pick, 0.0, bias)
            s = jnp.where(pick, NEG_INF, s)
        sel_ref[par] = bias

    q = q_ref[pl.ds(row0, TQ), :]
    bias_ref[...] = sel_ref[par, pl.ds(row0, TQ), :]
    lane = lax.broadcasted_iota(I32, (TQ, LANES), 1)
    r = lax.broadcasted_iota(I32, (TQ, TQ), 0)
    c = lax.broadcasted_iota(I32, (TQ, TQ), 1)
    causal = jnp.where(c <= r, 0.0, NEG_INF).astype(F32)

    def bias_at(j):
        parts = []
        for k in range(BPS):
            blk = j * BPS + k
            col = jnp.max(jnp.where(lane == blk, bias_ref[...], NEG_INF), axis=-1, keepdims=True)
            parts.append(jnp.where(blk == i, causal, jnp.broadcast_to(col, (TQ, TQ))))
        return jnp.concatenate(parts, axis=1)

    out = _two_pass_attention(q, par, kt_ref, vt_ref, slopes_ref[2 * pr + par], i, bias_at, l_ref, m_ref, acc_ref)
    _store_pair_half(o_ref, par, out)


def _moba_prompt(slopes, q, kt, vt, kmt, bsz, t):
    nq = t // TQ
    nsb = t // SB
    npair = N_HEADS // 2
    grid_spec = pltpu.PrefetchScalarGridSpec(
        num_scalar_prefetch=1,
        grid=(bsz, npair, nq, 2),
        in_specs=[pl.BlockSpec((SB, LANES), lambda bi, pr, i, par, s: (bi * nsb + i // BPS, pr)),
                  pl.BlockSpec((1, LANES, t), lambda bi, pr, i, par, s: (bi, pr, 0)),
                  pl.BlockSpec((1, LANES, t), lambda bi, pr, i, par, s: (bi, pr, 0)),
                  pl.BlockSpec((1, LANES, LANES), lambda bi, pr, i, par, s: (bi, pr, 0))],
        out_specs=pl.BlockSpec((TQ, LANES), lambda bi, pr, i, par, s: (bi * nq + i, pr)),
        scratch_shapes=[pltpu.VMEM((nsb, TQ, SB), F32), pltpu.VMEM((TQ, TQ), F32),
                        pltpu.VMEM((TQ, LANES), F32), pltpu.VMEM((TQ, LANES), F32), pltpu.VMEM((2, SB, LANES), F32)],
    )
    return pl.pallas_call(
        _moba_prompt_kernel,
        grid_spec=grid_spec,
        out_shape=jax.ShapeDtypeStruct((bsz * t, WIDTH), F32),
        compiler_params=_cp(("arbitrary",) * 4),
        name="moba_prompt",
    )(slopes, q, kt, vt, kmt)


def _threshold_from_code(code):
    bits = code ^ ((code >> 31) & 0x7FFFFFFF)
    return lax.bitcast_convert_type(bits, F32)


def _dsa_select(i, n_sel, idx_bits, sk_ref, b_ref, jc_ref):
    n_chunks = (i * TQ) // SEL_CHUNK + 1
    per_step = SB // SEL_CHUNK
    rc = TQ // 2
    c = lax.broadcasted_iota(I32, (rc, SEL_CHUNK), 1)

    def chunk(ref, j, rows):
        return ref.at[j // per_step, rows, pl.ds(pl.multiple_of((j % per_step) * SEL_CHUNK, SEL_CHUNK), SEL_CHUNK)]

    for r0 in range(0, TQ, rc):
        rows = slice(r0, r0 + rc)
        gr = lax.broadcasted_iota(I32, (rc, SEL_CHUNK), 0) + (i * TQ + r0)

        def count(pred, rows=rows):
            def body(j, acc):
                hit = jnp.where(pred(j, chunk(sk_ref, j, rows)[...]), 1.0, 0.0)
                for k in range(SEL_CHUNK // LANES):
                    acc = acc + hit[:, k * LANES:(k + 1) * LANES]
                return acc

            acc = lax.fori_loop(0, n_chunks, body, jnp.zeros((rc, LANES), F32))
            return jnp.sum(acc, axis=-1, keepdims=True)

        def bit_step(it, carry, count=count):
            v, cnt_v = carry
            cand = v + jnp.left_shift(jnp.int32(1), 31 - it)
            cand_f = _threshold_from_code(cand)
            cnt = count(lambda j, sc: sc >= cand_f)
            keep = cnt >= n_sel
            return jnp.where(keep, cand, v), jnp.where(keep, cnt, cnt_v)

        v, cnt_ge = lax.fori_loop(0, 32, bit_step, (jnp.full((rc, 1), INT_MIN, I32), jnp.zeros((rc, 1), F32)))
        few = v == INT_MIN
        thr = jnp.where(few, NEG_INF, _threshold_from_code(v))
        cnt_gt = count(lambda j, sc: sc > thr)
        need = n_sel - cnt_gt
        trim = (cnt_ge > n_sel) & jnp.logical_not(few)
        jc_ref[rows, :] = jnp.full((rc, LANES), 2 ** idx_bits, I32)

        @pl.when(jnp.max(jnp.where(trim, 1.0, 0.0)) > 0.0)
        def _(rows=rows, count=count, thr=thr, need=need, trim=trim):
            def idx_step(it, jcap):
                cand = jcap + jnp.left_shift(jnp.int32(1), idx_bits - 1 - it)
                cnt = count(lambda j, sc: (sc == thr) & (c + j * SEL_CHUNK < cand))
                return jnp.where(cnt < need, cand, jcap)

            jcap = lax.fori_loop(0, idx_bits, idx_step, jnp.zeros((rc, 1), I32))
            jc_ref[rows, :] = jnp.broadcast_to(jnp.where(trim, jcap, 2 ** idx_bits), (rc, LANES))

        jcap = jc_ref[rows, 0:1]

        def write(j, carry, rows=rows, thr=thr, jcap=jcap, gr=gr):
            sc = chunk(sk_ref, j, rows)[...]
            gc = c + j * SEL_CHUNK
            sel = ((sc > thr) | ((sc == thr) & (gc <= jcap))) & (gc <= gr)
            chunk(b_ref, j, rows)[...] = jnp.where(sel, 0.0, NEG_INF)
            return carry

        lax.fori_loop(0, (i // BPS + 1) * per_step, write, 0)


def _dsa_prompt_kernel(slopes_ref, qi_ref, kw_ref, kwt_ref, q_ref, kt_ref, vt_ref, o_ref,
                       sk_ref, b_ref, l_ref, m_ref, acc_ref, jc_ref, qz_ref, *, n_sel, idx_bits):
    i = pl.program_id(1)
    h = pl.program_id(2)
    par = h % 2

    @pl.when(h == 0)
    def _():
        wi = kw_ref[...] * (IDX_HEADS * IDX_DIM) ** -0.5
        wcols = [wi[:, IDX_DIM + hh:IDX_DIM + hh + 1] for hh in range(IDX_HEADS)]
        gr = lax.broadcasted_iota(I32, (TQ, SB), 0) + i * TQ
        c = lax.broadcasted_iota(I32, (TQ, SB), 1)
        first = _pair_half((TQ, LANES), 1) == 0
        for hh in range(IDX_HEADS):
            qp = qi_ref[:, (hh // 2) * LANES:(hh // 2 + 1) * LANES]
            if hh % 2:
                qp = pltpu.roll(qp, HEAD_DIM, axis=1)
            qz_ref[hh] = jnp.where(first, qp, 0.0).astype(BF)

        def score_step(j, carry):
            kit = kwt_ref[0, :, pl.ds(pl.multiple_of(j * SB, SB), SB)]
            sc = jnp.maximum(jnp.dot(qz_ref[0], kit, preferred_element_type=F32), 0.0) * wcols[0]
            for hh in range(1, IDX_HEADS):
                sc = sc + jnp.maximum(jnp.dot(qz_ref[hh], kit, preferred_element_type=F32), 0.0) * wcols[hh]
            sk_ref[j] = jnp.where(c + j * SB <= gr, sc, NEG_INF)
            return carry

        lax.fori_loop(0, i // BPS + 1, score_step, 0)
        _dsa_select(i, n_sel, idx_bits, sk_ref, b_ref, jc_ref)

    out = _two_pass_attention(q_ref[...], par, kt_ref, vt_ref, slopes_ref[h], i, lambda j: b_ref[j],
                              l_ref, m_ref, acc_ref)
    _store_pair_half(o_ref, par, out)


def _dsa_prompt(slopes, qi, kw, kwt, q, kt, vt, n_sel, bsz, t):
    nq = t // TQ
    nsb = t // SB
    idx_bits = max(1, (t - 1).bit_length())
    blk = lambda shape, fn: pl.BlockSpec(shape, fn)
    grid_spec = pltpu.PrefetchScalarGridSpec(
        num_scalar_prefetch=1,
        grid=(bsz, nq, N_HEADS),
        in_specs=[blk((TQ, WIDTH), lambda bi, i, hi, s: (bi * nq + i, 0)),
                  blk((TQ, LANES), lambda bi, i, hi, s: (bi * nq + i, 0)),
                  blk((1, LANES, t), lambda bi, i, hi, s: (bi, 0, 0)),
                  blk((TQ, LANES), lambda bi, i, hi, s: (bi * nq + i, hi // 2)),
                  blk((1, LANES, t), lambda bi, i, hi, s: (bi, hi // 2, 0)),
                  blk((1, LANES, t), lambda bi, i, hi, s: (bi, hi // 2, 0))],
        out_specs=blk((TQ, LANES), lambda bi, i, hi, s: (bi * nq + i, hi // 2)),
        scratch_shapes=[pltpu.VMEM((nsb, TQ, SB), F32), pltpu.VMEM((nsb, TQ, SB), F32), pltpu.VMEM((nsb, TQ, SB), F32),
                        pltpu.VMEM((TQ, TQ), F32), pltpu.VMEM((TQ, LANES), F32), pltpu.VMEM((TQ, LANES), I32),
                        pltpu.VMEM((IDX_HEADS, TQ, LANES), BF)],
    )
    return pl.pallas_call(
        functools.partial(_dsa_prompt_kernel, n_sel=n_sel, idx_bits=idx_bits),
        grid_spec=grid_spec,
        out_shape=jax.ShapeDtypeStruct((bsz * t, WIDTH), F32),
        compiler_params=_cp(("arbitrary", "arbitrary", "arbitrary")),
        name="dsa_prompt",
    )(slopes, qi, kw, kwt, q, kt, vt)


def _finish_kernel(*refs, tm, nchunk, seq_mode, tiles_per_seq):
    (x_ref, oa_ref, ob_ref, gate_a_ref, shift_f_ref, scale_f_ref, gate_f_ref, g_ref, wout_ref,
     wupa_ref, wupg_ref, cw_ref, cb_ref, wdn_ref) = refs[:14]
    if seq_mode:
        y_ref, st_ref, abuf_ref, carry_ref, yacc_ref = refs[14:]
    else:
        pre0_ref, pre1_ref, y_ref, st_ref, yacc_ref = refs[14:]
    i = pl.program_id(0)
    o = (jnp.dot(oa_ref[...].astype(BF), wout_ref[0:WIDTH, :], preferred_element_type=F32)
         + jnp.dot(ob_ref[...].astype(BF), wout_ref[WIDTH:2 * WIDTH, :], preferred_element_type=F32))
    x1 = x_ref[...] + gate_a_ref[0] * o
    ms = jnp.mean(x1 * x1, axis=-1, keepdims=True)
    h = ((x1 * lax.rsqrt(ms + EPS) * g_ref[...]) * (1.0 + scale_f_ref[0]) + shift_f_ref[0]).astype(BF)
    yacc_ref[...] = jnp.zeros_like(yacc_ref)
    for c in range(nchunk):
        cols = slice(c * FF_CHUNK, (c + 1) * FF_CHUNK)
        a = jnp.dot(h, wupa_ref[c], preferred_element_type=F32)
        gt = jnp.dot(h, wupg_ref[c], preferred_element_type=F32)
        if seq_mode:
            first = (i % tiles_per_seq) == 0
            buf = abuf_ref.at[c % 2]
            buf[0:SUBLANES, :] = jnp.where(first, 0.0, carry_ref[c])
            buf[SUBLANES:SUBLANES + tm, :] = a
            a1 = buf[SUBLANES - 1:SUBLANES - 1 + tm, :]
            a2 = buf[SUBLANES - 2:SUBLANES - 2 + tm, :]
            carry_ref[c] = a[tm - SUBLANES:tm, :]
            st_ref[0, :, cols] = a[tm - (CONV_WIDTH - 1):tm, :]
        else:
            a2 = pre0_ref[:, cols]
            a1 = pre1_ref[:, cols]
            st_ref[:, cols] = a
        conv = cb_ref[:, cols] + a2 * cw_ref[0:1, cols]
        conv = conv + a1 * cw_ref[1:2, cols]
        conv = conv + a * cw_ref[2:3, cols]
        act = (jax.nn.gelu(conv) * gt).astype(BF)
        yacc_ref[...] += jnp.dot(act, wdn_ref[c], preferred_element_type=F32)
    y_ref[...] = x1 + gate_f_ref[0] * yacc_ref[...]


def _finish(x2, oa, ob, mods, g_ffn, wout, wupa, wupg, conv_w, conv_b, wdn, *, tm, rows_per_group, prefix=None):
    n, d = x2.shape
    nchunk = wupa.shape[0]
    f = nchunk * FF_CHUNK
    seq_mode = prefix is None
    tiles_per_seq = rows_per_group // tm
    mr = mods[0].shape[1]
    mod_spec = pl.BlockSpec((1, mr, d), lambda i: (i // tiles_per_seq, 0, 0))
    row_spec = lambda w: pl.BlockSpec((tm, w), lambda i: (i, 0))
    in_specs = ([row_spec(d), row_spec(WIDTH), row_spec(WIDTH)] + [mod_spec] * 4 +
                [_const_spec((1, d)), _const_spec(wout.shape), _const_spec(wupa.shape), _const_spec(wupg.shape),
                 _const_spec(conv_w.shape), _const_spec((1, f)), _const_spec(wdn.shape)])
    args = [x2, oa, ob, *mods, g_ffn, wout, wupa, wupg, conv_w, conv_b, wdn]
    scratch = []
    if seq_mode:
        nseq = n // rows_per_group
        st_shape = jax.ShapeDtypeStruct((nseq, CONV_WIDTH - 1, f), F32)
        st_spec = pl.BlockSpec((1, CONV_WIDTH - 1, f), lambda i: (i // tiles_per_seq, 0, 0))
        scratch += [pltpu.VMEM((2, SUBLANES + tm, FF_CHUNK), F32), pltpu.VMEM((nchunk, SUBLANES, FF_CHUNK), F32)]
    else:
        in_specs += [row_spec(f), row_spec(f)]
        args += list(prefix)
        st_shape = jax.ShapeDtypeStruct((n, f), F32)
        st_spec = row_spec(f)
    scratch.append(pltpu.VMEM((tm, d), F32))
    return pl.pallas_call(
        functools.partial(_finish_kernel, tm=tm, nchunk=nchunk, seq_mode=seq_mode, tiles_per_seq=tiles_per_seq),
        grid=(n // tm,),
        in_specs=in_specs,
        out_specs=[row_spec(d), st_spec],
        out_shape=[jax.ShapeDtypeStruct((n, d), F32), st_shape],
        scratch_shapes=scratch,
        compiler_params=_cp(("arbitrary",)),
        name="finish",
    )(*args)


def _block_diag_rows(row):
    hrow = lax.broadcasted_iota(I32, (N_HEADS, WIDTH), 0)
    hlane = lax.broadcasted_iota(I32, (N_HEADS, WIDTH), 1) // HEAD_DIM
    return jnp.where(hrow == hlane, jnp.broadcast_to(row, (N_HEADS, WIDTH)), 0.0)


def _sample_stats_kernel(pt_ref, qi_ref, wi_ref, kin_ref, qa_ref, *refs, page, nblk):
    pps = PAGES_PER_STEP
    kidx_refs = refs[:pps]
    kmoba_refs = refs[pps:2 * pps]
    sc_ref, scn_ref, bb_ref, lg_ref, km_ref = refs[2 * pps:]
    s = pl.program_id(1)
    qbd = _block_diag_rows(qa_ref[0] * ATTN_SCALE).astype(BF)
    qi = qi_ref[0].astype(BF)
    wi = wi_ref[0] * (IDX_HEADS * IDX_DIM) ** -0.5
    ppb = MOBA_BLOCK // page
    lane = lax.broadcasted_iota(I32, (WIDTH, LANES), 1)

    @pl.when(s == 0)
    def _():
        km_ref[...] = jnp.zeros(km_ref.shape, F32)

    kidx = jnp.concatenate([kidx_refs[k][0, 0].astype(BF) for k in range(pps)], axis=1)
    rel = jnp.maximum(jnp.dot(qi, kidx, preferred_element_type=F32), 0.0)
    sc_ref[0] = jnp.sum(rel * wi, axis=0, keepdims=True)
    for kb in range(pps // ppb):
        tot = kmoba_refs[kb * ppb][0, 0].reshape(WIDTH, page)
        for k in range(1, ppb):
            tot = tot + kmoba_refs[kb * ppb + k][0, 0].reshape(WIDTH, page)
        mean = jnp.sum(tot, axis=-1, keepdims=True) * (1.0 / MOBA_BLOCK)
        km_ref[...] = jnp.where(lane == s * (pps // ppb) + kb, mean, km_ref[...])
    kt = jnp.concatenate([r[0, 0].reshape(WIDTH, page).astype(BF) for r in kmoba_refs], axis=1)
    lg_ref[0] = jnp.dot(qbd, kt, preferred_element_type=F32)

    @pl.when(s == pl.num_programs(1) - 1)
    def _():
        kin = jnp.broadcast_to(kin_ref[0], (SUBLANES, IDX_DIM)).astype(BF)
        reln = jnp.maximum(_dot_nt(qi, kin), 0.0)
        scn = jnp.sum(reln[:, 0:1] * wi, axis=0, keepdims=True)
        scn_ref[0] = jnp.broadcast_to(scn, (1, LANES))
        sb = jnp.dot(qbd, km_ref[...].astype(BF), preferred_element_type=F32)
        lane_b = lax.broadcasted_iota(I32, sb.shape, 1)
        lanef = lane_b.astype(F32)
        sb = jnp.where(lane_b < nblk, sb, NEG_INF)
        bias = jnp.full(sb.shape, NEG_INF, F32)
        for _ in range(MOBA_TOPK):
            mx = jnp.max(sb, axis=-1, keepdims=True)
            first = jnp.min(jnp.where(sb == mx, lanef, float(LANES)), axis=-1, keepdims=True)
            pick = (lanef == first) & (mx > NEG_INF)
            bias = jnp.where(pick, 0.0, bias)
            sb = jnp.where(pick, NEG_INF, sb)
        bb_ref[0] = bias


def _sample_stats(page_table, qi_s, wi_s, ki_s, qa_s, pool_kidx, pool_mk):
    db, n_pages = page_table.shape
    page = pool_kidx.shape[3]
    pps = PAGES_PER_STEP
    past = n_pages * page
    nblk = past // MOBA_BLOCK
    assert n_pages % pps == 0 and pps % (MOBA_BLOCK // page) == 0 and nblk <= LANES and page == LANES
    kidx_spec = lambda k: pl.BlockSpec((1, 1, IDX_DIM, page), lambda b, s, pt: (0, pt[b, s * pps + k], 0, 0))
    kmoba_spec = lambda k: pl.BlockSpec((1, 1, N_HEADS, HEAD_DIM, page),
                                        lambda b, s, pt: (0, pt[b, s * pps + k], 0, 0, 0))
    per_seq = lambda shape: pl.BlockSpec((1,) + shape, lambda b, s, pt: (b, 0, 0))
    grid_spec = pltpu.PrefetchScalarGridSpec(
        num_scalar_prefetch=1,
        grid=(db, n_pages // pps),
        in_specs=([per_seq((IDX_HEADS, IDX_DIM)), per_seq((IDX_HEADS, 1)), per_seq((1, IDX_DIM)), per_seq((1, WIDTH))]
                  + [kidx_spec(k) for k in range(pps)] + [kmoba_spec(k) for k in range(pps)]),
        out_specs=[pl.BlockSpec((1, 1, pps * page), lambda b, s, pt: (b, 0, s)),
                   per_seq((1, LANES)), per_seq((N_HEADS, LANES)),
                   pl.BlockSpec((1, N_HEADS, pps * page), lambda b, s, pt: (b, 0, s))],
        scratch_shapes=[pltpu.VMEM((WIDTH, LANES), F32)],
    )
    return pl.pallas_call(
        functools.partial(_sample_stats_kernel, page=page, nblk=nblk),
        grid_spec=grid_spec,
        out_shape=[jax.ShapeDtypeStruct((db, 1, past), F32), jax.ShapeDtypeStruct((db, 1, LANES), F32),
                   jax.ShapeDtypeStruct((db, N_HEADS, LANES), F32), jax.ShapeDtypeStruct((db, N_HEADS, past), F32)],
        compiler_params=_cp(("arbitrary", "arbitrary")),
        name="sample_stats",
    )(page_table, qi_s, wi_s, ki_s, qa_s, *([pool_kidx] * pps), *([pool_mk] * pps))


def _dsa_sample_select_kernel(sc_ref, scn_ref, b_ref, bn_ref, *, n_sel, idx_bits):
    sc = sc_ref[...]
    scn = scn_ref[:, 0:1]
    db, past = sc.shape
    gidx = lax.broadcasted_iota(I32, sc.shape, 1)

    def count(pred, pred_new):
        return (jnp.sum(jnp.where(pred, 1.0, 0.0), axis=-1, keepdims=True) + jnp.where(pred_new, 1.0, 0.0))

    def bit_step(it, v):
        cand = v + jnp.left_shift(jnp.int32(1), 31 - it)
        cand_f = _threshold_from_code(cand)
        cnt = count(sc >= cand_f, scn >= cand_f)
        return jnp.where(cnt >= n_sel, cand, v)

    v = lax.fori_loop(0, 32, bit_step, jnp.full((db, 1), INT_MIN, I32))
    few = v == INT_MIN
    thr = jnp.where(few, NEG_INF, _threshold_from_code(v))
    cnt_gt = count(sc > thr, scn > thr)
    cnt_ge = count(sc >= thr, scn >= thr)
    need = n_sel - cnt_gt
    trim = (cnt_ge > n_sel) & jnp.logical_not(few)

    def idx_step(it, jcap):
        cand = jcap + jnp.left_shift(jnp.int32(1), idx_bits - 1 - it)
        cnt = count((sc == thr) & (gidx < cand), (scn == thr) & (past < cand))
        return jnp.where(cnt < need, cand, jcap)

    jcap = lax.fori_loop(0, idx_bits, idx_step, jnp.zeros((db, 1), I32))
    jcap = jnp.where(trim, jcap, 2 ** idx_bits)
    sel = (sc > thr) | ((sc == thr) & (gidx <= jcap))
    seln = (scn > thr) | ((scn == thr) & (past <= jcap))
    bn_ref[...] = jnp.broadcast_to(jnp.where(seln, 0.0, NEG_INF), bn_ref.shape)
    b_ref[...] = jnp.where(sel, 0.0, NEG_INF)


def _dsa_sample_select(scores, score_new, n_sel):
    db, past = scores.shape
    idx_bits = past.bit_length()
    return pl.pallas_call(
        functools.partial(_dsa_sample_select_kernel, n_sel=n_sel, idx_bits=idx_bits),
        out_shape=[jax.ShapeDtypeStruct((db, past), F32), jax.ShapeDtypeStruct((db, LANES), F32)],
        compiler_params=pltpu.CompilerParams(vmem_limit_bytes=VMEM_LIMIT),
        name="dsa_sample_select",
    )(scores, score_new)


def _paged_attention_kernel(pt_ref, slopes_ref, q_ref, kn_ref, vn_ref, bias_ref, bn_ref, *refs, page):
    pps = PAGES_PER_STEP
    k_refs = refs[:pps]
    v_refs = refs[pps:2 * pps]
    o_ref, l_ref, mx_ref, m_ref, ps_ref, acc_ref, sl_ref, qbd_ref = refs[2 * pps:]
    s = pl.program_id(1)
    ns = pl.num_programs(1) // 2
    past = ns * pps * page
    hrow = lax.broadcasted_iota(I32, (N_HEADS, LANES), 0)
    pages = lambda prefs: jnp.concatenate([r[0, 0].reshape(WIDTH, page).astype(BF) for r in prefs], axis=1)

    @pl.when(s == 0)
    def _():
        sl = jnp.zeros((N_HEADS, LANES), F32)
        for h in range(N_HEADS):
            sl = jnp.where(hrow == h, slopes_ref[h], sl)
        sl_ref[...] = sl
        mx_ref[...] = jnp.full(mx_ref.shape, NEG_INF, F32)
        qbd_ref[...] = _block_diag_rows(q_ref[0] * ATTN_SCALE).astype(BF)

    @pl.when(s < ns)
    def _():
        wide = pps * page
        pos = s * wide + lax.broadcasted_iota(I32, (1, wide), 1)
        logits = jnp.dot(qbd_ref[...], pages(k_refs), preferred_element_type=F32)
        logits = logits - sl_ref[:, 0:1] * (past - pos).astype(F32)
        logits = logits + bias_ref[0]
        l_ref[s] = logits
        mx_ref[...] = jnp.maximum(mx_ref[...], logits)

    @pl.when(s == ns - 1)
    def _():
        kn = kn_ref[0].astype(BF).astype(F32)
        ln = jnp.sum(qbd_ref[...].astype(F32) * kn, axis=-1, keepdims=True) + bn_ref[0, :, 0:1]
        m = jnp.maximum(jnp.max(mx_ref[...], axis=-1, keepdims=True), ln)
        m_ref[...] = jnp.broadcast_to(m, m_ref.shape)
        pn = jnp.exp(ln - m)
        ps_ref[...] = jnp.zeros(ps_ref.shape, F32)
        ps_ref[:, 0:1] = pn
        acc_ref[...] = pn.astype(BF).astype(F32) * vn_ref[0].astype(BF).astype(F32)

    @pl.when(s >= ns)
    def _():
        p = jnp.exp(l_ref[s - ns] - m_ref[:, 0:1])
        ps_ref[...] += p
        acc_ref[...] += _dot_nt(p.astype(BF), pages(v_refs))

    @pl.when(s == 2 * ns - 1)
    def _():
        o = acc_ref[...] / jnp.sum(ps_ref[...], axis=-1, keepdims=True)
        hr = lax.broadcasted_iota(I32, (N_HEADS, WIDTH), 0)
        hl = lax.broadcasted_iota(I32, (N_HEADS, WIDTH), 1) // HEAD_DIM
        o_ref[0] = jnp.sum(jnp.where(hr == hl, o, 0.0), axis=0, keepdims=True)


def _paged_attention(page_table, slopes, q_s, k_new, v_new, bias, bias_new, pool_k, pool_v):
    db, n_pages = page_table.shape
    page = pool_k.shape[4]
    pps = PAGES_PER_STEP
    assert n_pages % pps == 0 and page == LANES
    ns = n_pages // pps
    pool_blk = (1, 1, N_HEADS, HEAD_DIM, page)
    k_spec = lambda k: pl.BlockSpec(pool_blk, lambda b, s, pt, sl: (0, pt[b, jnp.minimum(s, ns - 1) * pps + k], 0, 0, 0))
    v_spec = lambda k: pl.BlockSpec(pool_blk, lambda b, s, pt, sl: (0, pt[b, jnp.maximum(s - ns, 0) * pps + k], 0, 0, 0))
    per_seq = lambda shape: pl.BlockSpec((1,) + shape, lambda b, s, pt, sl: (b, 0, 0))
    bias_spec = pl.BlockSpec((1, 1, pps * page), lambda b, s, pt, sl: (b, 0, jnp.minimum(s, ns - 1)))
    grid_spec = pltpu.PrefetchScalarGridSpec(
        num_scalar_prefetch=2,
        grid=(db, 2 * ns),
        in_specs=([per_seq((1, WIDTH))] * 3 + [bias_spec, per_seq((1, LANES))]
                  + [k_spec(k) for k in range(pps)] + [v_spec(k) for k in range(pps)]),
        out_specs=per_seq((1, WIDTH)),
        scratch_shapes=[pltpu.VMEM((ns, N_HEADS, pps * page), F32), pltpu.VMEM((N_HEADS, pps * page), F32),
                        pltpu.VMEM((N_HEADS, LANES), F32), pltpu.VMEM((N_HEADS, pps * page), F32),
                        pltpu.VMEM((N_HEADS, WIDTH), F32), pltpu.VMEM((N_HEADS, LANES), F32),
                        pltpu.VMEM((N_HEADS, WIDTH), BF)],
    )
    return pl.pallas_call(
        functools.partial(_paged_attention_kernel, page=page),
        grid_spec=grid_spec,
        out_shape=jax.ShapeDtypeStruct((db, 1, WIDTH), F32),
        compiler_params=_cp(("arbitrary", "arbitrary")),
        name="paged_attention",
    )(page_table, slopes, q_s, k_new, v_new, bias, bias_new, *([pool_k] * pps), *([pool_v] * pps))


def _moba_sample_kernel(pt_ref, slopes_ref, lg_ref, q_ref, kn_ref, vn_ref, bias_ref, *refs, page):
    pps = PAGES_PER_STEP
    v_refs = refs[:pps]
    o_ref, p_ref, ps_ref, acc_ref = refs[pps:]
    s = pl.program_id(1)
    ns = pl.num_programs(1) - 1
    wide = pps * page
    past = ns * wide
    ppb = MOBA_BLOCK // page

    @pl.when(s == 0)
    def _():
        hrow = lax.broadcasted_iota(I32, (N_HEADS, 1), 0)
        sl = jnp.zeros((N_HEADS, 1), F32)
        for h in range(N_HEADS):
            sl = jnp.where(hrow == h, slopes_ref[h], sl)
        qbd = _block_diag_rows(q_ref[0] * ATTN_SCALE).astype(BF)
        kn = kn_ref[0].astype(BF).astype(F32)
        ln = jnp.sum(qbd.astype(F32) * kn, axis=-1, keepdims=True)
        bias = bias_ref[0]
        mx = ln
        for k in range(ns):
            pos = k * wide + lax.broadcasted_iota(I32, (1, wide), 1)
            cols = [jnp.broadcast_to(bias[:, (k * pps + kk) // ppb:(k * pps + kk) // ppb + 1], (N_HEADS, page))
                    for kk in range(pps)]
            l = lg_ref[0, :, k * wide:(k + 1) * wide] - sl * (past - pos).astype(F32) + jnp.concatenate(cols, axis=1)
            p_ref[k] = l
            mx = jnp.maximum(mx, jnp.max(l, axis=-1, keepdims=True))
        pn = jnp.exp(ln - mx)
        tot = pn
        for k in range(ns):
            p = jnp.exp(p_ref[k] - mx)
            p_ref[k] = p
            tot = tot + jnp.sum(p, axis=-1, keepdims=True)
        ps_ref[...] = jnp.broadcast_to(tot, ps_ref.shape)
        acc_ref[...] = pn.astype(BF).astype(F32) * vn_ref[0].astype(BF).astype(F32)

    @pl.when(s > 0)
    def _():
        vt = jnp.concatenate([r[0, 0].reshape(WIDTH, page).astype(BF) for r in v_refs], axis=1)
        acc_ref[...] += _dot_nt(p_ref[s - 1].astype(BF), vt)

    @pl.when(s == ns)
    def _():
        o = acc_ref[...] / ps_ref[:, 0:1]
        hr = lax.broadcasted_iota(I32, (N_HEADS, WIDTH), 0)
        hl = lax.broadcasted_iota(I32, (N_HEADS, WIDTH), 1) // HEAD_DIM
        o_ref[0] = jnp.sum(jnp.where(hr == hl, o, 0.0), axis=0, keepdims=True)


def _moba_sample_attention(page_table, slopes, logits, q_s, k_new, v_new, bias_blk, pool_v):
    db, n_pages = page_table.shape
    page = pool_v.shape[4]
    pps = PAGES_PER_STEP
    assert n_pages % pps == 0 and page == LANES
    ns = n_pages // pps
    past = n_pages * page
    v_spec = lambda k: pl.BlockSpec((1, 1, N_HEADS, HEAD_DIM, page),
                                    lambda b, s, pt, sl: (0, pt[b, jnp.maximum(s - 1, 0) * pps + k], 0, 0, 0))
    per_seq = lambda shape: pl.BlockSpec((1,) + shape, lambda b, s, pt, sl: (b, 0, 0))
    grid_spec = pltpu.PrefetchScalarGridSpec(
        num_scalar_prefetch=2,
        grid=(db, ns + 1),
        in_specs=([per_seq((N_HEADS, past))] + [per_seq((1, WIDTH))] * 3 + [per_seq((N_HEADS, LANES))]
                  + [v_spec(k) for k in range(pps)]),
        out_specs=per_seq((1, WIDTH)),
        scratch_shapes=[pltpu.VMEM((ns, N_HEADS, pps * page), F32), pltpu.VMEM((N_HEADS, LANES), F32),
                        pltpu.VMEM((N_HEADS, WIDTH), F32)],
    )
    return pl.pallas_call(
        functools.partial(_moba_sample_kernel, page=page),
        grid_spec=grid_spec,
        out_shape=jax.ShapeDtypeStruct((db, 1, WIDTH), F32),
        compiler_params=_cp(("arbitrary", "arbitrary")),
        name="moba_sample_attention",
    )(page_table, slopes, logits, q_s, k_new, v_new, bias_blk, *([pool_v] * pps))


def _alibi_slopes():
    n = 2 * N_HEADS
    i = jnp.arange(1, n + 1, dtype=F32)
    m = jnp.exp2(-8.0 * i / n)
    return m[0::2], m[1::2]


def kernel(x_prompt, x_sample, cache_moba_k, cache_moba_v, cache_dsa_k, cache_dsa_v, cache_dsa_kidx, state_ffn_conv, page_table, c_prompt, c_sample, w_ada, b_ada, g_attn, w_in, qn_a, kn_a, qn_b, kn_b, w_out, g_ffn, w_up, conv_w, conv_b, w_down):
    bsz, t, d = x_prompt.shape
    db, ds, _ = x_sample.shape
    depth = w_in.shape[0]
    assert depth == 1 and ds == 1 and t % SB == 0
    f = w_down.shape[1]
    nchunk = f // FF_CHUNK
    assert nchunk * FF_CHUNK == f
    slopes_a, slopes_b = _alibi_slopes()
    l = 0

    ncols = N_QKV_CHUNKS * WIDTH
    wqkv = w_in[l][:, :ncols].astype(BF)
    wkw = jnp.pad(w_in[l][:, ncols:], ((0, 0), (0, LANES - (w_in.shape[2] - ncols)))).astype(BF)
    nrm = jnp.stack([jnp.tile(g[l], N_HEADS) for g in (qn_a, kn_a, qn_b, kn_b)])
    hid = jnp.arange(WIDTH) // HEAD_DIM
    gmat = (hid[:, None] == hid[None, :]).astype(BF)
    wout = w_out[l].astype(BF)
    wupa = w_up[l][:, :f].astype(BF).reshape(d, nchunk, FF_CHUNK).transpose(1, 0, 2)
    wupg = w_up[l][:, f:].astype(BF).reshape(d, nchunk, FF_CHUNK).transpose(1, 0, 2)
    wdn = w_down[l].astype(BF).reshape(nchunk, FF_CHUNK, d)
    g_attn_l = g_attn[l].reshape(1, d)
    g_ffn_l = g_ffn[l].reshape(1, d)
    cb = conv_b[l].reshape(1, f)

    rc = bsz + db
    rpad = -rc % SUBLANES
    c_all = jnp.concatenate([c_prompt, c_sample, jnp.zeros((rpad, d), F32)], axis=0)
    mod = _modulation(c_all, w_ada[l], b_ada[l])
    mods_p = [m.reshape(bsz, 1, d) for m in jnp.split(mod[:bsz], N_MOD, axis=-1)]
    mods_s = [m.reshape(1, db, d) for m in jnp.split(mod[bsz:rc], N_MOD, axis=-1)]

    n = bsz * t
    xp2 = x_prompt.reshape(n, d)
    tm = 512
    (qa_bf, qb_bf, qi, kw, kat, vat, kbt, vbt, kat_bf, vat_bf, kbt_bf, vbt_bf, kwt_bf, kit, kmean) = _pre_project_prompt(
        xp2, mods_p[0], mods_p[1], g_attn_l, wqkv, wkw, nrm, gmat, tm=tm, bsz=bsz, t=t)
    nb = t // TQ
    assert nb <= LANES
    kmt = kmean.reshape(bsz, nb, WIDTH).transpose(0, 2, 1).astype(BF)
    kmt = jnp.pad(kmt, ((0, 0), (0, 0), (0, LANES - nb)))
    o_a = _moba_prompt(slopes_a, qa_bf, kat_bf, vat_bf, kmt, bsz, t)
    n_sel = min(DSA_TOPK, t // 4)
    o_b = _dsa_prompt(slopes_b, qi, kw, kwt_bf, qb_bf, kbt_bf, vbt_bf, n_sel, bsz, t)
    y_p, conv_p = _finish(xp2, o_a, o_b, mods_p[2:], g_ffn_l, wout, wupa, wupg, conv_w[l], cb, wdn,
                          tm=tm, rows_per_group=t)

    kv_out = lambda a: a.reshape(1, bsz, N_HEADS, HEAD_DIM, t).transpose(0, 1, 4, 2, 3)
    outs_p = (kv_out(kat), kv_out(vat), kv_out(kbt), kv_out(vbt),
              kit.reshape(1, bsz, IDX_DIM, t).transpose(0, 1, 3, 2), conv_p[None])

    page = cache_moba_k.shape[2]
    n_pages = page_table.shape[1]
    past = n_pages * page
    assert past % MOBA_BLOCK == 0 and db % SUBLANES == 0 and cache_moba_k.shape[0] == 1
    xs2 = x_sample.reshape(db, d)
    qa_s, ka_s, va_s, qb_s, kb_s, vb_s, qi_s, kw_s = _pre_project(
        xs2, mods_s[0], mods_s[1], g_attn_l, wqkv, wkw, nrm, gmat, tm=db, rows_per_group=db, with_kmean=False)
    pool = lambda cch: cch.transpose(0, 1, 3, 4, 2)
    pool_kidx = cache_dsa_kidx.transpose(0, 1, 3, 2)
    row3 = lambda a: a.reshape(db, 1, a.shape[-1])
    scores, score_new, bias_blk, logits_a = _sample_stats(
        page_table, qi_s.reshape(db, IDX_HEADS, IDX_DIM), kw_s[:, IDX_DIM:IDX_DIM + IDX_HEADS].reshape(db, IDX_HEADS, 1),
        row3(kw_s[:, :IDX_DIM]), row3(qa_s), pool_kidx, pool(cache_moba_k))
    n_sel_s = min(DSA_TOPK, (past + ds) // 4)
    bias_pos, bias_new = _dsa_sample_select(scores.reshape(db, past), score_new.reshape(db, LANES), n_sel_s)
    o_a_s = _moba_sample_attention(page_table, slopes_a, logits_a, row3(qa_s), row3(ka_s), row3(va_s), bias_blk,
                                   pool(cache_moba_v))
    o_b_s = _paged_attention(page_table, slopes_b, row3(qb_s), row3(kb_s), row3(vb_s),
                             bias_pos.reshape(db, 1, past), bias_new.reshape(db, 1, LANES),
                             pool(cache_dsa_k), pool(cache_dsa_v))
    state = state_ffn_conv[l]
    y_s, a_s = _finish(xs2, o_a_s.reshape(db, WIDTH), o_b_s.reshape(db, WIDTH), mods_s[2:], g_ffn_l, wout, wupa, wupg,
                       conv_w[l], cb, wdn, tm=db, rows_per_group=db, prefix=(state[:, 0], state[:, 1]))
    conv_s = jnp.stack([state[:, 1], a_s], axis=1)
    kvs_shape = (1, db, ds, N_HEADS, HEAD_DIM)
    outs_s = (ka_s.reshape(kvs_shape), va_s.reshape(kvs_shape), kb_s.reshape(kvs_shape), vb_s.reshape(kvs_shape),
              kw_s[:, :IDX_DIM].reshape(1, db, ds, IDX_DIM), conv_s[None])
    return (y_p.reshape(bsz, t, d), y_s.reshape(db, ds, d)) + outs_p + outs_s
```
